```python
import jax, jax.numpy as jnp
from jax import lax
import numpy as np

D_MODEL = 1024
BATCH = 4
SEQ = 8192
DEPTH = 2

N_MIXERS = 2
N_A = (DEPTH + 1) // 2
N_B = DEPTH // 2

M_HEADS = 8
M_QK_DIM = 64
M_V_DIM = D_MODEL // M_HEADS
M_CHUNK = 64
M_F_BIAS_LO = 3.0
M_F_BIAS_HI = 6.0
M_IN_WIDTH = 2 * M_HEADS * M_QK_DIM + 2 * D_MODEL + 4 * M_HEADS

DILATED_GROUPS = ((128, 1), (512, 4), (2048, 16))
N_GROUPS = len(DILATED_GROUPS)
A_HEADS = 8
A_HEAD_DIM = D_MODEL // A_HEADS
A_IN_WIDTH = N_GROUPS * 3 * A_HEADS * A_HEAD_DIM
ROPE_DIM = A_HEAD_DIM // 4
ROPE_THETA = 500000.0
NEG_INF = -1e30

FFN_HIDDEN = -(-(8 * D_MODEL) // (3 * 256)) * 256
PLE_DIM = 256
EPS = 1e-6

kernel_name = "hybrid_mlstm_dilated_attn_encoder"


def rms_norm(x, w):
    xf = x.astype(jnp.float32)
    y = xf * lax.rsqrt(jnp.mean(xf * xf, axis=-1, keepdims=True) + EPS)
    return (y * w.astype(jnp.float32)).astype(x.dtype)


def mlstm_chunkwise(q, k, v, ig, lf):
    B, S, H, Dk = q.shape
    Dv = v.shape[-1]
    L = M_CHUNK
    nc = S // L
    qc = q.reshape(B, nc, L, H, Dk)
    kc = k.reshape(B, nc, L, H, Dk)
    vc = v.reshape(B, nc, L, H, Dv)
    igc = ig.reshape(B, nc, L, H)
    b = jnp.cumsum(lf.reshape(B, nc, L, H), axis=2)
    g = b[:, :, -1]
    a = g[:, :, None, :] - b + igc
    a_max = jnp.max(a, axis=2)

    def step(carry, inp):
        C, n, m = carry
        k_c, v_c, a_c, amax_c, g_c = inp
        m_new = jnp.maximum(g_c + m, amax_c)
        decay = jnp.exp(g_c + m - m_new)
        kw = k_c * jnp.exp(a_c - m_new[:, None, :])[..., None]
        C_new = decay[..., None, None] * C + jnp.einsum('blhk,blhv->bhkv', kw, v_c)
        n_new = decay[..., None] * n + jnp.sum(kw, axis=1)
        return (C_new, n_new, m_new), (C, n, m)

    init = (jnp.zeros((B, H, Dk, Dv), jnp.float32), jnp.zeros((B, H, Dk), jnp.float32),
            jnp.zeros((B, H), jnp.float32))
    xs = (jnp.moveaxis(kc, 1, 0), jnp.moveaxis(vc, 1, 0), jnp.moveaxis(a, 1, 0),
          jnp.moveaxis(a_max, 1, 0), jnp.moveaxis(g, 1, 0))
    _, (C_prev, n_prev, m_prev) = lax.scan(step, init, xs)
    C_prev = jnp.moveaxis(C_prev, 0, 1)
    n_prev = jnp.moveaxis(n_prev, 0, 1)
    m_prev = jnp.moveaxis(m_prev, 0, 1)

    bT = jnp.swapaxes(b, 2, 3)
    igT = jnp.swapaxes(igc, 2, 3)
    D = bT[..., :, None] - bT[..., None, :] + igT[..., None, :]
    tril = jnp.tril(jnp.ones((L, L), dtype=bool))
    D = jnp.where(tril, D, -jnp.inf)
    inter_log = bT + m_prev[..., None]
    m_j = jnp.maximum(inter_log, jnp.max(D, axis=-1))
    Dw = jnp.exp(D - m_j[..., None])
    inter_w = jnp.exp(inter_log - m_j)
    Sm = jnp.einsum('bnlhk,bnshk->bnhls', qc, kc) * Dw
    num = (jnp.einsum('bnhls,bnshv->bnlhv', Sm, vc)
           + jnp.swapaxes(inter_w, 2, 3)[..., None] * jnp.einsum('bnlhk,bnhkv->bnlhv', qc, C_prev))
    den = jnp.sum(Sm, axis=-1) + inter_w * jnp.einsum('bnlhk,bnhk->bnhl', qc, n_prev)
    denom = jnp.maximum(jnp.abs(den), jnp.exp(-m_j))
    h = num / jnp.swapaxes(denom, 2, 3)[..., None]
    return h.reshape(B, S, H, Dv)


def mlstm_mixer(xn, w_in, gate_bias, head_norm, w_out):
    B, S, _ = xn.shape
    H, Dk, Dv = M_HEADS, M_QK_DIM, M_V_DIM
    f32 = jnp.float32
    proj = xn @ w_in
    q, k, v, o, gates = jnp.split(
        proj, [H * Dk, 2 * H * Dk, 2 * H * Dk + D_MODEL, 2 * H * Dk + 2 * D_MODEL], axis=-1)
    q = q.astype(f32).reshape(B, S, H, Dk) * (Dk ** -0.5)
    k = k.astype(f32).reshape(B, S, H, Dk)
    v = v.astype(f32).reshape(B, S, H, Dv)
    gates = (gates.astype(f32) + gate_bias.astype(f32)).reshape(B, S, 4, H)
    ig_f, ig_b = gates[:, :, 0], gates[:, :, 1]
    lf_f = jax.nn.log_sigmoid(gates[:, :, 2])
    lf_b = jax.nn.log_sigmoid(gates[:, :, 3])
    h_fwd = mlstm_chunkwise(q, k, v, ig_f, lf_f)
    flip = lambda t: jnp.flip(t, axis=1)
    h_bwd = flip(mlstm_chunkwise(flip(q), flip(k), flip(v), flip(ig_b), flip(lf_b)))
    h = h_fwd + h_bwd
    h = h * lax.rsqrt(jnp.mean(h * h, axis=-1, keepdims=True) + EPS) * head_norm.astype(f32)
    h = jax.nn.sigmoid(o.astype(f32)) * h.reshape(B, S, H * Dv)
    return h.astype(xn.dtype) @ w_out


def apply_partial_rope(x, cos, sin):
    half = ROPE_DIM // 2
    x1 = x[..., :half]
    x2 = x[..., half:ROPE_DIM]
    rot = jnp.concatenate([x1 * cos - x2 * sin, x2 * cos + x1 * sin], axis=-1).astype(x.dtype)
    return jnp.concatenate([rot, x[..., ROPE_DIM:]], axis=-1)


def dilated_window_attention(q, k, v, dilation, radius):
    B, S, H, Dh = q.shape
    blk = radius
    U = S // dilation
    nb = -(-U // blk)
    Up = nb * blk

    def to_sub(t, front, back):
        t = t.reshape(B, U, dilation, H, Dh)
        return jnp.pad(t, ((0, 0), (front, back), (0, 0), (0, 0), (0, 0)))

    qb = to_sub(q, 0, Up - U).reshape(B, nb, blk, dilation, H, Dh)
    kb = to_sub(k, blk, Up - U + blk).reshape(B, nb + 2, blk, dilation, H, Dh)
    vb = to_sub(v, blk, Up - U + blk).reshape(B, nb + 2, blk, dilation, H, Dh)
    scores = jnp.concatenate(
        [jnp.einsum('bnqrhd,bnkrhd->bnrhqk', qb, kb[:, s:s + nb]) for s in range(3)], axis=-1)
    scores = scores.astype(jnp.float32) * (Dh ** -0.5)
    n_idx = jnp.arange(nb)[:, None, None]
    a_idx = jnp.arange(blk)[None, :, None]
    c_idx = jnp.arange(3 * blk)[None, None, :]
    delta = c_idx - blk - a_idx
    key_u = n_idx * blk - blk + c_idx
    valid = (jnp.abs(delta) <= radius) & (key_u >= 0) & (key_u < U)
    scores = jnp.where(valid[None, :, None, None], scores, NEG_INF)
    lse = jax.nn.logsumexp(scores, axis=-1)
    probs = jnp.exp(scores - lse[..., None]).astype(v.dtype)
    out = jnp.einsum('bnrhqk,bnkrhd->bnqrhd', probs[..., :blk], vb[:, 0:nb])
    for s in range(1, 3):
        out = out + jnp.einsum('bnrhqk,bnkrhd->bnqrhd', probs[..., s * blk:(s + 1) * blk], vb[:, s:s + nb])
    out = out.reshape(B, Up, dilation, H, Dh)[:, :U].reshape(B, S, H, Dh)
    lse = jnp.transpose(lse, (0, 1, 4, 2, 3)).reshape(B, Up, dilation, H)[:, :U].reshape(B, S, H)
    return out, lse


def dilated_mixer(xn, cos, sin, w_in, w_out):
    B, S, _ = xn.shape
    proj = (xn @ w_in).reshape(B, S, N_GROUPS, 3, A_HEADS, A_HEAD_DIM)
    outs, lses = [], []
    for g, (window, dil) in enumerate(DILATED_GROUPS):
        q = apply_partial_rope(proj[:, :, g, 0], cos, sin)
        k = apply_partial_rope(proj[:, :, g, 1], cos, sin)
        o_g, l_g = dilated_window_attention(q, k, proj[:, :, g, 2], dil, window // 2 // dil)
        outs.append(o_g)
        lses.append(l_g)
    w = jax.nn.softmax(jnp.stack(lses, axis=0), axis=0)
    o = jnp.einsum('gbsh,gbshd->bshd', w, jnp.stack(outs, axis=0).astype(jnp.float32))
    return o.reshape(B, S, A_HEADS * A_HEAD_DIM).astype(xn.dtype) @ w_out


def swiglu(xn, w_gate, w_up, w_down):
    return (jax.nn.silu(xn @ w_gate) * (xn @ w_up)) @ w_down


def setup_inputs(seed: int = 0) -> dict:
    key = jax.random.key(seed)
    ks = jax.random.split(key, 24)
    f32 = jnp.float32
    nrm = lambda k, shape, scale: jax.random.normal(k, shape, f32) * scale
    x = nrm(ks[0], (BATCH, SEQ, D_MODEL), 1.0)
    p = nrm(ks[1], (DEPTH, BATCH, SEQ, PLE_DIM), 1.0)
    positions = (jnp.arange(SEQ, dtype=jnp.int32)[None, :]
                 + jax.random.randint(ks[2], (BATCH, 1), 0, 4096, dtype=jnp.int32))
    norm_mix = 1.0 + nrm(ks[3], (DEPTH, D_MODEL), 0.01)
    a_w_in = nrm(ks[4], (N_A, D_MODEL, M_IN_WIDTH), D_MODEL ** -0.5)
    ig_bias = nrm(ks[5], (N_A, 2 * M_HEADS), 0.1)
    fg_bias = (jnp.tile(jnp.linspace(M_F_BIAS_LO, M_F_BIAS_HI, M_HEADS, dtype=f32), (N_A, 2))
               + nrm(ks[6], (N_A, 2 * M_HEADS), 0.01))
    a_gate_bias = jnp.concatenate([ig_bias, fg_bias], axis=-1)
    a_head_norm = 1.0 + nrm(ks[7], (N_A, M_HEADS, M_V_DIM), 0.01)
    a_w_out = nrm(ks[8], (N_A, D_MODEL, D_MODEL), D_MODEL ** -0.5)
    b_w_in = nrm(ks[9], (N_B, D_MODEL, A_IN_WIDTH), D_MODEL ** -0.5)
    b_w_out = nrm(ks[10], (N_B, A_HEADS * A_HEAD_DIM, D_MODEL), (A_HEADS * A_HEAD_DIM) ** -0.5)
    norm_ffn = 1.0 + nrm(ks[11], (DEPTH, D_MODEL), 0.01)
    w_gate = nrm(ks[12], (DEPTH, D_MODEL, FFN_HIDDEN), D_MODEL ** -0.5)
    w_up = nrm(ks[13], (DEPTH, D_MODEL, FFN_HIDDEN), D_MODEL ** -0.5)
    w_down = nrm(ks[14], (DEPTH, FFN_HIDDEN, D_MODEL), FFN_HIDDEN ** -0.5)
    norm_ple = 1.0 + nrm(ks[15], (DEPTH, D_MODEL), 0.01)
    ple_gate = nrm(ks[16], (DEPTH, D_MODEL, D_MODEL), D_MODEL ** -0.5)
    ple_proj = nrm(ks[17], (DEPTH, PLE_DIM, D_MODEL), PLE_DIM ** -0.5)
    final_norm = 1.0 + nrm(ks[18], (D_MODEL,), 0.01)
    return {"x": x, "p": p, "positions": positions, "norm_mix": norm_mix,
            "a_w_in": a_w_in, "a_gate_bias": a_gate_bias, "a_head_norm": a_head_norm, "a_w_out": a_w_out,
            "b_w_in": b_w_in, "b_w_out": b_w_out, "norm_ffn": norm_ffn,
            "w_gate": w_gate, "w_up": w_up, "w_down": w_down,
            "norm_ple": norm_ple, "ple_gate": ple_gate, "ple_proj": ple_proj, "final_norm": final_norm}


def reference(x, p, positions, norm_mix, a_w_in, a_gate_bias, a_head_norm, a_w_out, b_w_in, b_w_out,
              norm_ffn, w_gate, w_up, w_down, norm_ple, ple_gate, ple_proj, final_norm):
    inv_freq = ROPE_THETA ** (-jnp.arange(0, ROPE_DIM, 2, dtype=jnp.float32) / ROPE_DIM)
    angles = positions.astype(jnp.float32)[..., None] * inv_freq
    cos = jnp.cos(angles)[:, :, None, :]
    sin = jnp.sin(angles)[:, :, None, :]
    h = x
    for i in range(DEPTH):
        j = i // N_MIXERS
        hn = rms_norm(h, norm_mix[i])
        if i % N_MIXERS == 0:
            mix = mlstm_mixer(hn, a_w_in[j], a_gate_bias[j], a_head_norm[j], a_w_out[j])
        else:
            mix = dilated_mixer(hn, cos, sin, b_w_in[j], b_w_out[j])
        h = h + mix
        h = h + swiglu(rms_norm(h, norm_ffn[i]), w_gate[i], w_up[i], w_down[i])
        gate = jax.nn.sigmoid(rms_norm(h, norm_ple[i]) @ ple_gate[i])
        h = h + gate * (p[i] @ ple_proj[i])
    return rms_norm(h, final_norm)
```

```python
import functools

import jax
import jax.numpy as jnp
from jax import lax
from jax.experimental import pallas as pl
from jax.experimental.pallas import tpu as pltpu

F32 = jnp.float32
BF16 = jnp.bfloat16

D_MODEL = 1024
LANES = 128
EPS = 1e-6

M_HEADS = 8
M_QK = 64
M_V = 128
M_CHUNK = 64
M_QK_ALL = M_HEADS * M_QK

A_HEADS = 8
A_DH = 128
RADIUS = 64
DILATED_GROUPS = ((128, 1), (512, 4), (2048, 16))
N_GROUPS = 3
ROPE_DIM = 32
ROPE_HALF = 16
ROPE_THETA = 500000.0
NEG_INF = -1e30

FFN_HIDDEN = 2816
PLE_DIM = 256

VMEM_LIMIT = 56 * 1024 * 1024


def _dot(a, b):
    return jnp.dot(a, b, preferred_element_type=F32)


def _dot_nt(a, b):
    return lax.dot_general(a, b, (((1,), (1,)), ((), ())), preferred_element_type=F32)


def _rms(x, w):
    ms = jnp.mean(x * x, axis=-1, keepdims=True)
    return x * lax.rsqrt(ms + EPS) * w


def _const_spec(shape):
    nd = len(shape)
    return pl.BlockSpec(shape, lambda *_: (0,) * nd, pipeline_mode=pl.Buffered(1))


def _params(sem):
    return pltpu.CompilerParams(dimension_semantics=sem, vmem_limit_bytes=VMEM_LIMIT)


def _in_proj0_kernel(x_ref, nw_ref, w_ref, wg_ref, gb_ref, trif_ref, trib_ref,
                     q_ref, k_ref, v_ref, o_ref, g_ref):
    hn = _rms(x_ref[...], nw_ref[...]).astype(BF16)
    q_ref[...] = (_dot(hn, w_ref[:, 0:M_QK_ALL]) * (M_QK ** -0.5)).astype(BF16)
    k_ref[...] = _dot(hn, w_ref[:, M_QK_ALL:2 * M_QK_ALL]).astype(BF16)
    v_ref[...] = _dot(hn, w_ref[:, 2 * M_QK_ALL:2 * M_QK_ALL + D_MODEL]).astype(BF16)
    o_ref[...] = _dot(hn, w_ref[:, 2 * M_QK_ALL + D_MODEL:]).astype(BF16)
    gates = _dot(hn, wg_ref[...]) + gb_ref[...]
    fg = gates[:, 2 * M_HEADS:]
    lf = jnp.minimum(fg, 0.0) - jnp.log1p(jnp.exp(-jnp.abs(fg)))
    h1 = lf.astype(BF16)
    r1 = lf - h1.astype(F32)
    h2 = r1.astype(BF16)
    h3 = (r1 - h2.astype(F32)).astype(BF16)
    trif = trif_ref[...]
    trib = trib_ref[...]
    cf = _dot(trif, h1) + _dot(trif, h2) + _dot(trif, h3)
    cb = _dot(trib, h1) + _dot(trib, h2) + _dot(trib, h3)
    g_ref[...] = jnp.concatenate([gates[:, :2 * M_HEADS], cf[:, :M_HEADS], cb[:, M_HEADS:]], axis=-1)


def _chunk_tri(tm, reverse):
    r = jnp.arange(tm)[:, None]
    c = jnp.arange(tm)[None, :]
    same = (r // M_CHUNK) == (c // M_CHUNK)
    tri = (c >= r) if reverse else (c <= r)
    return (same & tri).astype(BF16)


def _in_proj0(x2, nw, w_main, w_gate, gate_bias, tm):
    T = x2.shape[0]
    row = lambda n: pl.BlockSpec((tm, n), lambda i: (i, 0))
    return pl.pallas_call(
        _in_proj0_kernel,
        grid=(T // tm,),
        in_specs=[row(D_MODEL), _const_spec((1, D_MODEL)), _const_spec(w_main.shape),
                  _const_spec(w_gate.shape), _const_spec((1, 4 * M_HEADS)),
                  _const_spec((tm, tm)), _const_spec((tm, tm))],
        out_specs=[row(M_QK_ALL), row(M_QK_ALL), row(D_MODEL), row(D_MODEL), row(4 * M_HEADS)],
        out_shape=[jax.ShapeDtypeStruct((T, M_QK_ALL), BF16), jax.ShapeDtypeStruct((T, M_QK_ALL), BF16),
                   jax.ShapeDtypeStruct((T, D_MODEL), BF16), jax.ShapeDtypeStruct((T, D_MODEL), BF16),
                   jax.ShapeDtypeStruct((T, 4 * M_HEADS), F32)],
        compiler_params=_params(("parallel",)),
        name="in_proj0",
    )(x2, nw, w_main, w_gate, gate_bias, _chunk_tri(tm, False), _chunk_tri(tm, True))


def _mlstm_chunk(q_ref, k_ref, v_ref, gc_ref, gr_ref, h_ref, s_ref, m_ref, sub, fwd):
    L = M_CHUNK
    r0 = pl.multiple_of(sub * L, L)
    q = q_ref[pl.ds(r0, L), :]
    kT = k_ref[pl.ds(r0, L), :].astype(F32).T
    v = v_ref[pl.ds(r0, L), :]
    gc = gc_ref[pl.ds(r0, L), :]
    gr = gr_ref[sub]
    io = 0 if fwd else M_HEADS
    bo = 2 * M_HEADS if fwd else 3 * M_HEADS
    li = lax.broadcasted_iota(jnp.int32, (L, L), 0)
    si = lax.broadcasted_iota(jnp.int32, (L, L), 1)
    mask = (si <= li) if fwd else (si >= li)
    ones = jnp.ones((L, LANES), BF16)
    for h in range(M_HEADS):
        b_col = gc[:, bo + h:bo + h + 1]
        b_row = gr[bo + h:bo + h + 1, :]
        ig_row = gr[io + h:io + h + 1, :]
        g = b_row[:, L - 1:L] if fwd else b_row[:, 0:1]
        m_prev = m_ref[h]
        s_prev = s_ref[h]
        a_row = g - b_row + ig_row
        m_new = jnp.maximum(g + m_prev, jnp.max(a_row, axis=1, keepdims=True))
        decay = jnp.exp(g + m_prev - m_new)
        k_scale = jnp.exp(a_row - m_new)
        dmat = jnp.where(mask, b_col - b_row + ig_row, -jnp.inf)
        inter = b_col + m_prev
        m_j = jnp.maximum(inter, jnp.max(dmat, axis=1, keepdims=True))
        dw = jnp.exp(dmat - m_j)
        iw = jnp.exp(inter - m_j)
        qh = q[:, h * M_QK:(h + 1) * M_QK]
        kTh = kT[h * M_QK:(h + 1) * M_QK, :]
        sm = (_dot(qh, kTh.astype(BF16)) * dw).astype(BF16)
        qi = (qh.astype(F32) * iw).astype(BF16)
        v_ext = jnp.concatenate([v[:, h * M_V:(h + 1) * M_V], ones], axis=1)
        out = _dot(sm, v_ext) + _dot(qi, s_prev.astype(BF16))
        denom = jnp.maximum(jnp.abs(out[:, M_V:]), jnp.exp(-m_j))
        h_ref[pl.ds(r0, L), h * M_V:(h + 1) * M_V] = out[:, :M_V] / denom
        kw = (kTh * k_scale).astype(BF16)
        s_ref[h] = decay * s_prev + _dot(kw, v_ext)
        m_ref[h] = m_new


def _mlstm_kernel(qf, kf, vf, gcf, grf, qb, kb, vb, gcb, grb, hf_ref, hb_ref, sf, sb, mf, mb, *, nsub):
    @pl.when(pl.program_id(1) == 0)
    def _():
        sf[...] = jnp.zeros_like(sf)
        sb[...] = jnp.zeros_like(sb)
        mf[...] = jnp.zeros_like(mf)
        mb[...] = jnp.zeros_like(mb)

    def body(j, carry):
        _mlstm_chunk(qf, kf, vf, gcf, grf, hf_ref, sf, mf, j, True)
        _mlstm_chunk(qb, kb, vb, gcb, grb, hb_ref, sb, mb, nsub - 1 - j, False)
        return carry

    lax.fori_loop(0, nsub, body, 0)


def _mlstm(q, k, v, gcol, grow, tb):
    B, S, _ = q.shape
    n = S // tb
    nsub = tb // M_CHUNK
    fw = lambda b, c: (b, c, 0)
    bw = lambda b, c: (b, n - 1 - c, 0)
    fw4 = lambda b, c: (b, c, 0, 0)
    bw4 = lambda b, c: (b, n - 1 - c, 0, 0)

    def specs(im, im4):
        return [pl.BlockSpec((None, tb, M_QK_ALL), im), pl.BlockSpec((None, tb, M_QK_ALL), im),
                pl.BlockSpec((None, tb, D_MODEL), im), pl.BlockSpec((None, tb, 4 * M_HEADS), im),
                pl.BlockSpec((None, nsub, 4 * M_HEADS, M_CHUNK), im4)]

    return pl.pallas_call(
        functools.partial(_mlstm_kernel, nsub=nsub),
        grid=(B, n),
        in_specs=specs(fw, fw4) + specs(bw, bw4),
        out_specs=[pl.BlockSpec((None, tb, D_MODEL), fw), pl.BlockSpec((None, tb, D_MODEL), bw)],
        out_shape=[jax.ShapeDtypeStruct((B, S, D_MODEL), F32)] * 2,
        scratch_shapes=[pltpu.VMEM((M_HEADS, M_QK, 2 * M_V), F32), pltpu.VMEM((M_HEADS, M_QK, 2 * M_V), F32),
                        pltpu.VMEM((M_HEADS, 1, 1), F32), pltpu.VMEM((M_HEADS, 1, 1), F32)],
        compiler_params=_params(("parallel", "arbitrary")),
        name="mlstm",
    )(q, k, v, gcol, grow, q, k, v, gcol, grow)


def _tail_common(mix_in, x_ref, w_out_ref, nf_ref, wg_ref, wu_ref, wd_ref, npl_ref, pg_ref, p_ref, pp_ref):
    h1 = x_ref[...] + _dot(mix_in, w_out_ref[...])
    hn = _rms(h1, nf_ref[...]).astype(BF16)
    gate = _dot(hn, wg_ref[...])
    act = (gate * jax.nn.sigmoid(gate) * _dot(hn, wu_ref[...])).astype(BF16)
    h2 = h1 + _dot(act, wd_ref[...])
    pgate = jax.nn.sigmoid(_dot(_rms(h2, npl_ref[...]).astype(BF16), pg_ref[...]))
    return h2 + pgate * _dot(p_ref[...].astype(BF16), pp_ref[...])


def _tail0_kernel(hf_ref, hb_ref, o_ref, hnorm_ref, x_ref, w_out_ref, nf_ref, wg_ref, wu_ref, wd_ref,
                  npl_ref, pg_ref, p_ref, pp_ref, nnext_ref, out_ref, hn_ref, mix_ref):
    for h in range(M_HEADS):
        sl = slice(h * M_V, (h + 1) * M_V)
        hh = hf_ref[:, sl] + hb_ref[:, sl]
        hh = hh * lax.rsqrt(jnp.mean(hh * hh, axis=-1, keepdims=True) + EPS) * hnorm_ref[:, sl]
        mix_ref[:, sl] = (jax.nn.sigmoid(o_ref[:, sl].astype(F32)) * hh).astype(BF16)
    h3 = _tail_common(mix_ref[...], x_ref, w_out_ref, nf_ref, wg_ref, wu_ref, wd_ref, npl_ref, pg_ref,
                      p_ref, pp_ref)
    out_ref[...] = h3
    hn_ref[...] = _rms(h3, nnext_ref[...]).astype(BF16)


def _tail1_kernel(o0_ref, o1_ref, o2_ref, l0_ref, l1_ref, l2_ref, x_ref, w_out_ref, nf_ref, wg_ref, wu_ref,
                  wd_ref, npl_ref, pg_ref, p_ref, pp_ref, nfin_ref, out_ref, mix_ref):
    l0 = l0_ref[...]
    l1 = l1_ref[...]
    l2 = l2_ref[...]
    mx = jnp.maximum(jnp.maximum(l0, l1), l2)
    e0 = jnp.exp(l0 - mx)
    e1 = jnp.exp(l1 - mx)
    e2 = jnp.exp(l2 - mx)
    inv = 1.0 / (e0 + e1 + e2)
    w0 = e0 * inv
    w1 = e1 * inv
    w2 = e2 * inv
    for h in range(A_HEADS):
        sl = slice(h * A_DH, (h + 1) * A_DH)
        mix_ref[:, sl] = (w0[:, h:h + 1] * o0_ref[:, sl].astype(F32)
                          + w1[:, h:h + 1] * o1_ref[:, sl].astype(F32)
                          + w2[:, h:h + 1] * o2_ref[:, sl].astype(F32)).astype(BF16)
    h3 = _tail_common(mix_ref[...], x_ref, w_out_ref, nf_ref, wg_ref, wu_ref, wd_ref, npl_ref, pg_ref,
                      p_ref, pp_ref)
    out_ref[...] = _rms(h3, nfin_ref[...])


def _tail_weight_specs():
    return [_const_spec((D_MODEL, D_MODEL)), _const_spec((1, D_MODEL)),
            _const_spec((D_MODEL, FFN_HIDDEN)), _const_spec((D_MODEL, FFN_HIDDEN)),
            _const_spec((FFN_HIDDEN, D_MODEL)), _const_spec((1, D_MODEL)), _const_spec((D_MODEL, D_MODEL))]


def _tail0(hf, hb, o, hnorm, x2, w_out, nf, wg, wu, wd, npl, pg, p2, pproj, nnext, tm):
    T = x2.shape[0]
    row = lambda n: pl.BlockSpec((tm, n), lambda i: (i, 0))
    return pl.pallas_call(
        _tail0_kernel,
        grid=(T // tm,),
        in_specs=[row(D_MODEL), row(D_MODEL), row(D_MODEL), _const_spec((1, D_MODEL)), row(D_MODEL)]
        + _tail_weight_specs() + [row(PLE_DIM), _const_spec((PLE_DIM, D_MODEL)), _const_spec((1, D_MODEL))],
        out_specs=[row(D_MODEL), row(D_MODEL)],
        out_shape=[jax.ShapeDtypeStruct((T, D_MODEL), F32), jax.ShapeDtypeStruct((T, D_MODEL), BF16)],
        scratch_shapes=[pltpu.VMEM((tm, D_MODEL), BF16)],
        compiler_params=_params(("parallel",)),
        name="tail0",
    )(hf, hb, o, hnorm, x2, w_out, nf, wg, wu, wd, npl, pg, p2, pproj, nnext)


def _tail1(o0, o1, o2, l0, l1, l2, x2, w_out, nf, wg, wu, wd, npl, pg, p2, pproj, nfin, tm):
    T = x2.shape[0]
    row = lambda n: pl.BlockSpec((tm, n), lambda i: (i, 0))
    return pl.pallas_call(
        _tail1_kernel,
        grid=(T // tm,),
        in_specs=[row(D_MODEL)] * 3 + [row(LANES)] * 3 + [row(D_MODEL)]
        + _tail_weight_specs() + [row(PLE_DIM), _const_spec((PLE_DIM, D_MODEL)), _const_spec((1, D_MODEL))],
        out_specs=row(D_MODEL),
        out_shape=jax.ShapeDtypeStruct((T, D_MODEL), F32),
        scratch_shapes=[pltpu.VMEM((tm, D_MODEL), BF16)],
        compiler_params=_params(("parallel",)),
        name="tail1",
    )(o0, o1, o2, l0, l1, l2, x2, w_out, nf, wg, wu, wd, npl, pg, p2, pproj, nfin)


def _rope_tab_kernel(pos_ref, invf_ref, cos_ref, sin_ref):
    ang = pos_ref[...].astype(F32) * invf_ref[...]
    lane = lax.broadcasted_iota(jnp.int32, ang.shape, 1)
    s = jnp.sin(ang)
    cos_ref[...] = jnp.cos(ang)
    sin_ref[...] = jnp.where(lane < ROPE_HALF, -s, s)


def _rope_tables(pos2, invf_lane, tm):
    T = pos2.shape[0]
    row = pl.BlockSpec((tm, LANES), lambda i: (i, 0))
    return pl.pallas_call(
        _rope_tab_kernel,
        grid=(T // tm,),
        in_specs=[pl.BlockSpec((tm, 1), lambda i: (i, 0)), _const_spec((1, LANES))],
        out_specs=[row, row],
        out_shape=[jax.ShapeDtypeStruct((T, LANES), F32)] * 2,
        compiler_params=_params(("parallel",)),
        name="rope_tab",
    )(pos2, invf_lane)


def _in_proj1_kernel(hn_ref, w_ref, cos_ref, sin_ref, out_ref):
    j = pl.program_id(0) % 3
    acc = _dot(hn_ref[...], w_ref[...])

    @pl.when(j == 2)
    def _():
        out_ref[...] = acc.astype(BF16)

    @pl.when(j < 2)
    def _():
        c = cos_ref[...]
        s = sin_ref[...]
        lane = lax.broadcasted_iota(jnp.int32, c.shape, 1)
        scale = jnp.where(j == 0, A_DH ** -0.5, 1.0).astype(F32)
        for h in range(A_HEADS):
            sl = slice(h * A_DH, (h + 1) * A_DH)
            seg = acc[:, sl]
            partner = jnp.where(lane < ROPE_HALF, pltpu.roll(seg, LANES - ROPE_HALF, 1),
                                pltpu.roll(seg, ROPE_HALF, 1))
            out_ref[:, sl] = ((seg * c + partner * s) * scale).astype(BF16)


def _in_proj1(hn, w, cos_t, sin_t, tm):
    T = hn.shape[0]
    ncol = w.shape[1] // D_MODEL
    return pl.pallas_call(
        _in_proj1_kernel,
        grid=(ncol, T // tm),
        in_specs=[pl.BlockSpec((tm, D_MODEL), lambda c, i: (i, 0)),
                  pl.BlockSpec((D_MODEL, D_MODEL), lambda c, i: (0, c)),
                  pl.BlockSpec((tm, LANES), lambda c, i: (i, 0)),
                  pl.BlockSpec((tm, LANES), lambda c, i: (i, 0))],
        out_specs=pl.BlockSpec((tm, D_MODEL), lambda c, i: (i, c)),
        out_shape=jax.ShapeDtypeStruct((T, w.shape[1]), BF16),
        compiler_params=_params(("parallel", "parallel")),
        name="in_proj1",
    )(hn, w, cos_t, sin_t)


ATT_SUB = 128


def _attn_kernel(q_ref, kp_ref, kc_ref, kn_ref, vp_ref, vc_ref, vn_ref, o_ref, lse_ref, kall, vall,
                 *, blkq, n_keys):
    i = pl.program_id(2)
    kall[0:RADIUS, :] = kp_ref[...]
    kall[RADIUS:RADIUS + blkq, :] = kc_ref[...]
    kall[RADIUS + blkq:, :] = kn_ref[...]
    vall[0:RADIUS, :] = vp_ref[...]
    vall[RADIUS:RADIUS + blkq, :] = vc_ref[...]
    vall[RADIUS + blkq:, :] = vn_ref[...]
    nk = ATT_SUB + 2 * RADIUS
    r = lax.broadcasted_iota(jnp.int32, (ATT_SUB, nk), 0)
    c = lax.broadcasted_iota(jnp.int32, (ATT_SUB, nk), 1)
    band = jnp.abs(c - RADIUS - r) <= RADIUS
    lane = lax.broadcasted_iota(jnp.int32, (ATT_SUB, LANES), 1)
    for a in range(blkq // ATT_SUB):
        key0 = i * blkq + a * ATT_SUB - RADIUS
        valid = band & (c + key0 >= 0) & (c + key0 < n_keys)
        rows = slice(a * ATT_SUB, (a + 1) * ATT_SUB)
        krows = slice(a * ATT_SUB, a * ATT_SUB + nk)
        lse_tile = jnp.zeros((ATT_SUB, LANES), F32)
        for h in range(A_HEADS):
            sl = slice(h * A_DH, (h + 1) * A_DH)
            s = jnp.where(valid, _dot_nt(q_ref[rows, sl], kall[krows, sl]), NEG_INF)
            m = jnp.max(s, axis=1, keepdims=True)
            p = jnp.exp(s - m)
            l = jnp.sum(p, axis=1, keepdims=True)
            pv = _dot(p.astype(BF16), vall[krows, sl])
            o_ref[rows, sl] = (pv / l).astype(BF16)
            lse_tile = jnp.where(lane == h, m + jnp.log(l), lse_tile)
        lse_ref[rows, :] = lse_tile


def _attention(proj, g, dil, blkq):
    B, S, W = proj.shape
    U = S // dil
    blkq = min(blkq, U)
    ncb = W // D_MODEL
    pv = proj.reshape(B, U, dil * W)
    hb = blkq // RADIUS
    nhalo = U // RADIUS

    def cur(j):
        return pl.BlockSpec((None, blkq, D_MODEL), lambda b, r, i: (b, i, r * ncb + 3 * g + j))

    def prev(j):
        return pl.BlockSpec((None, RADIUS, D_MODEL),
                            lambda b, r, i: (b, jnp.maximum(i * hb - 1, 0), r * ncb + 3 * g + j))

    def nxt(j):
        return pl.BlockSpec((None, RADIUS, D_MODEL),
                            lambda b, r, i: (b, jnp.minimum((i + 1) * hb, nhalo - 1), r * ncb + 3 * g + j))

    o, lse = pl.pallas_call(
        functools.partial(_attn_kernel, blkq=blkq, n_keys=U),
        grid=(B, dil, U // blkq),
        in_specs=[cur(0), prev(1), cur(1), nxt(1), prev(2), cur(2), nxt(2)],
        out_specs=[pl.BlockSpec((None, blkq, D_MODEL), lambda b, r, i: (b, i, r)),
                   pl.BlockSpec((None, blkq, LANES), lambda b, r, i: (b, i, r))],
        out_shape=[jax.ShapeDtypeStruct((B, U, dil * D_MODEL), BF16),
                   jax.ShapeDtypeStruct((B, U, dil * LANES), F32)],
        scratch_shapes=[pltpu.VMEM((blkq + 2 * RADIUS, D_MODEL), BF16)] * 2,
        compiler_params=_params(("parallel", "parallel", "parallel")),
        name=f"attn_d{dil}",
    )(pv, pv, pv, pv, pv, pv, pv)
    return o.reshape(B * S, D_MODEL), lse.reshape(B * S, LANES)


def kernel(x, p, positions, norm_mix, a_w_in, a_gate_bias, a_head_norm, a_w_out, b_w_in, b_w_out,
           norm_ffn, w_gate, w_up, w_down, norm_ple, ple_gate, ple_proj, final_norm):
    B, S, _ = x.shape
    T = B * S
    bf = lambda w: w.astype(BF16)
    vec = lambda w: w.reshape(1, -1).astype(F32)
    x2 = x.reshape(T, D_MODEL)
    n_main = 2 * M_QK_ALL + 2 * D_MODEL

    w_in = bf(a_w_in[0])
    q, k, v, o, gfeat = _in_proj0(x2, vec(norm_mix[0]), w_in[:, :n_main], w_in[:, n_main:],
                                  vec(a_gate_bias[0]), tm=512)
    gcol = gfeat.reshape(B, S, 4 * M_HEADS)
    grow = jnp.swapaxes(gfeat.reshape(B, S // M_CHUNK, M_CHUNK, 4 * M_HEADS), 2, 3)
    hf, hb = _mlstm(q.reshape(B, S, -1), k.reshape(B, S, -1), v.reshape(B, S, -1), gcol, grow, tb=256)
    h, hn = _tail0(hf.reshape(T, -1), hb.reshape(T, -1), o, vec(a_head_norm[0]), x2, bf(a_w_out[0]),
                   vec(norm_ffn[0]), bf(w_gate[0]), bf(w_up[0]), bf(w_down[0]), vec(norm_ple[0]),
                   bf(ple_gate[0]), p[0].reshape(T, PLE_DIM), bf(ple_proj[0]), vec(norm_mix[1]), tm=256)

    inv_freq = ROPE_THETA ** (-jnp.arange(0, ROPE_DIM, 2, dtype=F32) / ROPE_DIM)
    invf_lane = jnp.zeros((1, LANES), F32).at[0, :ROPE_DIM].set(jnp.tile(inv_freq, 2))
    cos_t, sin_t = _rope_tables(positions.reshape(T, 1), invf_lane, tm=1024)
    proj = _in_proj1(hn, bf(b_w_in[0]), cos_t, sin_t, tm=1024).reshape(B, S, -1)
    outs = [_attention(proj, g, dil, blkq=512) for g, (_, dil) in enumerate(DILATED_GROUPS)]
    out = _tail1(outs[0][0], outs[1][0], outs[2][0], outs[0][1], outs[1][1], outs[2][1], h, bf(b_w_out[0]),
                 vec(norm_ffn[1]), bf(w_gate[1]), bf(w_up[1]), bf(w_down[1]), vec(norm_ple[1]),
                 bf(ple_gate[1]), p[1].reshape(T, PLE_DIM), bf(ple_proj[1]), vec(final_norm), tm=256)
    return out.reshape(B, S, D_MODEL)
```

```python
import functools

import jax
import jax.numpy as jnp
from jax import lax
from jax.experimental import pallas as pl
from jax.experimental.pallas import tpu as pltpu

F32 = jnp.float32
BF16 = jnp.bfloat16

D_MODEL = 1024
LANES = 128
EPS = 1e-6

M_HEADS = 8
M_QK = 64
M_V = 128
M_CHUNK = 64
M_QK_ALL = M_HEADS * M_QK

A_HEADS = 8
A_DH = 128
RADIUS = 64
DILATED_GROUPS = ((128, 1), (512, 4), (2048, 16))
N_GROUPS = 3
ROPE_DIM = 32
ROPE_HALF = 16
ROPE_THETA = 500000.0
NEG_INF = -1e30

FFN_HIDDEN = 2816
PLE_DIM = 256

VMEM_LIMIT = 56 * 1024 * 1024


def _dot(a, b):
    return jnp.dot(a, b, preferred_element_type=F32)


def _dot_nt(a, b):
    return lax.dot_general(a, b, (((1,), (1,)), ((), ())), preferred_element_type=F32)


def _rms(x, w):
    ms = jnp.mean(x * x, axis=-1, keepdims=True)
    return x * lax.rsqrt(ms + EPS) * w


def _const_spec(shape):
    nd = len(shape)
    return pl.BlockSpec(shape, lambda *_: (0,) * nd, pipeline_mode=pl.Buffered(1))


def _params(sem):
    return pltpu.CompilerParams(dimension_semantics=sem, vmem_limit_bytes=VMEM_LIMIT)


def _in_proj0_kernel(x_ref, nw_ref, w_ref, wg_ref, gb_ref, trif_ref, trib_ref,
                     q_ref, k_ref, v_ref, o_ref, g_ref):
    hn = _rms(x_ref[...], nw_ref[...]).astype(BF16)
    q_ref[...] = (_dot(hn, w_ref[:, 0:M_QK_ALL]) * (M_QK ** -0.5)).astype(BF16)
    k_ref[...] = _dot(hn, w_ref[:, M_QK_ALL:2 * M_QK_ALL]).astype(BF16)
    v_ref[...] = _dot(hn, w_ref[:, 2 * M_QK_ALL:2 * M_QK_ALL + D_MODEL]).astype(BF16)
    o_ref[...] = _dot(hn, w_ref[:, 2 * M_QK_ALL + D_MODEL:]).astype(BF16)
    gates = _dot(hn, wg_ref[...]) + gb_ref[...]
    fg = gates[:, 2 * M_HEADS:]
    lf = jnp.minimum(fg, 0.0) - jnp.log1p(jnp.exp(-jnp.abs(fg)))
    h1 = lf.astype(BF16)
    r1 = lf - h1.astype(F32)
    h2 = r1.astype(BF16)
    h3 = (r1 - h2.astype(F32)).astype(BF16)
    trif = trif_ref[...]
    trib = trib_ref[...]
    cf = _dot(trif, h1) + _dot(trif, h2) + _dot(trif, h3)
    cb = _dot(trib, h1) + _dot(trib, h2) + _dot(trib, h3)
    g_ref[...] = jnp.concatenate([gates[:, :2 * M_HEADS], cf[:, :M_HEADS], cb[:, M_HEADS:]], axis=-1)


def _chunk_tri(tm, reverse):
    r = jnp.arange(tm)[:, None]
    c = jnp.arange(tm)[None, :]
    same = (r // M_CHUNK) == (c // M_CHUNK)
    tri = (c >= r) if reverse else (c <= r)
    return (same & tri).astype(BF16)


def _in_proj0(x2, nw, w_main, w_gate, gate_bias, tm):
    T = x2.shape[0]
    row = lambda n: pl.BlockSpec((tm, n), lambda i: (i, 0))
    return pl.pallas_call(
        _in_proj0_kernel,
        grid=(T // tm,),
        in_specs=[row(D_MODEL), _const_spec((1, D_MODEL)), _const_spec(w_main.shape),
                  _const_spec(w_gate.shape), _const_spec((1, 4 * M_HEADS)),
                  _const_spec((tm, tm)), _const_spec((tm, tm))],
        out_specs=[row(M_QK_ALL), row(M_QK_ALL), row(D_MODEL), row(D_MODEL), row(4 * M_HEADS)],
        out_shape=[jax.ShapeDtypeStruct((T, M_QK_ALL), BF16), jax.ShapeDtypeStruct((T, M_QK_ALL), BF16),
                   jax.ShapeDtypeStruct((T, D_MODEL), BF16), jax.ShapeDtypeStruct((T, D_MODEL), BF16),
                   jax.ShapeDtypeStruct((T, 4 * M_HEADS), F32)],
        compiler_params=_params(("parallel",)),
        name="in_proj0",
    )(x2, nw, w_main, w_gate, gate_bias, _chunk_tri(tm, False), _chunk_tri(tm, True))


def _running_max(x, reverse):
    n = x.shape[0]
    idx = lax.broadcasted_iota(jnp.int32, x.shape, 0)
    shift = 1
    while shift < n:
        if reverse:
            shifted, ok = pltpu.roll(x, n - shift, 0), idx < n - shift
        else:
            shifted, ok = pltpu.roll(x, shift, 0), idx >= shift
        x = jnp.maximum(x, jnp.where(ok, shifted, -jnp.inf))
        shift *= 2
    return x


def _mlstm_chunk(q_ref, k_ref, v_ref, gc_ref, gr_ref, h_ref, s_ref, mc_ref, mr_ref, sub, fwd):
    L, H = M_CHUNK, M_HEADS
    r0 = pl.multiple_of(sub * L, L)
    q = q_ref[pl.ds(r0, L), :]
    kT = k_ref[pl.ds(r0, L), :].astype(F32).T
    v = v_ref[pl.ds(r0, L), :]
    gc = gc_ref[pl.ds(r0, L), :]
    gr = gr_ref[sub]
    io = 0 if fwd else H
    bo = 2 * H if fwd else 3 * H
    s_prev = [s_ref[h] for h in range(H)]
    m_r = mr_ref[...]
    b_c = gc[:, bo:bo + H]
    w_c = gc[:, io:io + H] - b_c
    mj_c = b_c + jnp.maximum(m_r, _running_max(w_c, reverse=not fwd))
    u_c = b_c - mj_c
    iw_c = jnp.exp(b_c + m_r - mj_c)
    en_c = jnp.exp(-mj_c)
    g_r = b_c[L - 1:L, :] if fwd else b_c[0:1, :]
    mr_ref[...] = g_r + jnp.maximum(m_r, jnp.max(w_c, axis=0, keepdims=True))
    m_c = mc_ref[...]
    b_r = gr[bo:bo + H, :]
    w_r = gr[io:io + H, :] - b_r
    g_c = b_r[:, L - 1:L] if fwd else b_r[:, 0:1]
    m_new = g_c + jnp.maximum(m_c, jnp.max(w_r, axis=1, keepdims=True))
    decay = jnp.exp(g_c + m_c - m_new)
    k_scale = jnp.exp(g_c + w_r - m_new)
    mc_ref[...] = m_new

    li = lax.broadcasted_iota(jnp.int32, (L, L), 0)
    si = lax.broadcasted_iota(jnp.int32, (L, L), 1)
    mask = (si <= li) if fwd else (si >= li)
    ones = jnp.ones((L, LANES), BF16)
    outs = []
    for h in range(H):
        dw = jnp.exp(jnp.where(mask, u_c[:, h:h + 1] + w_r[h:h + 1, :], -jnp.inf))
        qh = q[:, h * M_QK:(h + 1) * M_QK]
        kTh = kT[h * M_QK:(h + 1) * M_QK, :]
        sm = (_dot(qh, kTh.astype(BF16)) * dw).astype(BF16)
        kw = (kTh * k_scale[h:h + 1, :]).astype(BF16)
        qi = (qh.astype(F32) * iw_c[:, h:h + 1]).astype(BF16)
        v_ext = jnp.concatenate([v[:, h * M_V:(h + 1) * M_V], ones], axis=1)
        both = _dot(jnp.concatenate([sm, kw], axis=0), v_ext)
        out = both[:L] + _dot(qi, s_prev[h].astype(BF16))
        denom = jnp.maximum(jnp.abs(out[:, M_V:]), en_c[:, h:h + 1])
        outs.append(out[:, :M_V] / denom)
        s_ref[h] = decay[h:h + 1, :] * s_prev[h] + both[L:]
    h_ref[pl.ds(r0, L), :] = jnp.concatenate(outs, axis=1)


def _mlstm_kernel(qf, kf, vf, gcf, grf, qb, kb, vb, gcb, grb, hf_ref, hb_ref, sf, sb, mcf, mrf, mcb, mrb,
                  *, nsub):
    @pl.when(pl.program_id(1) == 0)
    def _():
        for ref in (sf, sb, mcf, mrf, mcb, mrb):
            ref[...] = jnp.zeros_like(ref)

    def body(j, carry):
        _mlstm_chunk(qf, kf, vf, gcf, grf, hf_ref, sf, mcf, mrf, j, True)
        _mlstm_chunk(qb, kb, vb, gcb, grb, hb_ref, sb, mcb, mrb, nsub - 1 - j, False)
        return carry

    lax.fori_loop(0, nsub, body, 0)


def _mlstm(q, k, v, gcol, grow, tb):
    B, S, _ = q.shape
    n = S // tb
    nsub = tb // M_CHUNK
    fw = lambda b, c: (b, c, 0)
    bw = lambda b, c: (b, n - 1 - c, 0)
    fw4 = lambda b, c: (b, c, 0, 0)
    bw4 = lambda b, c: (b, n - 1 - c, 0, 0)

    def specs(im, im4):
        return [pl.BlockSpec((None, tb, M_QK_ALL), im), pl.BlockSpec((None, tb, M_QK_ALL), im),
                pl.BlockSpec((None, tb, D_MODEL), im), pl.BlockSpec((None, tb, 4 * M_HEADS), im),
                pl.BlockSpec((None, nsub, 4 * M_HEADS, M_CHUNK), im4)]

    return pl.pallas_call(
        functools.partial(_mlstm_kernel, nsub=nsub),
        grid=(B, n),
        in_specs=specs(fw, fw4) + specs(bw, bw4),
        out_specs=[pl.BlockSpec((None, tb, D_MODEL), fw), pl.BlockSpec((None, tb, D_MODEL), bw)],
        out_shape=[jax.ShapeDtypeStruct((B, S, D_MODEL), F32)] * 2,
        scratch_shapes=[pltpu.VMEM((M_HEADS, M_QK, 2 * M_V), F32), pltpu.VMEM((M_HEADS, M_QK, 2 * M_V), F32),
                        pltpu.VMEM((M_HEADS, 1), F32), pltpu.VMEM((1, M_HEADS), F32),
                        pltpu.VMEM((M_HEADS, 1), F32), pltpu.VMEM((1, M_HEADS), F32)],
        compiler_params=_params(("parallel", "arbitrary")),
        name="mlstm",
    )(q, k, v, gcol, grow, q, k, v, gcol, grow)


def _tail_common(mix_in, x_ref, w_out_ref, nf_ref, wg_ref, wu_ref, wd_ref, npl_ref, pg_ref, p_ref, pp_ref):
    h1 = x_ref[...] + _dot(mix_in, w_out_ref[...])
    hn = _rms(h1, nf_ref[...]).astype(BF16)
    gate = _dot(hn, wg_ref[...])
    act = (gate * jax.nn.sigmoid(gate) * _dot(hn, wu_ref[...])).astype(BF16)
    h2 = h1 + _dot(act, wd_ref[...])
    pgate = jax.nn.sigmoid(_dot(_rms(h2, npl_ref[...]).astype(BF16), pg_ref[...]))
    return h2 + pgate * _dot(p_ref[...].astype(BF16), pp_ref[...])


N_SLABS = D_MODEL // LANES


def _tail0_kernel(hf_ref, hb_ref, o_ref, hnorm_ref, x_ref, w_out_ref, nf_ref, wg_ref, wu_ref, wd_ref,
                  npl_ref, pg_ref, p_ref, pp_ref, nnext_ref, out_ref, hn_ref, hn4_ref, hn16_ref,
                  mix_ref, slab_ref):
    for h in range(M_HEADS):
        sl = slice(h * M_V, (h + 1) * M_V)
        hh = hf_ref[:, sl] + hb_ref[:, sl]
        hh = hh * lax.rsqrt(jnp.mean(hh * hh, axis=-1, keepdims=True) + EPS) * hnorm_ref[:, sl]
        mix_ref[:, sl] = (jax.nn.sigmoid(o_ref[:, sl].astype(F32)) * hh).astype(BF16)
    h3 = _tail_common(mix_ref[...], x_ref, w_out_ref, nf_ref, wg_ref, wu_ref, wd_ref, npl_ref, pg_ref,
                      p_ref, pp_ref)
    out_ref[...] = h3
    hn = _rms(h3, nnext_ref[...])
    hn_ref[...] = hn.astype(BF16)
    tm = hn.shape[0]
    for s in range(N_SLABS):
        slab_ref[s] = hn[:, s * LANES:(s + 1) * LANES]
    for dil, ref in ((4, hn4_ref), (16, hn16_ref)):
        for r in range(dil):
            ref[r] = jnp.concatenate(
                [slab_ref[s, pl.ds(r, tm // dil, stride=dil), :] for s in range(N_SLABS)], axis=1).astype(BF16)


def _tail1_kernel(o0_ref, o1_ref, o2_ref, l0_ref, l1_ref, l2_ref, x_ref, w_out_ref, nf_ref, wg_ref, wu_ref,
                  wd_ref, npl_ref, pg_ref, p_ref, pp_ref, nfin_ref, out_ref, mix_ref, slab1_ref, slab2_ref,
                  ls1_ref, ls2_ref):
    tm = o0_ref.shape[0]
    for dil, o_ref, l_ref, slab_ref, ls_ref in ((4, o1_ref, l1_ref, slab1_ref, ls1_ref),
                                                (16, o2_ref, l2_ref, slab2_ref, ls2_ref)):
        for r in range(dil):
            rows = pl.ds(r, tm // dil, stride=dil)
            ls_ref[rows, :] = l_ref[r]
            o_r = o_ref[r].astype(F32)
            for s in range(N_SLABS):
                slab_ref[s, rows, :] = o_r[:, s * LANES:(s + 1) * LANES]
    l0 = l0_ref[...]
    l1 = ls1_ref[...]
    l2 = ls2_ref[...]
    mx = jnp.maximum(jnp.maximum(l0, l1), l2)
    e0 = jnp.exp(l0 - mx)
    e1 = jnp.exp(l1 - mx)
    e2 = jnp.exp(l2 - mx)
    inv = 1.0 / (e0 + e1 + e2)
    w0 = e0 * inv
    w1 = e1 * inv
    w2 = e2 * inv
    for h in range(A_HEADS):
        sl = slice(h * A_DH, (h + 1) * A_DH)
        mix_ref[:, sl] = (w0[:, h:h + 1] * o0_ref[:, sl].astype(F32) + w1[:, h:h + 1] * slab1_ref[h]
                          + w2[:, h:h + 1] * slab2_ref[h]).astype(BF16)
    h3 = _tail_common(mix_ref[...], x_ref, w_out_ref, nf_ref, wg_ref, wu_ref, wd_ref, npl_ref, pg_ref,
                      p_ref, pp_ref)
    out_ref[...] = _rms(h3, nfin_ref[...])


def _tail_weight_specs():
    return [_const_spec((D_MODEL, D_MODEL)), _const_spec((1, D_MODEL)),
            _const_spec((D_MODEL, FFN_HIDDEN)), _const_spec((D_MODEL, FFN_HIDDEN)),
            _const_spec((FFN_HIDDEN, D_MODEL)), _const_spec((1, D_MODEL)), _const_spec((D_MODEL, D_MODEL))]


def _residue_spec(dil, tm, n, seq):
    nb = seq // tm
    return pl.BlockSpec((None, dil, tm // dil, n), lambda i: (i // nb, 0, i % nb, 0))


def _tail0(hf, hb, o, hnorm, x2, w_out, nf, wg, wu, wd, npl, pg, p2, pproj, nnext, tm, batch):
    T = x2.shape[0]
    S = T // batch
    row = lambda n: pl.BlockSpec((tm, n), lambda i: (i, 0))
    return pl.pallas_call(
        _tail0_kernel,
        grid=(T // tm,),
        in_specs=[row(D_MODEL), row(D_MODEL), row(D_MODEL), _const_spec((1, D_MODEL)), row(D_MODEL)]
        + _tail_weight_specs() + [row(PLE_DIM), _const_spec((PLE_DIM, D_MODEL)), _const_spec((1, D_MODEL))],
        out_specs=[row(D_MODEL), row(D_MODEL), _residue_spec(4, tm, D_MODEL, S),
                   _residue_spec(16, tm, D_MODEL, S)],
        out_shape=[jax.ShapeDtypeStruct((T, D_MODEL), F32), jax.ShapeDtypeStruct((T, D_MODEL), BF16),
                   jax.ShapeDtypeStruct((batch, 4, S // 4, D_MODEL), BF16),
                   jax.ShapeDtypeStruct((batch, 16, S // 16, D_MODEL), BF16)],
        scratch_shapes=[pltpu.VMEM((tm, D_MODEL), BF16), pltpu.VMEM((N_SLABS, tm, LANES), F32)],
        compiler_params=_params(("parallel",)),
        name="tail0",
    )(hf, hb, o, hnorm, x2, w_out, nf, wg, wu, wd, npl, pg, p2, pproj, nnext)


def _tail1(o0, o1, o2, l0, l1, l2, x2, w_out, nf, wg, wu, wd, npl, pg, p2, pproj, nfin, tm, batch):
    T = x2.shape[0]
    S = T // batch
    row = lambda n: pl.BlockSpec((tm, n), lambda i: (i, 0))
    return pl.pallas_call(
        _tail1_kernel,
        grid=(T // tm,),
        in_specs=[row(D_MODEL), _residue_spec(4, tm, D_MODEL, S), _residue_spec(16, tm, D_MODEL, S),
                  row(LANES), _residue_spec(4, tm, LANES, S), _residue_spec(16, tm, LANES, S), row(D_MODEL)]
        + _tail_weight_specs() + [row(PLE_DIM), _const_spec((PLE_DIM, D_MODEL)), _const_spec((1, D_MODEL))],
        out_specs=row(D_MODEL),
        out_shape=jax.ShapeDtypeStruct((T, D_MODEL), F32),
        scratch_shapes=[pltpu.VMEM((tm, D_MODEL), BF16), pltpu.VMEM((N_SLABS, tm, LANES), F32),
                        pltpu.VMEM((N_SLABS, tm, LANES), F32), pltpu.VMEM((tm, LANES), F32),
                        pltpu.VMEM((tm, LANES), F32)],
        compiler_params=_params(("parallel",)),
        name="tail1",
    )(o0, o1, o2, l0, l1, l2, x2, w_out, nf, wg, wu, wd, npl, pg, p2, pproj, nfin)


def _rope_tab_kernel(pos_ref, invf_ref, cos_ref, sin_ref, cos4_ref, sin4_ref, cos16_ref, sin16_ref):
    ang = pos_ref[...].astype(F32) * invf_ref[...]
    lane = lax.broadcasted_iota(jnp.int32, ang.shape, 1)
    s = jnp.sin(ang)
    cos_ref[...] = jnp.cos(ang)
    sin_ref[...] = jnp.where(lane < ROPE_HALF, -s, s)
    tm = ang.shape[0]
    for dil, c_ref, s_ref in ((4, cos4_ref, sin4_ref), (16, cos16_ref, sin16_ref)):
        for r in range(dil):
            rows = pl.ds(r, tm // dil, stride=dil)
            c_ref[r] = cos_ref[rows, :]
            s_ref[r] = sin_ref[rows, :]


def _rope_tables(pos2, invf_lane, tm, batch):
    T = pos2.shape[0]
    S = T // batch
    row = pl.BlockSpec((tm, LANES), lambda i: (i, 0))
    res = lambda dil: jax.ShapeDtypeStruct((batch, dil, S // dil, LANES), F32)
    return pl.pallas_call(
        _rope_tab_kernel,
        grid=(T // tm,),
        in_specs=[pl.BlockSpec((tm, 1), lambda i: (i, 0)), _const_spec((1, LANES))],
        out_specs=[row, row] + [_residue_spec(4, tm, LANES, S)] * 2 + [_residue_spec(16, tm, LANES, S)] * 2,
        out_shape=[jax.ShapeDtypeStruct((T, LANES), F32)] * 2 + [res(4)] * 2 + [res(16)] * 2,
        compiler_params=_params(("parallel",)),
        name="rope_tab",
    )(pos2, invf_lane)


def _in_proj1_kernel(hn_ref, w_ref, cos_ref, sin_ref, out_ref):
    j = pl.program_id(0)
    acc = _dot(hn_ref[...], w_ref[...])

    @pl.when(j == 2)
    def _():
        out_ref[...] = acc.astype(BF16)

    @pl.when(j < 2)
    def _():
        c = cos_ref[...]
        s = sin_ref[...]
        lane = lax.broadcasted_iota(jnp.int32, c.shape, 1)
        scale = jnp.where(j == 0, A_DH ** -0.5, 1.0).astype(F32)
        for h in range(A_HEADS):
            sl = slice(h * A_DH, (h + 1) * A_DH)
            seg = acc[:, sl]
            partner = jnp.where(lane < ROPE_HALF, pltpu.roll(seg, LANES - ROPE_HALF, 1),
                                pltpu.roll(seg, ROPE_HALF, 1))
            out_ref[:, sl] = ((seg * c + partner * s) * scale).astype(BF16)


def _in_proj1(hn, w, g, cos_t, sin_t, tm):
    T = hn.shape[0]
    return pl.pallas_call(
        _in_proj1_kernel,
        grid=(3, T // tm),
        in_specs=[pl.BlockSpec((tm, D_MODEL), lambda c, i: (i, 0)),
                  pl.BlockSpec((D_MODEL, D_MODEL), lambda c, i: (0, 3 * g + c)),
                  pl.BlockSpec((tm, LANES), lambda c, i: (i, 0)),
                  pl.BlockSpec((tm, LANES), lambda c, i: (i, 0))],
        out_specs=pl.BlockSpec((tm, D_MODEL), lambda c, i: (i, c)),
        out_shape=jax.ShapeDtypeStruct((T, 3 * D_MODEL), BF16),
        compiler_params=_params(("parallel", "parallel")),
        name=f"in_proj1_g{g}",
    )(hn, w, cos_t, sin_t)


ATT_SUB = 128


def _attn_kernel(q_ref, kp_ref, kc_ref, kn_ref, vp_ref, vc_ref, vn_ref, o_ref, lse_ref, kall, vall,
                 *, blkq, n_keys):
    i = pl.program_id(2)
    kall[0:RADIUS, :] = kp_ref[...]
    kall[RADIUS:RADIUS + blkq, :] = kc_ref[...]
    kall[RADIUS + blkq:, :] = kn_ref[...]
    vall[0:RADIUS, :] = vp_ref[...]
    vall[RADIUS:RADIUS + blkq, :] = vc_ref[...]
    vall[RADIUS + blkq:, :] = vn_ref[...]
    nk = ATT_SUB + 2 * RADIUS
    r = lax.broadcasted_iota(jnp.int32, (ATT_SUB, nk), 0)
    c = lax.broadcasted_iota(jnp.int32, (ATT_SUB, nk), 1)
    band = jnp.abs(c - RADIUS - r) <= RADIUS
    lane = lax.broadcasted_iota(jnp.int32, (ATT_SUB, LANES), 1)
    for a in range(blkq // ATT_SUB):
        key0 = i * blkq + a * ATT_SUB - RADIUS
        valid = band & (c + key0 >= 0) & (c + key0 < n_keys)
        rows = slice(a * ATT_SUB, (a + 1) * ATT_SUB)
        krows = slice(a * ATT_SUB, a * ATT_SUB + nk)
        lse_tile = jnp.zeros((ATT_SUB, LANES), F32)
        for h in range(A_HEADS):
            sl = slice(h * A_DH, (h + 1) * A_DH)
            s = jnp.where(valid, _dot_nt(q_ref[rows, sl], kall[krows, sl]), NEG_INF)
            m = jnp.max(s, axis=1, keepdims=True)
            p = jnp.exp(s - m)
            l = jnp.sum(p, axis=1, keepdims=True)
            pv = _dot(p.astype(BF16), vall[krows, sl])
            o_ref[rows, sl] = (pv / l).astype(BF16)
            lse_tile = jnp.where(lane == h, m + jnp.log(l), lse_tile)
        lse_ref[rows, :] = lse_tile


def _attention(proj, blkq):
    B, dil, U, _ = proj.shape
    blkq = min(blkq, U)
    hb = blkq // RADIUS
    nhalo = U // RADIUS

    def cur(j):
        return pl.BlockSpec((None, None, blkq, D_MODEL), lambda b, r, i: (b, r, i, j))

    def prev(j):
        return pl.BlockSpec((None, None, RADIUS, D_MODEL),
                            lambda b, r, i: (b, r, jnp.maximum(i * hb - 1, 0), j))

    def nxt(j):
        return pl.BlockSpec((None, None, RADIUS, D_MODEL),
                            lambda b, r, i: (b, r, jnp.minimum((i + 1) * hb, nhalo - 1), j))

    return pl.pallas_call(
        functools.partial(_attn_kernel, blkq=blkq, n_keys=U),
        grid=(B, dil, U // blkq),
        in_specs=[cur(0), prev(1), cur(1), nxt(1), prev(2), cur(2), nxt(2)],
        out_specs=[pl.BlockSpec((None, None, blkq, D_MODEL), lambda b, r, i: (b, r, i, 0)),
                   pl.BlockSpec((None, None, blkq, LANES), lambda b, r, i: (b, r, i, 0))],
        out_shape=[jax.ShapeDtypeStruct((B, dil, U, D_MODEL), BF16),
                   jax.ShapeDtypeStruct((B, dil, U, LANES), F32)],
        scratch_shapes=[pltpu.VMEM((blkq + 2 * RADIUS, D_MODEL), BF16)] * 2,
        compiler_params=_params(("parallel", "parallel", "parallel")),
        name=f"attn_d{dil}",
    )(proj, proj, proj, proj, proj, proj, proj)


def kernel(x, p, positions, norm_mix, a_w_in, a_gate_bias, a_head_norm, a_w_out, b_w_in, b_w_out,
           norm_ffn, w_gate, w_up, w_down, norm_ple, ple_gate, ple_proj, final_norm):
    B, S, _ = x.shape
    T = B * S
    bf = lambda w: w.astype(BF16)
    vec = lambda w: w.reshape(1, -1).astype(F32)
    x2 = x.reshape(T, D_MODEL)
    n_main = 2 * M_QK_ALL + 2 * D_MODEL

    w_in = bf(a_w_in[0])
    q, k, v, o, gfeat = _in_proj0(x2, vec(norm_mix[0]), w_in[:, :n_main], w_in[:, n_main:],
                                  vec(a_gate_bias[0]), tm=512)
    gcol = gfeat.reshape(B, S, 4 * M_HEADS)
    grow = jnp.swapaxes(gfeat.reshape(B, S // M_CHUNK, M_CHUNK, 4 * M_HEADS), 2, 3)
    hf, hb = _mlstm(q.reshape(B, S, -1), k.reshape(B, S, -1), v.reshape(B, S, -1), gcol, grow, tb=256)
    h, *hns = _tail0(hf.reshape(T, -1), hb.reshape(T, -1), o, vec(a_head_norm[0]), x2, bf(a_w_out[0]),
                     vec(norm_ffn[0]), bf(w_gate[0]), bf(w_up[0]), bf(w_down[0]), vec(norm_ple[0]),
                     bf(ple_gate[0]), p[0].reshape(T, PLE_DIM), bf(ple_proj[0]), vec(norm_mix[1]),
                     tm=256, batch=B)

    inv_freq = ROPE_THETA ** (-jnp.arange(0, ROPE_DIM, 2, dtype=F32) / ROPE_DIM)
    invf_lane = jnp.zeros((1, LANES), F32).at[0, :ROPE_DIM].set(jnp.tile(inv_freq, 2))
    tabs = _rope_tables(positions.reshape(T, 1), invf_lane, tm=1024, batch=B)
    w1 = bf(b_w_in[0])
    o_g, l_g = [], []
    for g, (_, dil) in enumerate(DILATED_GROUPS):
        proj = _in_proj1(hns[g].reshape(T, D_MODEL), w1, g, tabs[2 * g].reshape(T, LANES),
                         tabs[2 * g + 1].reshape(T, LANES), tm=1024)
        og, lg = _attention(proj.reshape(B, dil, S // dil, 3 * D_MODEL), blkq=512)
        o_g.append(og)
        l_g.append(lg)
    out = _tail1(o_g[0].reshape(T, D_MODEL), o_g[1], o_g[2], l_g[0].reshape(T, LANES), l_g[1], l_g[2], h,
                 bf(b_w_out[0]), vec(norm_ffn[1]), bf(w_gate[1]), bf(w_up[1]), bf(w_down[1]),
                 vec(norm_ple[1]), bf(ple_gate[1]), p[1].reshape(T, PLE_DIM), bf(ple_proj[1]),
                 vec(final_norm), tm=256, batch=B)
    return out.reshape(B, S, D_MODEL)
```

```python
import functools

import jax
import jax.numpy as jnp
from jax import lax
from jax.experimental import pallas as pl
from jax.experimental.pallas import tpu as pltpu

F32 = jnp.float32
BF16 = jnp.bfloat16

D_MODEL = 1024
LANES = 128
EPS = 1e-6

M_HEADS = 8
M_QK = 64
M_V = 128
M_CHUNK = 64
M_QK_ALL = M_HEADS * M_QK

A_HEADS = 8
A_DH = 128
RADIUS = 64
DILATED_GROUPS = ((128, 1), (512, 4), (2048, 16))
N_GROUPS = 3
ROPE_DIM = 32
ROPE_HALF = 16
ROPE_THETA = 500000.0
NEG_INF = -1e30

FFN_HIDDEN = 2816
PLE_DIM = 256

VMEM_LIMIT = 56 * 1024 * 1024


def _dot(a, b):
    return jnp.dot(a, b, preferred_element_type=F32)


def _dot_nt(a, b):
    return lax.dot_general(a, b, (((1,), (1,)), ((), ())), preferred_element_type=F32)


def _rms(x, w):
    ms = jnp.mean(x * x, axis=-1, keepdims=True)
    return x * lax.rsqrt(ms + EPS) * w


def _const_spec(shape):
    nd = len(shape)
    return pl.BlockSpec(shape, lambda *_: (0,) * nd, pipeline_mode=pl.Buffered(1))


def _params(sem):
    return pltpu.CompilerParams(dimension_semantics=sem, vmem_limit_bytes=VMEM_LIMIT)


def _in_proj0_kernel(x_ref, nw_ref, w_ref, wg_ref, gb_ref, trif_ref, trib_ref,
                     q_ref, k_ref, v_ref, o_ref, g_ref):
    hn = _rms(x_ref[...], nw_ref[...]).astype(BF16)
    q_ref[...] = (_dot(hn, w_ref[:, 0:M_QK_ALL]) * (M_QK ** -0.5)).astype(BF16)
    k_ref[...] = _dot(hn, w_ref[:, M_QK_ALL:2 * M_QK_ALL]).astype(BF16)
    v_ref[...] = _dot(hn, w_ref[:, 2 * M_QK_ALL:2 * M_QK_ALL + D_MODEL]).astype(BF16)
    o_ref[...] = _dot(hn, w_ref[:, 2 * M_QK_ALL + D_MODEL:]).astype(BF16)
    gates = _dot(hn, wg_ref[...]) + gb_ref[...]
    fg = gates[:, 2 * M_HEADS:]
    lf = jnp.minimum(fg, 0.0) - jnp.log1p(jnp.exp(-jnp.abs(fg)))
    h1 = lf.astype(BF16)
    r1 = lf - h1.astype(F32)
    h2 = r1.astype(BF16)
    h3 = (r1 - h2.astype(F32)).astype(BF16)
    trif = trif_ref[...]
    trib = trib_ref[...]
    cf = _dot(trif, h1) + _dot(trif, h2) + _dot(trif, h3)
    cb = _dot(trib, h1) + _dot(trib, h2) + _dot(trib, h3)
    g_ref[...] = jnp.concatenate([gates[:, :2 * M_HEADS], cf[:, :M_HEADS], cb[:, M_HEADS:]], axis=-1)


def _chunk_tri(tm, reverse):
    r = jnp.arange(tm)[:, None]
    c = jnp.arange(tm)[None, :]
    same = (r // M_CHUNK) == (c // M_CHUNK)
    tri = (c >= r) if reverse else (c <= r)
    return (same & tri).astype(BF16)


def _in_proj0(x2, nw, w_main, w_gate, gate_bias, tm):
    T = x2.shape[0]
    row = lambda n: pl.BlockSpec((tm, n), lambda i: (i, 0))
    return pl.pallas_call(
        _in_proj0_kernel,
        grid=(T // tm,),
        in_specs=[row(D_MODEL), _const_spec((1, D_MODEL)), _const_spec(w_main.shape),
                  _const_spec(w_gate.shape), _const_spec((1, 4 * M_HEADS)),
                  _const_spec((tm, tm)), _const_spec((tm, tm))],
        out_specs=[row(M_QK_ALL), row(M_QK_ALL), row(D_MODEL), row(D_MODEL), row(4 * M_HEADS)],
        out_shape=[jax.ShapeDtypeStruct((T, M_QK_ALL), BF16), jax.ShapeDtypeStruct((T, M_QK_ALL), BF16),
                   jax.ShapeDtypeStruct((T, D_MODEL), BF16), jax.ShapeDtypeStruct((T, D_MODEL), BF16),
                   jax.ShapeDtypeStruct((T, 4 * M_HEADS), F32)],
        compiler_params=_params(("parallel",)),
        name="in_proj0",
    )(x2, nw, w_main, w_gate, gate_bias, _chunk_tri(tm, False), _chunk_tri(tm, True))


def _running_max(x, reverse):
    n = x.shape[0]
    idx = lax.broadcasted_iota(jnp.int32, x.shape, 0)
    shift = 1
    while shift < n:
        if reverse:
            shifted, ok = pltpu.roll(x, n - shift, 0), idx < n - shift
        else:
            shifted, ok = pltpu.roll(x, shift, 0), idx >= shift
        x = jnp.maximum(x, jnp.where(ok, shifted, -jnp.inf))
        shift *= 2
    return x


def _mlstm_chunk(q_ref, k_ref, v_ref, gc_ref, gr_ref, h_ref, s_ref, mc_ref, mr_ref, sub, fwd):
    L, H = M_CHUNK, M_HEADS
    r0 = pl.multiple_of(sub * L, L)
    q = q_ref[pl.ds(r0, L), :]
    kT = k_ref[pl.ds(r0, L), :].astype(F32).T
    v = v_ref[pl.ds(r0, L), :]
    gc = gc_ref[pl.ds(r0, L), :]
    gr = gr_ref[sub]
    io = 0 if fwd else H
    bo = 2 * H if fwd else 3 * H
    s_prev = [s_ref[h] for h in range(H)]
    m_r = mr_ref[...]
    b_c = gc[:, bo:bo + H]
    w_c = gc[:, io:io + H] - b_c
    mj_c = b_c + jnp.maximum(m_r, _running_max(w_c, reverse=not fwd))
    u_c = b_c - mj_c
    iw_c = jnp.exp(b_c + m_r - mj_c)
    en_c = jnp.exp(-mj_c)
    g_r = b_c[L - 1:L, :] if fwd else b_c[0:1, :]
    mr_ref[...] = g_r + jnp.maximum(m_r, jnp.max(w_c, axis=0, keepdims=True))
    m_c = mc_ref[...]
    b_r = gr[bo:bo + H, :]
    w_r = gr[io:io + H, :] - b_r
    g_c = b_r[:, L - 1:L] if fwd else b_r[:, 0:1]
    m_new = g_c + jnp.maximum(m_c, jnp.max(w_r, axis=1, keepdims=True))
    decay = jnp.exp(g_c + m_c - m_new)
    k_scale = jnp.exp(g_c + w_r - m_new)
    mc_ref[...] = m_new

    li = lax.broadcasted_iota(jnp.int32, (L, L), 0)
    si = lax.broadcasted_iota(jnp.int32, (L, L), 1)
    mask = (si <= li) if fwd else (si >= li)
    ones = jnp.ones((L, LANES), BF16)
    outs = []
    for h in range(H):
        dw = jnp.exp(jnp.where(mask, u_c[:, h:h + 1] + w_r[h:h + 1, :], -jnp.inf))
        qh = q[:, h * M_QK:(h + 1) * M_QK]
        kTh = kT[h * M_QK:(h + 1) * M_QK, :]
        sm = (_dot(qh, kTh.astype(BF16)) * dw).astype(BF16)
        kw = (kTh * k_scale[h:h + 1, :]).astype(BF16)
        qi = (qh.astype(F32) * iw_c[:, h:h + 1]).astype(BF16)
        v_ext = jnp.concatenate([v[:, h * M_V:(h + 1) * M_V], ones], axis=1)
        both = _dot(jnp.concatenate([sm, kw], axis=0), v_ext)
        out = both[:L] + _dot(qi, s_prev[h].astype(BF16))
        denom = jnp.maximum(jnp.abs(out[:, M_V:]), en_c[:, h:h + 1])
        outs.append(out[:, :M_V] / denom)
        s_ref[h] = decay[h:h + 1, :] * s_prev[h] + both[L:]
    h_ref[pl.ds(r0, L), :] = jnp.concatenate(outs, axis=1)


def _mlstm_kernel(qf, kf, vf, gcf, grf, qb, kb, vb, gcb, grb, hf_ref, hb_ref, sf, sb, mcf, mrf, mcb, mrb,
                  *, nsub):
    @pl.when(pl.program_id(1) == 0)
    def _():
        for ref in (sf, sb, mcf, mrf, mcb, mrb):
            ref[...] = jnp.zeros_like(ref)

    def body(j, carry):
        _mlstm_chunk(qf, kf, vf, gcf, grf, hf_ref, sf, mcf, mrf, j, True)
        _mlstm_chunk(qb, kb, vb, gcb, grb, hb_ref, sb, mcb, mrb, nsub - 1 - j, False)
        return carry

    lax.fori_loop(0, nsub, body, 0)


def _mlstm(q, k, v, gcol, grow, tb):
    B, S, _ = q.shape
    n = S // tb
    nsub = tb // M_CHUNK
    fw = lambda b, c: (b, c, 0)
    bw = lambda b, c: (b, n - 1 - c, 0)
    fw4 = lambda b, c: (b, c, 0, 0)
    bw4 = lambda b, c: (b, n - 1 - c, 0, 0)

    def specs(im, im4):
        return [pl.BlockSpec((None, tb, M_QK_ALL), im), pl.BlockSpec((None, tb, M_QK_ALL), im),
                pl.BlockSpec((None, tb, D_MODEL), im), pl.BlockSpec((None, tb, 4 * M_HEADS), im),
                pl.BlockSpec((None, nsub, 4 * M_HEADS, M_CHUNK), im4)]

    return pl.pallas_call(
        functools.partial(_mlstm_kernel, nsub=nsub),
        grid=(B, n),
        in_specs=specs(fw, fw4) + specs(bw, bw4),
        out_specs=[pl.BlockSpec((None, tb, D_MODEL), fw), pl.BlockSpec((None, tb, D_MODEL), bw)],
        out_shape=[jax.ShapeDtypeStruct((B, S, D_MODEL), F32)] * 2,
        scratch_shapes=[pltpu.VMEM((M_HEADS, M_QK, 2 * M_V), F32), pltpu.VMEM((M_HEADS, M_QK, 2 * M_V), F32),
                        pltpu.VMEM((M_HEADS, 1), F32), pltpu.VMEM((1, M_HEADS), F32),
                        pltpu.VMEM((M_HEADS, 1), F32), pltpu.VMEM((1, M_HEADS), F32)],
        compiler_params=_params(("parallel", "arbitrary")),
        name="mlstm",
    )(q, k, v, gcol, grow, q, k, v, gcol, grow)


def _tail_common(mix_in, x_ref, w_out_ref, nf_ref, wg_ref, wu_ref, wd_ref, npl_ref, pg_ref, p_ref, pp_ref):
    h1 = x_ref[...] + _dot(mix_in, w_out_ref[...])
    hn = _rms(h1, nf_ref[...]).astype(BF16)
    gate = _dot(hn, wg_ref[...])
    act = (gate * jax.nn.sigmoid(gate) * _dot(hn, wu_ref[...])).astype(BF16)
    h2 = h1 + _dot(act, wd_ref[...])
    pgate = jax.nn.sigmoid(_dot(_rms(h2, npl_ref[...]).astype(BF16), pg_ref[...]))
    return h2 + pgate * _dot(p_ref[...].astype(BF16), pp_ref[...])


N_SLABS = D_MODEL // LANES


def _tail0_kernel(hf_ref, hb_ref, o_ref, hnorm_ref, x_ref, w_out_ref, nf_ref, wg_ref, wu_ref, wd_ref,
                  npl_ref, pg_ref, p_ref, pp_ref, nnext_ref, out_ref, hn_ref, hn4_ref, hn16_ref,
                  mix_ref, slab_ref):
    for h in range(M_HEADS):
        sl = slice(h * M_V, (h + 1) * M_V)
        hh = hf_ref[:, sl] + hb_ref[:, sl]
        hh = hh * lax.rsqrt(jnp.mean(hh * hh, axis=-1, keepdims=True) + EPS) * hnorm_ref[:, sl]
        mix_ref[:, sl] = (jax.nn.sigmoid(o_ref[:, sl].astype(F32)) * hh).astype(BF16)
    h3 = _tail_common(mix_ref[...], x_ref, w_out_ref, nf_ref, wg_ref, wu_ref, wd_ref, npl_ref, pg_ref,
                      p_ref, pp_ref)
    out_ref[...] = h3
    hn = _rms(h3, nnext_ref[...])
    hn_ref[...] = hn.astype(BF16)
    tm = hn.shape[0]
    for s in range(N_SLABS):
        slab_ref[s] = hn[:, s * LANES:(s + 1) * LANES]
    for dil, ref in ((4, hn4_ref), (16, hn16_ref)):
        for r in range(dil):
            ref[r] = jnp.concatenate(
                [slab_ref[s, pl.ds(r, tm // dil, stride=dil), :] for s in range(N_SLABS)], axis=1).astype(BF16)


def _tail1_kernel(o0_ref, o1_ref, o2_ref, l0_ref, l1_ref, l2_ref, x_ref, w_out_ref, nf_ref, wg_ref, wu_ref,
                  wd_ref, npl_ref, pg_ref, p_ref, pp_ref, nfin_ref, out_ref, mix_ref, slab1_ref, slab2_ref,
                  ls1_ref, ls2_ref):
    tm = o0_ref.shape[0]
    for dil, o_ref, l_ref, slab_ref, ls_ref in ((4, o1_ref, l1_ref, slab1_ref, ls1_ref),
                                                (16, o2_ref, l2_ref, slab2_ref, ls2_ref)):
        for r in range(dil):
            rows = pl.ds(r, tm // dil, stride=dil)
            ls_ref[rows, :] = l_ref[r]
            o_r = o_ref[r].astype(F32)
            for s in range(N_SLABS):
                slab_ref[s, rows, :] = o_r[:, s * LANES:(s + 1) * LANES]
    l0 = l0_ref[...]
    l1 = ls1_ref[...]
    l2 = ls2_ref[...]
    mx = jnp.maximum(jnp.maximum(l0, l1), l2)
    e0 = jnp.exp(l0 - mx)
    e1 = jnp.exp(l1 - mx)
    e2 = jnp.exp(l2 - mx)
    inv = 1.0 / (e0 + e1 + e2)
    w0 = e0 * inv
    w1 = e1 * inv
    w2 = e2 * inv
    for h in range(A_HEADS):
        sl = slice(h * A_DH, (h + 1) * A_DH)
        mix_ref[:, sl] = (w0[:, h:h + 1] * o0_ref[:, sl].astype(F32) + w1[:, h:h + 1] * slab1_ref[h]
                          + w2[:, h:h + 1] * slab2_ref[h]).astype(BF16)
    h3 = _tail_common(mix_ref[...], x_ref, w_out_ref, nf_ref, wg_ref, wu_ref, wd_ref, npl_ref, pg_ref,
                      p_ref, pp_ref)
    out_ref[...] = _rms(h3, nfin_ref[...])


def _tail_weight_specs():
    return [_const_spec((D_MODEL, D_MODEL)), _const_spec((1, D_MODEL)),
            _const_spec((D_MODEL, FFN_HIDDEN)), _const_spec((D_MODEL, FFN_HIDDEN)),
            _const_spec((FFN_HIDDEN, D_MODEL)), _const_spec((1, D_MODEL)), _const_spec((D_MODEL, D_MODEL))]


def _residue_spec(dil, tm, n, seq):
    nb = seq // tm
    return pl.BlockSpec((None, dil, tm // dil, n), lambda i: (i // nb, 0, i % nb, 0))


def _tail0(hf, hb, o, hnorm, x2, w_out, nf, wg, wu, wd, npl, pg, p2, pproj, nnext, tm, batch):
    T = x2.shape[0]
    S = T // batch
    row = lambda n: pl.BlockSpec((tm, n), lambda i: (i, 0))
    return pl.pallas_call(
        _tail0_kernel,
        grid=(T // tm,),
        in_specs=[row(D_MODEL), row(D_MODEL), row(D_MODEL), _const_spec((1, D_MODEL)), row(D_MODEL)]
        + _tail_weight_specs() + [row(PLE_DIM), _const_spec((PLE_DIM, D_MODEL)), _const_spec((1, D_MODEL))],
        out_specs=[row(D_MODEL), row(D_MODEL), _residue_spec(4, tm, D_MODEL, S),
                   _residue_spec(16, tm, D_MODEL, S)],
        out_shape=[jax.ShapeDtypeStruct((T, D_MODEL), F32), jax.ShapeDtypeStruct((T, D_MODEL), BF16),
                   jax.ShapeDtypeStruct((batch, 4, S // 4, D_MODEL), BF16),
                   jax.ShapeDtypeStruct((batch, 16, S // 16, D_MODEL), BF16)],
        scratch_shapes=[pltpu.VMEM((tm, D_MODEL), BF16), pltpu.VMEM((N_SLABS, tm, LANES), F32)],
        compiler_params=_params(("parallel",)),
        name="tail0",
    )(hf, hb, o, hnorm, x2, w_out, nf, wg, wu, wd, npl, pg, p2, pproj, nnext)


def _tail1(o0, o1, o2, l0, l1, l2, x2, w_out, nf, wg, wu, wd, npl, pg, p2, pproj, nfin, tm, batch):
    T = x2.shape[0]
    S = T // batch
    row = lambda n: pl.BlockSpec((tm, n), lambda i: (i, 0))
    return pl.pallas_call(
        _tail1_kernel,
        grid=(T // tm,),
        in_specs=[row(D_MODEL), _residue_spec(4, tm, D_MODEL, S), _residue_spec(16, tm, D_MODEL, S),
                  row(LANES), _residue_spec(4, tm, LANES, S), _residue_spec(16, tm, LANES, S), row(D_MODEL)]
        + _tail_weight_specs() + [row(PLE_DIM), _const_spec((PLE_DIM, D_MODEL)), _const_spec((1, D_MODEL))],
        out_specs=row(D_MODEL),
        out_shape=jax.ShapeDtypeStruct((T, D_MODEL), F32),
        scratch_shapes=[pltpu.VMEM((tm, D_MODEL), BF16), pltpu.VMEM((N_SLABS, tm, LANES), F32),
                        pltpu.VMEM((N_SLABS, tm, LANES), F32), pltpu.VMEM((tm, LANES), F32),
                        pltpu.VMEM((tm, LANES), F32)],
        compiler_params=_params(("parallel",)),
        name="tail1",
    )(o0, o1, o2, l0, l1, l2, x2, w_out, nf, wg, wu, wd, npl, pg, p2, pproj, nfin)


def _rope_tab_kernel(pos_ref, invf_ref, cos_ref, sin_ref, cos4_ref, sin4_ref, cos16_ref, sin16_ref):
    ang = pos_ref[...].astype(F32) * invf_ref[...]
    lane = lax.broadcasted_iota(jnp.int32, ang.shape, 1)
    s = jnp.sin(ang)
    cos_ref[...] = jnp.cos(ang)
    sin_ref[...] = jnp.where(lane < LANES // 2, -s, s)
    tm = ang.shape[0]
    for dil, c_ref, s_ref in ((4, cos4_ref, sin4_ref), (16, cos16_ref, sin16_ref)):
        for r in range(dil):
            rows = pl.ds(r, tm // dil, stride=dil)
            c_ref[r] = cos_ref[rows, :]
            s_ref[r] = sin_ref[rows, :]


def _rope_tables(pos2, invf_lane, tm, batch):
    T = pos2.shape[0]
    S = T // batch
    row = pl.BlockSpec((tm, LANES), lambda i: (i, 0))
    res = lambda dil: jax.ShapeDtypeStruct((batch, dil, S // dil, LANES), F32)
    return pl.pallas_call(
        _rope_tab_kernel,
        grid=(T // tm,),
        in_specs=[pl.BlockSpec((tm, 1), lambda i: (i, 0)), _const_spec((1, LANES))],
        out_specs=[row, row] + [_residue_spec(4, tm, LANES, S)] * 2 + [_residue_spec(16, tm, LANES, S)] * 2,
        out_shape=[jax.ShapeDtypeStruct((T, LANES), F32)] * 2 + [res(4)] * 2 + [res(16)] * 2,
        compiler_params=_params(("parallel",)),
        name="rope_tab",
    )(pos2, invf_lane)


MXU_COLS = 256
LOG2E = 1.4426950408889634
LN2 = 0.6931471805599453


def _in_proj1_kernel(hn_ref, w_ref, cos_ref, sin_ref, out_ref):
    j = pl.program_id(0)
    scale = jnp.where(j == 0, (A_DH ** -0.5) * LOG2E, 1.0).astype(F32)
    c = jnp.where(j == 2, 1.0, cos_ref[...] * scale)
    s = jnp.where(j == 2, 0.0, sin_ref[...] * scale)
    hn = hn_ref[...]
    for nb in range(D_MODEL // MXU_COLS):
        acc = _dot(hn, w_ref[:, nb * MXU_COLS:(nb + 1) * MXU_COLS])
        for half in range(MXU_COLS // A_DH):
            seg = acc[:, half * A_DH:(half + 1) * A_DH]
            col = nb * MXU_COLS + half * A_DH
            out_ref[:, col:col + A_DH] = (seg * c + pltpu.roll(seg, A_DH // 2, 1) * s).astype(BF16)


def _in_proj1(hn, w, g, cos_t, sin_t, tm):
    T = hn.shape[0]
    return pl.pallas_call(
        _in_proj1_kernel,
        grid=(3, T // tm),
        in_specs=[pl.BlockSpec((tm, D_MODEL), lambda c, i: (i, 0)),
                  pl.BlockSpec((D_MODEL, D_MODEL), lambda c, i: (0, 3 * g + c)),
                  pl.BlockSpec((tm, LANES), lambda c, i: (i, 0)),
                  pl.BlockSpec((tm, LANES), lambda c, i: (i, 0))],
        out_specs=pl.BlockSpec((tm, D_MODEL), lambda c, i: (i, c)),
        out_shape=jax.ShapeDtypeStruct((T, 3 * D_MODEL), BF16),
        compiler_params=_params(("parallel", "parallel")),
        name=f"in_proj1_g{g}",
    )(hn, w, cos_t, sin_t)


ATT_SUB = 128


def _attn_kernel(q_ref, kp_ref, kc_ref, kn_ref, vp_ref, vc_ref, vn_ref, o_ref, lse_ref, kall, vall,
                 *, blkq, n_keys):
    i = pl.program_id(2)
    kall[0:RADIUS, :] = kp_ref[...]
    kall[RADIUS:RADIUS + blkq, :] = kc_ref[...]
    kall[RADIUS + blkq:, :] = kn_ref[...]
    vall[0:RADIUS, :] = vp_ref[...]
    vall[RADIUS:RADIUS + blkq, :] = vc_ref[...]
    vall[RADIUS + blkq:, :] = vn_ref[...]
    nk = ATT_SUB + 2 * RADIUS
    r = lax.broadcasted_iota(jnp.int32, (ATT_SUB, nk), 0)
    c = lax.broadcasted_iota(jnp.int32, (ATT_SUB, nk), 1)
    band = jnp.abs(c - RADIUS - r) <= RADIUS
    lane = lax.broadcasted_iota(jnp.int32, (ATT_SUB, LANES), 1)
    ones = jnp.ones((nk, A_DH), BF16)
    for a in range(blkq // ATT_SUB):
        key0 = i * blkq + a * ATT_SUB - RADIUS
        valid = band & (c + key0 >= 0) & (c + key0 < n_keys)
        rows = slice(a * ATT_SUB, (a + 1) * ATT_SUB)
        krows = slice(a * ATT_SUB, a * ATT_SUB + nk)
        lse_tile = jnp.zeros((ATT_SUB, LANES), F32)
        for h in range(A_HEADS):
            sl = slice(h * A_DH, (h + 1) * A_DH)
            s = jnp.where(valid, _dot_nt(q_ref[rows, sl], kall[krows, sl]), NEG_INF)
            m = jnp.max(s, axis=1, keepdims=True)
            p = jnp.exp2(s - m).astype(BF16)
            pv = _dot(p, jnp.concatenate([vall[krows, sl], ones], axis=1))
            den = pv[:, A_DH:]
            o_ref[rows, sl] = (pv[:, :A_DH] / den).astype(BF16)
            lse_tile = jnp.where(lane == h, (m + jnp.log2(den)) * LN2, lse_tile)
        lse_ref[rows, :] = lse_tile


def _attention(proj, blkq):
    B, dil, U, _ = proj.shape
    blkq = min(blkq, U)
    hb = blkq // RADIUS
    nhalo = U // RADIUS

    def cur(j):
        return pl.BlockSpec((None, None, blkq, D_MODEL), lambda b, r, i: (b, r, i, j))

    def prev(j):
        return pl.BlockSpec((None, None, RADIUS, D_MODEL),
                            lambda b, r, i: (b, r, jnp.maximum(i * hb - 1, 0), j))

    def nxt(j):
        return pl.BlockSpec((None, None, RADIUS, D_MODEL),
                            lambda b, r, i: (b, r, jnp.minimum((i + 1) * hb, nhalo - 1), j))

    return pl.pallas_call(
        functools.partial(_attn_kernel, blkq=blkq, n_keys=U),
        grid=(B, dil, U // blkq),
        in_specs=[cur(0), prev(1), cur(1), nxt(1), prev(2), cur(2), nxt(2)],
        out_specs=[pl.BlockSpec((None, None, blkq, D_MODEL), lambda b, r, i: (b, r, i, 0)),
                   pl.BlockSpec((None, None, blkq, LANES), lambda b, r, i: (b, r, i, 0))],
        out_shape=[jax.ShapeDtypeStruct((B, dil, U, D_MODEL), BF16),
                   jax.ShapeDtypeStruct((B, dil, U, LANES), F32)],
        scratch_shapes=[pltpu.VMEM((blkq + 2 * RADIUS, D_MODEL), BF16)] * 2,
        compiler_params=_params(("parallel", "parallel", "parallel")),
        name=f"attn_d{dil}",
    )(proj, proj, proj, proj, proj, proj, proj)


def kernel(x, p, positions, norm_mix, a_w_in, a_gate_bias, a_head_norm, a_w_out, b_w_in, b_w_out,
           norm_ffn, w_gate, w_up, w_down, norm_ple, ple_gate, ple_proj, final_norm):
    B, S, _ = x.shape
    T = B * S
    bf = lambda w: w.astype(BF16)
    vec = lambda w: w.reshape(1, -1).astype(F32)
    x2 = x.reshape(T, D_MODEL)
    n_main = 2 * M_QK_ALL + 2 * D_MODEL

    w_in = bf(a_w_in[0])
    q, k, v, o, gfeat = _in_proj0(x2, vec(norm_mix[0]), w_in[:, :n_main], w_in[:, n_main:],
                                  vec(a_gate_bias[0]), tm=512)
    gcol = gfeat.reshape(B, S, 4 * M_HEADS)
    grow = jnp.swapaxes(gfeat.reshape(B, S // M_CHUNK, M_CHUNK, 4 * M_HEADS), 2, 3)
    hf, hb = _mlstm(q.reshape(B, S, -1), k.reshape(B, S, -1), v.reshape(B, S, -1), gcol, grow, tb=256)
    h, *hns = _tail0(hf.reshape(T, -1), hb.reshape(T, -1), o, vec(a_head_norm[0]), x2, bf(a_w_out[0]),
                     vec(norm_ffn[0]), bf(w_gate[0]), bf(w_up[0]), bf(w_down[0]), vec(norm_ple[0]),
                     bf(ple_gate[0]), p[0].reshape(T, PLE_DIM), bf(ple_proj[0]), vec(norm_mix[1]),
                     tm=256, batch=B)

    inv_freq = ROPE_THETA ** (-jnp.arange(0, ROPE_DIM, 2, dtype=F32) / ROPE_DIM)
    invf_lane = (jnp.zeros((1, LANES), F32).at[0, :ROPE_HALF].set(inv_freq)
                 .at[0, LANES // 2:LANES // 2 + ROPE_HALF].set(inv_freq))
    tabs = _rope_tables(positions.reshape(T, 1), invf_lane, tm=1024, batch=B)
    dim_order = jnp.concatenate([jnp.arange(0, ROPE_HALF), jnp.arange(ROPE_DIM, ROPE_DIM + 48),
                                 jnp.arange(ROPE_HALF, ROPE_DIM), jnp.arange(ROPE_DIM + 48, A_DH)])
    w1 = bf(b_w_in[0]).reshape(D_MODEL, N_GROUPS, 3, A_HEADS, A_DH)
    w1 = jnp.concatenate([w1[:, :, :2][..., dim_order], w1[:, :, 2:]], axis=2).reshape(D_MODEL, -1)
    o_g, l_g = [], []
    for g, (_, dil) in enumerate(DILATED_GROUPS):
        proj = _in_proj1(hns[g].reshape(T, D_MODEL), w1, g, tabs[2 * g].reshape(T, LANES),
                         tabs[2 * g + 1].reshape(T, LANES), tm=1024)
        og, lg = _attention(proj.reshape(B, dil, S // dil, 3 * D_MODEL), blkq=512)
        o_g.append(og)
        l_g.append(lg)
    out = _tail1(o_g[0].reshape(T, D_MODEL), o_g[1], o_g[2], l_g[0].reshape(T, LANES), l_g[1], l_g[2], h,
                 bf(b_w_out[0]), vec(norm_ffn[1]), bf(w_gate[1]), bf(w_up[1]), bf(w_down[1]),
                 vec(norm_ple[1]), bf(ple_gate[1]), p[1].reshape(T, PLE_DIM), bf(ple_proj[1]),
                 vec(final_norm), tm=256, batch=B)
    return out.reshape(B, S, D_MODEL)
```

```python
import functools

import jax
import jax.numpy as jnp
from jax import lax
from jax.experimental import pallas as pl
from jax.experimental.pallas import tpu as pltpu

F32 = jnp.float32
BF16 = jnp.bfloat16

D_MODEL = 1024
LANES = 128
EPS = 1e-6

M_HEADS = 8
M_QK = 64
M_V = 128
M_CHUNK = 64
M_QK_ALL = M_HEADS * M_QK

A_HEADS = 8
A_DH = 128
RADIUS = 64
DILATED_GROUPS = ((128, 1), (512, 4), (2048, 16))
N_GROUPS = 3
ROPE_DIM = 32
ROPE_HALF = 16
ROPE_THETA = 500000.0
NEG_INF = -1e30

FFN_HIDDEN = 2816
PLE_DIM = 256

VMEM_LIMIT = 56 * 1024 * 1024


def _dot(a, b):
    return jnp.dot(a, b, preferred_element_type=F32)


def _dot_nt(a, b):
    return lax.dot_general(a, b, (((1,), (1,)), ((), ())), preferred_element_type=F32)


def _rms(x, w):
    ms = jnp.mean(x * x, axis=-1, keepdims=True)
    return x * lax.rsqrt(ms + EPS) * w


def _const_spec(shape):
    nd = len(shape)
    return pl.BlockSpec(shape, lambda *_: (0,) * nd, pipeline_mode=pl.Buffered(1))


def _params(sem):
    return pltpu.CompilerParams(dimension_semantics=sem, vmem_limit_bytes=VMEM_LIMIT)


N_DIRS = 2
N_GATE = N_DIRS * M_HEADS
N_FEAT = 6


def _chunk_scan(x, op, fill, reverse):
    n = x.shape[0]
    pos = lax.broadcasted_iota(jnp.int32, x.shape, 0) % M_CHUNK
    shift = 1
    while shift < M_CHUNK:
        if reverse:
            shifted, ok = pltpu.roll(x, n - shift, 0), pos < M_CHUNK - shift
        else:
            shifted, ok = pltpu.roll(x, shift, 0), pos >= shift
        x = op(x, jnp.where(ok, shifted, fill))
        shift *= 2
    return x


def _in_proj0_kernel(x_ref, nw_ref, w_ref, wkt_ref, wg_ref, gb_ref, q_ref, ktd_ref, v_ref, o_ref, f_ref):
    tm = x_ref.shape[0]
    hn = _rms(x_ref[...], nw_ref[...]).astype(BF16)
    gates = _dot(hn, wg_ref[...]) + gb_ref[...]
    ig = gates[:, :N_GATE]
    fg = gates[:, N_GATE:]
    lf = jnp.minimum(fg, 0.0) - jnp.log1p(jnp.exp(-jnp.abs(fg)))
    is_fwd = lax.broadcasted_iota(jnp.int32, lf.shape, 1) < M_HEADS
    q_ref[...] = (_dot(hn, w_ref[:, 0:M_QK_ALL]) * (M_QK ** -0.5)).astype(BF16)
    pre = _chunk_scan(lf, jnp.add, 0.0, False)
    suf = _chunk_scan(lf, jnp.add, 0.0, True)
    b = jnp.where(is_fwd, pre, suf)
    g = pre + suf - lf
    w = ig - b
    v_ref[...] = _dot(hn, w_ref[:, M_QK_ALL:M_QK_ALL + D_MODEL]).astype(BF16)
    wpre = _chunk_scan(w, jnp.maximum, -jnp.inf, False)
    wsuf = _chunk_scan(w, jnp.maximum, -jnp.inf, True)
    for k, feat in enumerate((b, jnp.where(is_fwd, wpre, wsuf), g, jnp.maximum(wpre, wsuf), w, g + w)):
        f_ref[k] = feat
    o_ref[...] = _dot(hn, w_ref[:, M_QK_ALL + D_MODEL:]).astype(BF16)
    hn_dup = jnp.concatenate([hn[c * M_CHUNK:(c + 1) * M_CHUNK] for c in range(tm // M_CHUNK) for _ in (0, 1)],
                             axis=0)
    ktd = _dot_nt(wkt_ref[...], hn_dup).astype(BF16)
    for c in range(tm // M_CHUNK):
        ktd_ref[c] = ktd[:, c * LANES:(c + 1) * LANES]


def _in_proj0(x2, nw, w_main, w_kt, w_gate, gate_bias, tm):
    T = x2.shape[0]
    row = lambda n: pl.BlockSpec((tm, n), lambda i: (i, 0))
    nch = tm // M_CHUNK
    return pl.pallas_call(
        _in_proj0_kernel,
        grid=(T // tm,),
        in_specs=[row(D_MODEL), _const_spec((1, D_MODEL)), _const_spec(w_main.shape), _const_spec(w_kt.shape),
                  _const_spec(w_gate.shape), _const_spec((1, 2 * N_GATE))],
        out_specs=[row(M_QK_ALL), pl.BlockSpec((nch, M_QK_ALL, LANES), lambda i: (i, 0, 0)),
                   row(D_MODEL), row(D_MODEL), pl.BlockSpec((N_FEAT, tm, N_GATE), lambda i: (0, i, 0))],
        out_shape=[jax.ShapeDtypeStruct((T, M_QK_ALL), BF16),
                   jax.ShapeDtypeStruct((T // M_CHUNK, M_QK_ALL, LANES), BF16),
                   jax.ShapeDtypeStruct((T, D_MODEL), BF16), jax.ShapeDtypeStruct((T, D_MODEL), BF16),
                   jax.ShapeDtypeStruct((N_FEAT, T, N_GATE), F32)],
        compiler_params=_params(("parallel",)),
        name="in_proj0",
    )(x2, nw, w_main, w_kt, w_gate, gate_bias)


N_PAIRS = M_HEADS // 2
HALF = LANES // 2
PAIR_W = 2 * 2 * M_V
ROW_W, ROW_A, ROW_G, ROW_WMAX, N_ROWF = 0, N_GATE // 2, N_GATE, 2 * N_GATE, 3 * N_GATE


def _lane_pair(x, col):
    lane = lax.broadcasted_iota(jnp.int32, (x.shape[0], LANES), 1)
    return jnp.where(lane < HALF, x[:, col:col + 1], x[:, col + 1:col + 2])


def _mlstm_chunk(q_ref, kt_ref, v_ref, cf_ref, rf_ref, h_ref, s_ref, sb_ref, mc_ref, mr_ref, sub, fwd):
    L = M_CHUNK
    r0 = pl.multiple_of(sub * L, L)
    rows = pl.ds(r0, L)
    lo = 0 if fwd else M_HEADS
    b_c = cf_ref[0, rows, :]
    m_r = mr_ref[...]
    mj = b_c + jnp.maximum(m_r, cf_ref[1, rows, :])
    u = b_c - mj
    iw = jnp.exp(b_c + m_r - mj)
    en = jnp.exp(-mj)
    mr_ref[...] = cf_ref[2, pl.ds(r0, 1), :] + jnp.maximum(m_r, cf_ref[3, pl.ds(r0, 1), :])
    rf = rf_ref[sub]
    m_c = mc_ref[...]
    g = rf[ROW_G + lo:ROW_G + lo + M_HEADS]
    m_new = g + jnp.maximum(m_c, rf[ROW_WMAX + lo:ROW_WMAX + lo + M_HEADS])
    decay = jnp.exp(g + m_c - m_new)
    mc_ref[...] = m_new

    low_half = lax.broadcasted_iota(jnp.int32, (1, LANES), 1) < HALF
    li = lax.broadcasted_iota(jnp.int32, (L, LANES), 0)
    si = lax.broadcasted_iota(jnp.int32, (L, LANES), 1) % HALF
    mask = (si <= li) if fwd else (si >= li)
    same_head = (lax.broadcasted_iota(jnp.int32, (LANES, LANES), 0) // HALF
                 == lax.broadcasted_iota(jnp.int32, (LANES, LANES), 1) // HALF)
    ones = jnp.ones((L, M_V), BF16)
    zeros = jnp.zeros((L, M_V), BF16)
    yield
    q2s, kbds, scores = [], [], []
    for p in range(N_PAIRS):
        q2s.append(q_ref[rows, p * LANES:(p + 1) * LANES])
        kbds.append(jnp.where(same_head, kt_ref[sub, p * LANES:(p + 1) * LANES, :], jnp.zeros((), BF16)))
        scores.append(_dot(q2s[p], kbds[p]))
    yield
    mains, inters = [], []
    for p in range(N_PAIRS):
        pr = lo // 2 + p
        w_pr = rf[ROW_W + pr:ROW_W + pr + 1]
        m_new_pr = jnp.where(low_half, m_new[2 * p:2 * p + 1], m_new[2 * p + 1:2 * p + 2])
        k_scale = jnp.exp(rf[ROW_A + pr:ROW_A + pr + 1] - m_new_pr)
        dw = jnp.exp(jnp.where(mask, _lane_pair(u, lo + 2 * p) + w_pr, -jnp.inf))
        sm = (scores[p] * dw).astype(BF16)
        kw = (kbds[p].astype(F32) * k_scale).astype(BF16)
        qi = (q2s[p].astype(F32) * _lane_pair(iw, lo + 2 * p)).astype(BF16)
        v0 = v_ref[rows, 2 * p * M_V:(2 * p + 1) * M_V]
        v1 = v_ref[rows, (2 * p + 1) * M_V:(2 * p + 2) * M_V]
        vbd = jnp.concatenate([jnp.concatenate([v0, ones, zeros, zeros], axis=1),
                               jnp.concatenate([zeros, zeros, v1, ones], axis=1)], axis=0)
        mains.append(_dot(jnp.concatenate([sm, kw], axis=0), vbd))
        inters.append(_dot(qi, sb_ref[p]))
    yield
    outs = []
    for p in range(N_PAIRS):
        out = mains[p][:L] + inters[p]
        for j in range(2):
            h = 2 * p + j
            num = out[:, j * 2 * M_V:j * 2 * M_V + M_V]
            den = out[:, j * 2 * M_V + M_V:(j + 1) * 2 * M_V]
            outs.append(num / jnp.maximum(jnp.abs(den), en[:, lo + h:lo + h + 1]))
            blk = (p, slice(j * M_QK, (j + 1) * M_QK), slice(j * 2 * M_V, (j + 1) * 2 * M_V))
            dec = jnp.concatenate([decay[h:h + 1]] * 2, axis=1)
            s_new = dec * s_ref[blk] + mains[p][L + j * M_QK:L + (j + 1) * M_QK,
                                                j * 2 * M_V:(j + 1) * 2 * M_V]
            s_ref[blk] = s_new
            sb_ref[blk] = s_new.astype(BF16)
    h_ref[rows, :] = jnp.concatenate(outs, axis=1).astype(h_ref.dtype)
    yield


N_MLSTM_STAGES = 4


def _mlstm_kernel(qf, ktf, vf, cff, rff, qb, ktb, vb, cfb, rfb, hf_ref, hb_ref,
                  sf, sbf, mcf, mrf, sb, sbb, mcb, mrb, *, nsub):
    @pl.when(pl.program_id(1) == 0)
    def _():
        for ref in (sf, sbf, mcf, mrf, sb, sbb, mcb, mrb):
            ref[...] = jnp.zeros_like(ref)

    def body(j, carry):
        chunks = [_mlstm_chunk(qf, ktf, vf, cff, rff, hf_ref, sf, sbf, mcf, mrf, j, True),
                  _mlstm_chunk(qb, ktb, vb, cfb, rfb, hb_ref, sb, sbb, mcb, mrb, nsub - 1 - j, False)]
        for _ in range(N_MLSTM_STAGES):
            for chunk in chunks:
                next(chunk)
        return carry

    lax.fori_loop(0, nsub, body, 0)


def _mlstm(q, ktd, v, cfeat, rfeat, tb):
    B, S, _ = q.shape
    n = S // tb
    nsub = tb // M_CHUNK
    fw = lambda b, c: (b, c, 0)
    bw = lambda b, c: (b, n - 1 - c, 0)
    fw4 = lambda b, c: (b, c, 0, 0)
    bw4 = lambda b, c: (b, n - 1 - c, 0, 0)

    def specs(im, im4):
        return [pl.BlockSpec((None, tb, M_QK_ALL), im), pl.BlockSpec((None, nsub, M_QK_ALL, LANES), im4),
                pl.BlockSpec((None, tb, D_MODEL), im),
                pl.BlockSpec((4, None, tb, N_GATE), lambda b, c: (0,) + im(b, c)),
                pl.BlockSpec((None, nsub, N_ROWF, LANES), im4)]

    state = [pltpu.VMEM((N_PAIRS, LANES, PAIR_W), F32), pltpu.VMEM((N_PAIRS, LANES, PAIR_W), BF16),
             pltpu.VMEM((M_HEADS, LANES), F32), pltpu.VMEM((1, N_GATE), F32)]
    return pl.pallas_call(
        functools.partial(_mlstm_kernel, nsub=nsub),
        grid=(B, n),
        in_specs=specs(fw, fw4) + specs(bw, bw4),
        out_specs=[pl.BlockSpec((None, tb, D_MODEL), fw), pl.BlockSpec((None, tb, D_MODEL), bw)],
        out_shape=[jax.ShapeDtypeStruct((B, S, D_MODEL), BF16)] * 2,
        scratch_shapes=state + state,
        compiler_params=_params(("parallel", "arbitrary")),
        name="mlstm",
    )(q, ktd, v, cfeat, rfeat, q, ktd, v, cfeat, rfeat)


def _tail_common(mix_in, x_ref, w_out_ref, nf_ref, wg_ref, wu_ref, wd_ref, npl_ref, pg_ref, p_ref, pp_ref):
    h1 = x_ref[...] + _dot(mix_in, w_out_ref[...])
    hn = _rms(h1, nf_ref[...]).astype(BF16)
    gate = _dot(hn, wg_ref[...])
    act = (gate * jax.nn.sigmoid(gate) * _dot(hn, wu_ref[...])).astype(BF16)
    h2 = h1 + _dot(act, wd_ref[...])
    pgate = jax.nn.sigmoid(_dot(_rms(h2, npl_ref[...]).astype(BF16), pg_ref[...]))
    return h2 + pgate * _dot(p_ref[...].astype(BF16), pp_ref[...])


N_SLABS = D_MODEL // LANES


def _tail0_kernel(hf_ref, hb_ref, o_ref, hnorm_ref, x_ref, w_out_ref, nf_ref, wg_ref, wu_ref, wd_ref,
                  npl_ref, pg_ref, p_ref, pp_ref, nnext_ref, out_ref, hn_ref, hn4_ref, hn16_ref,
                  mix_ref, slab_ref):
    for h in range(M_HEADS):
        sl = slice(h * M_V, (h + 1) * M_V)
        hh = hf_ref[:, sl].astype(F32) + hb_ref[:, sl].astype(F32)
        hh = hh * lax.rsqrt(jnp.mean(hh * hh, axis=-1, keepdims=True) + EPS) * hnorm_ref[:, sl]
        mix_ref[:, sl] = (jax.nn.sigmoid(o_ref[:, sl].astype(F32)) * hh).astype(BF16)
    h3 = _tail_common(mix_ref[...], x_ref, w_out_ref, nf_ref, wg_ref, wu_ref, wd_ref, npl_ref, pg_ref,
                      p_ref, pp_ref)
    out_ref[...] = h3
    hn = _rms(h3, nnext_ref[...])
    hn_ref[...] = hn.astype(BF16)
    tm = hn.shape[0]
    for s in range(N_SLABS):
        slab_ref[s] = hn[:, s * LANES:(s + 1) * LANES]
    for dil, ref in ((4, hn4_ref), (16, hn16_ref)):
        for r in range(dil):
            ref[r] = jnp.concatenate(
                [slab_ref[s, pl.ds(r, tm // dil, stride=dil), :] for s in range(N_SLABS)], axis=1).astype(BF16)


def _tail1_kernel(o0_ref, o1_ref, o2_ref, l0_ref, l1_ref, l2_ref, x_ref, w_out_ref, nf_ref, wg_ref, wu_ref,
                  wd_ref, npl_ref, pg_ref, p_ref, pp_ref, nfin_ref, out_ref, mix_ref, slab1_ref, slab2_ref,
                  ls1_ref, ls2_ref):
    tm = o0_ref.shape[0]
    for dil, o_ref, l_ref, slab_ref, ls_ref in ((4, o1_ref, l1_ref, slab1_ref, ls1_ref),
                                                (16, o2_ref, l2_ref, slab2_ref, ls2_ref)):
        for r in range(dil):
            rows = pl.ds(r, tm // dil, stride=dil)
            ls_ref[rows, :] = l_ref[r]
            o_r = o_ref[r].astype(F32)
            for s in range(N_SLABS):
                slab_ref[s, rows, :] = o_r[:, s * LANES:(s + 1) * LANES]
    l0 = l0_ref[...]
    l1 = ls1_ref[...]
    l2 = ls2_ref[...]
    mx = jnp.maximum(jnp.maximum(l0, l1), l2)
    e0 = jnp.exp(l0 - mx)
    e1 = jnp.exp(l1 - mx)
    e2 = jnp.exp(l2 - mx)
    inv = 1.0 / (e0 + e1 + e2)
    w0 = e0 * inv
    w1 = e1 * inv
    w2 = e2 * inv
    for h in range(A_HEADS):
        sl = slice(h * A_DH, (h + 1) * A_DH)
        mix_ref[:, sl] = (w0[:, h:h + 1] * o0_ref[:, sl].astype(F32) + w1[:, h:h + 1] * slab1_ref[h]
                          + w2[:, h:h + 1] * slab2_ref[h]).astype(BF16)
    h3 = _tail_common(mix_ref[...], x_ref, w_out_ref, nf_ref, wg_ref, wu_ref, wd_ref, npl_ref, pg_ref,
                      p_ref, pp_ref)
    out_ref[...] = _rms(h3, nfin_ref[...])


def _tail_weight_specs():
    return [_const_spec((D_MODEL, D_MODEL)), _const_spec((1, D_MODEL)),
            _const_spec((D_MODEL, FFN_HIDDEN)), _const_spec((D_MODEL, FFN_HIDDEN)),
            _const_spec((FFN_HIDDEN, D_MODEL)), _const_spec((1, D_MODEL)), _const_spec((D_MODEL, D_MODEL))]


def _residue_spec(dil, tm, n, seq):
    nb = seq // tm
    return pl.BlockSpec((None, dil, tm // dil, n), lambda i: (i // nb, 0, i % nb, 0))


def _tail0(hf, hb, o, hnorm, x2, w_out, nf, wg, wu, wd, npl, pg, p2, pproj, nnext, tm, batch):
    T = x2.shape[0]
    S = T // batch
    row = lambda n: pl.BlockSpec((tm, n), lambda i: (i, 0))
    return pl.pallas_call(
        _tail0_kernel,
        grid=(T // tm,),
        in_specs=[row(D_MODEL), row(D_MODEL), row(D_MODEL), _const_spec((1, D_MODEL)), row(D_MODEL)]
        + _tail_weight_specs() + [row(PLE_DIM), _const_spec((PLE_DIM, D_MODEL)), _const_spec((1, D_MODEL))],
        out_specs=[row(D_MODEL), row(D_MODEL), _residue_spec(4, tm, D_MODEL, S),
                   _residue_spec(16, tm, D_MODEL, S)],
        out_shape=[jax.ShapeDtypeStruct((T, D_MODEL), F32), jax.ShapeDtypeStruct((T, D_MODEL), BF16),
                   jax.ShapeDtypeStruct((batch, 4, S // 4, D_MODEL), BF16),
                   jax.ShapeDtypeStruct((batch, 16, S // 16, D_MODEL), BF16)],
        scratch_shapes=[pltpu.VMEM((tm, D_MODEL), BF16), pltpu.VMEM((N_SLABS, tm, LANES), F32)],
        compiler_params=_params(("parallel",)),
        name="tail0",
    )(hf, hb, o, hnorm, x2, w_out, nf, wg, wu, wd, npl, pg, p2, pproj, nnext)


def _tail1(o0, o1, o2, l0, l1, l2, x2, w_out, nf, wg, wu, wd, npl, pg, p2, pproj, nfin, tm, batch):
    T = x2.shape[0]
    S = T // batch
    row = lambda n: pl.BlockSpec((tm, n), lambda i: (i, 0))
    return pl.pallas_call(
        _tail1_kernel,
        grid=(T // tm,),
        in_specs=[row(D_MODEL), _residue_spec(4, tm, D_MODEL, S), _residue_spec(16, tm, D_MODEL, S),
                  row(LANES), _residue_spec(4, tm, LANES, S), _residue_spec(16, tm, LANES, S), row(D_MODEL)]
        + _tail_weight_specs() + [row(PLE_DIM), _const_spec((PLE_DIM, D_MODEL)), _const_spec((1, D_MODEL))],
        out_specs=row(D_MODEL),
        out_shape=jax.ShapeDtypeStruct((T, D_MODEL), F32),
        scratch_shapes=[pltpu.VMEM((tm, D_MODEL), BF16), pltpu.VMEM((N_SLABS, tm, LANES), F32),
                        pltpu.VMEM((N_SLABS, tm, LANES), F32), pltpu.VMEM((tm, LANES), F32),
                        pltpu.VMEM((tm, LANES), F32)],
        compiler_params=_params(("parallel",)),
        name="tail1",
    )(o0, o1, o2, l0, l1, l2, x2, w_out, nf, wg, wu, wd, npl, pg, p2, pproj, nfin)


def _rope_tab_kernel(pos_ref, invf_ref, cos_ref, sin_ref, cos4_ref, sin4_ref, cos16_ref, sin16_ref):
    ang = pos_ref[...].astype(F32) * invf_ref[...]
    lane = lax.broadcasted_iota(jnp.int32, ang.shape, 1)
    s = jnp.sin(ang)
    cos_ref[...] = jnp.cos(ang)
    sin_ref[...] = jnp.where(lane < LANES // 2, -s, s)
    tm = ang.shape[0]
    for dil, c_ref, s_ref in ((4, cos4_ref, sin4_ref), (16, cos16_ref, sin16_ref)):
        for r in range(dil):
            rows = pl.ds(r, tm // dil, stride=dil)
            c_ref[r] = cos_ref[rows, :]
            s_ref[r] = sin_ref[rows, :]


def _rope_tables(pos2, invf_lane, tm, batch):
    T = pos2.shape[0]
    S = T // batch
    row = pl.BlockSpec((tm, LANES), lambda i: (i, 0))
    res = lambda dil: jax.ShapeDtypeStruct((batch, dil, S // dil, LANES), F32)
    return pl.pallas_call(
        _rope_tab_kernel,
        grid=(T // tm,),
        in_specs=[pl.BlockSpec((tm, 1), lambda i: (i, 0)), _const_spec((1, LANES))],
        out_specs=[row, row] + [_residue_spec(4, tm, LANES, S)] * 2 + [_residue_spec(16, tm, LANES, S)] * 2,
        out_shape=[jax.ShapeDtypeStruct((T, LANES), F32)] * 2 + [res(4)] * 2 + [res(16)] * 2,
        compiler_params=_params(("parallel",)),
        name="rope_tab",
    )(pos2, invf_lane)


MXU_COLS = 256
LOG2E = 1.4426950408889634
LN2 = 0.6931471805599453


def _in_proj1_kernel(hn_ref, w_ref, cos_ref, sin_ref, out_ref):
    j = pl.program_id(0)
    scale = jnp.where(j == 0, (A_DH ** -0.5) * LOG2E, 1.0).astype(F32)
    c = jnp.where(j == 2, 1.0, cos_ref[...] * scale)
    s = jnp.where(j == 2, 0.0, sin_ref[...] * scale)
    hn = hn_ref[...]
    for nb in range(D_MODEL // MXU_COLS):
        acc = _dot(hn, w_ref[:, nb * MXU_COLS:(nb + 1) * MXU_COLS])
        for half in range(MXU_COLS // A_DH):
            seg = acc[:, half * A_DH:(half + 1) * A_DH]
            col = nb * MXU_COLS + half * A_DH
            out_ref[:, col:col + A_DH] = (seg * c + pltpu.roll(seg, A_DH // 2, 1) * s).astype(BF16)


def _in_proj1(hn, w, g, cos_t, sin_t, tm):
    T = hn.shape[0]
    return pl.pallas_call(
        _in_proj1_kernel,
        grid=(3, T // tm),
        in_specs=[pl.BlockSpec((tm, D_MODEL), lambda c, i: (i, 0)),
                  pl.BlockSpec((D_MODEL, D_MODEL), lambda c, i: (0, 3 * g + c)),
                  pl.BlockSpec((tm, LANES), lambda c, i: (i, 0)),
                  pl.BlockSpec((tm, LANES), lambda c, i: (i, 0))],
        out_specs=pl.BlockSpec((tm, D_MODEL), lambda c, i: (i, c)),
        out_shape=jax.ShapeDtypeStruct((T, 3 * D_MODEL), BF16),
        compiler_params=_params(("parallel", "parallel")),
        name=f"in_proj1_g{g}",
    )(hn, w, cos_t, sin_t)


ATT_SUB = 128


def _attn_kernel(q_ref, kp_ref, kc_ref, kn_ref, vp_ref, vc_ref, vn_ref, o_ref, lse_ref, kall, vall,
                 *, blkq, n_keys):
    i = pl.program_id(2)
    kall[0:RADIUS, :] = kp_ref[...]
    kall[RADIUS:RADIUS + blkq, :] = kc_ref[...]
    kall[RADIUS + blkq:, :] = kn_ref[...]
    vall[0:RADIUS, :] = vp_ref[...]
    vall[RADIUS:RADIUS + blkq, :] = vc_ref[...]
    vall[RADIUS + blkq:, :] = vn_ref[...]
    nk = ATT_SUB + 2 * RADIUS
    r = lax.broadcasted_iota(jnp.int32, (ATT_SUB, nk), 0)
    c = lax.broadcasted_iota(jnp.int32, (ATT_SUB, nk), 1)
    band = jnp.abs(c - RADIUS - r) <= RADIUS
    lane = lax.broadcasted_iota(jnp.int32, (ATT_SUB, LANES), 1)
    ones = jnp.ones((nk, A_DH), BF16)
    for a in range(blkq // ATT_SUB):
        key0 = i * blkq + a * ATT_SUB - RADIUS
        valid = band & (c + key0 >= 0) & (c + key0 < n_keys)
        rows = slice(a * ATT_SUB, (a + 1) * ATT_SUB)
        krows = slice(a * ATT_SUB, a * ATT_SUB + nk)
        lse_tile = jnp.zeros((ATT_SUB, LANES), F32)
        for h in range(A_HEADS):
            sl = slice(h * A_DH, (h + 1) * A_DH)
            s = jnp.where(valid, _dot_nt(q_ref[rows, sl], kall[krows, sl]), NEG_INF)
            m = jnp.max(s, axis=1, keepdims=True)
            p = jnp.exp2(s - m).astype(BF16)
            pv = _dot(p, jnp.concatenate([vall[krows, sl], ones], axis=1))
            den = pv[:, A_DH:]
            o_ref[rows, sl] = (pv[:, :A_DH] / den).astype(BF16)
            lse_tile = jnp.where(lane == h, (m + jnp.log2(den)) * LN2, lse_tile)
        lse_ref[rows, :] = lse_tile


def _attention(proj, blkq):
    B, dil, U, _ = proj.shape
    blkq = min(blkq, U)
    hb = blkq // RADIUS
    nhalo = U // RADIUS

    def cur(j):
        return pl.BlockSpec((None, None, blkq, D_MODEL), lambda b, r, i: (b, r, i, j))

    def prev(j):
        return pl.BlockSpec((None, None, RADIUS, D_MODEL),
                            lambda b, r, i: (b, r, jnp.maximum(i * hb - 1, 0), j))

    def nxt(j):
        return pl.BlockSpec((None, None, RADIUS, D_MODEL),
                            lambda b, r, i: (b, r, jnp.minimum((i + 1) * hb, nhalo - 1), j))

    return pl.pallas_call(
        functools.partial(_attn_kernel, blkq=blkq, n_keys=U),
        grid=(B, dil, U // blkq),
        in_specs=[cur(0), prev(1), cur(1), nxt(1), prev(2), cur(2), nxt(2)],
        out_specs=[pl.BlockSpec((None, None, blkq, D_MODEL), lambda b, r, i: (b, r, i, 0)),
                   pl.BlockSpec((None, None, blkq, LANES), lambda b, r, i: (b, r, i, 0))],
        out_shape=[jax.ShapeDtypeStruct((B, dil, U, D_MODEL), BF16),
                   jax.ShapeDtypeStruct((B, dil, U, LANES), F32)],
        scratch_shapes=[pltpu.VMEM((blkq + 2 * RADIUS, D_MODEL), BF16)] * 2,
        compiler_params=_params(("parallel", "parallel", "parallel")),
        name=f"attn_d{dil}",
    )(proj, proj, proj, proj, proj, proj, proj)


def kernel(x, p, positions, norm_mix, a_w_in, a_gate_bias, a_head_norm, a_w_out, b_w_in, b_w_out,
           norm_ffn, w_gate, w_up, w_down, norm_ple, ple_gate, ple_proj, final_norm):
    B, S, _ = x.shape
    T = B * S
    bf = lambda w: w.astype(BF16)
    vec = lambda w: w.reshape(1, -1).astype(F32)
    x2 = x.reshape(T, D_MODEL)
    nc = S // M_CHUNK

    w_in = bf(a_w_in[0])
    w_main = jnp.concatenate([w_in[:, :M_QK_ALL], w_in[:, 2 * M_QK_ALL:2 * M_QK_ALL + 2 * D_MODEL]], axis=1)
    q, ktd, v, o, feat = _in_proj0(x2, vec(norm_mix[0]), w_main, w_in[:, M_QK_ALL:2 * M_QK_ALL].T,
                                   w_in[:, 2 * M_QK_ALL + 2 * D_MODEL:], vec(a_gate_bias[0]), tm=512)
    by_chunk = feat.reshape(N_FEAT, B, nc, M_CHUNK, N_GATE)
    pair_rows = lambda z: jnp.swapaxes(z, 2, 3).reshape(B, nc, N_GATE // 2, LANES)
    per_chunk = lambda z: jnp.broadcast_to(z[:, :, 0, :, None], (B, nc, N_GATE, LANES))
    rfeat = jnp.concatenate([pair_rows(by_chunk[4]), pair_rows(by_chunk[5]),
                             per_chunk(by_chunk[2]), per_chunk(by_chunk[3])], axis=2)
    hf, hb = _mlstm(q.reshape(B, S, -1), ktd.reshape(B, nc, M_QK_ALL, LANES), v.reshape(B, S, -1),
                    feat.reshape(N_FEAT, B, S, N_GATE), rfeat, tb=256)
    h, *hns = _tail0(hf.reshape(T, -1), hb.reshape(T, -1), o, vec(a_head_norm[0]), x2, bf(a_w_out[0]),
                     vec(norm_ffn[0]), bf(w_gate[0]), bf(w_up[0]), bf(w_down[0]), vec(norm_ple[0]),
                     bf(ple_gate[0]), p[0].reshape(T, PLE_DIM), bf(ple_proj[0]), vec(norm_mix[1]),
                     tm=256, batch=B)

    inv_freq = ROPE_THETA ** (-jnp.arange(0, ROPE_DIM, 2, dtype=F32) / ROPE_DIM)
    invf_lane = (jnp.zeros((1, LANES), F32).at[0, :ROPE_HALF].set(inv_freq)
                 .at[0, LANES // 2:LANES // 2 + ROPE_HALF].set(inv_freq))
    tabs = _rope_tables(positions.reshape(T, 1), invf_lane, tm=1024, batch=B)
    dim_order = jnp.concatenate([jnp.arange(0, ROPE_HALF), jnp.arange(ROPE_DIM, ROPE_DIM + 48),
                                 jnp.arange(ROPE_HALF, ROPE_DIM), jnp.arange(ROPE_DIM + 48, A_DH)])
    w1 = bf(b_w_in[0]).reshape(D_MODEL, N_GROUPS, 3, A_HEADS, A_DH)
    w1 = jnp.concatenate([w1[:, :, :2][..., dim_order], w1[:, :, 2:]], axis=2).reshape(D_MODEL, -1)
    o_g, l_g = [], []
    for g, (_, dil) in enumerate(DILATED_GROUPS):
        proj = _in_proj1(hns[g].reshape(T, D_MODEL), w1, g, tabs[2 * g].reshape(T, LANES),
                         tabs[2 * g + 1].reshape(T, LANES), tm=1024)
        og, lg = _attention(proj.reshape(B, dil, S // dil, 3 * D_MODEL), blkq=512)
        o_g.append(og)
        l_g.append(lg)
    out = _tail1(o_g[0].reshape(T, D_MODEL), o_g[1], o_g[2], l_g[0].reshape(T, LANES), l_g[1], l_g[2], h,
                 bf(b_w_out[0]), vec(norm_ffn[1]), bf(w_gate[1]), bf(w_up[1]), bf(w_down[1]),
                 vec(norm_ple[1]), bf(ple_gate[1]), p[1].reshape(T, PLE_DIM), bf(ple_proj[1]),
                 vec(final_norm), tm=256, batch=B)
    return out.reshape(B, S, D_MODEL)
```

```python
import functools

import jax
import jax.numpy as jnp
from jax import lax
from jax.experimental import pallas as pl
from jax.experimental.pallas import tpu as pltpu

F32 = jnp.float32
BF16 = jnp.bfloat16

D_MODEL = 1024
LANES = 128
EPS = 1e-6

M_HEADS = 8
M_QK = 64
M_V = 128
M_CHUNK = 64
M_QK_ALL = M_HEADS * M_QK

A_HEADS = 8
A_DH = 128
RADIUS = 64
DILATED_GROUPS = ((128, 1), (512, 4), (2048, 16))
N_GROUPS = 3
ROPE_DIM = 32
ROPE_HALF = 16
ROPE_THETA = 500000.0
NEG_INF = -1e30

FFN_HIDDEN = 2816
PLE_DIM = 256

VMEM_LIMIT = 56 * 1024 * 1024


def _dot(a, b):
    return jnp.dot(a, b, preferred_element_type=F32)


def _dot_nt(a, b):
    return lax.dot_general(a, b, (((1,), (1,)), ((), ())), preferred_element_type=F32)


def _rms(x, w):
    ms = jnp.mean(x * x, axis=-1, keepdims=True)
    return x * lax.rsqrt(ms + EPS) * w


def _const_spec(shape):
    nd = len(shape)
    return pl.BlockSpec(shape, lambda *_: (0,) * nd, pipeline_mode=pl.Buffered(1))


def _params(sem):
    return pltpu.CompilerParams(dimension_semantics=sem, vmem_limit_bytes=VMEM_LIMIT)


N_DIRS = 2
N_GATE = N_DIRS * M_HEADS
N_FEAT = 6


def _chunk_scan(x, op, fill, reverse):
    n = x.shape[0]
    pos = lax.broadcasted_iota(jnp.int32, x.shape, 0) % M_CHUNK
    shift = 1
    while shift < M_CHUNK:
        if reverse:
            shifted, ok = pltpu.roll(x, n - shift, 0), pos < M_CHUNK - shift
        else:
            shifted, ok = pltpu.roll(x, shift, 0), pos >= shift
        x = op(x, jnp.where(ok, shifted, fill))
        shift *= 2
    return x


def _in_proj0_kernel(x_ref, nw_ref, w_ref, wkt_ref, wg_ref, gb_ref, q_ref, ktd_ref, v_ref, o_ref, f_ref):
    tm = x_ref.shape[0]
    hn = _rms(x_ref[...], nw_ref[...]).astype(BF16)
    gates = _dot(hn, wg_ref[...]) + gb_ref[...]
    ig = gates[:, :N_GATE]
    fg = gates[:, N_GATE:]
    lf = jnp.minimum(fg, 0.0) - jnp.log1p(jnp.exp(-jnp.abs(fg)))
    is_fwd = lax.broadcasted_iota(jnp.int32, lf.shape, 1) < M_HEADS
    q_ref[...] = (_dot(hn, w_ref[:, 0:M_QK_ALL]) * (M_QK ** -0.5)).astype(BF16)
    pre = _chunk_scan(lf, jnp.add, 0.0, False)
    suf = _chunk_scan(lf, jnp.add, 0.0, True)
    b = jnp.where(is_fwd, pre, suf)
    g = pre + suf - lf
    w = ig - b
    v_ref[...] = _dot(hn, w_ref[:, M_QK_ALL:M_QK_ALL + D_MODEL]).astype(BF16)
    wpre = _chunk_scan(w, jnp.maximum, -jnp.inf, False)
    wsuf = _chunk_scan(w, jnp.maximum, -jnp.inf, True)
    for k, feat in enumerate((b, jnp.where(is_fwd, wpre, wsuf), g, jnp.maximum(wpre, wsuf), w, g + w)):
        f_ref[k] = feat
    o_ref[...] = _dot(hn, w_ref[:, M_QK_ALL + D_MODEL:]).astype(BF16)
    hn_dup = jnp.concatenate([hn[c * M_CHUNK:(c + 1) * M_CHUNK] for c in range(tm // M_CHUNK) for _ in (0, 1)],
                             axis=0)
    ktd = _dot_nt(wkt_ref[...], hn_dup).astype(BF16)
    for c in range(tm // M_CHUNK):
        ktd_ref[c] = ktd[:, c * LANES:(c + 1) * LANES]


def _in_proj0(x2, nw, w_main, w_kt, w_gate, gate_bias, tm):
    T = x2.shape[0]
    row = lambda n: pl.BlockSpec((tm, n), lambda i: (i, 0))
    nch = tm // M_CHUNK
    return pl.pallas_call(
        _in_proj0_kernel,
        grid=(T // tm,),
        in_specs=[row(D_MODEL), _const_spec((1, D_MODEL)), _const_spec(w_main.shape), _const_spec(w_kt.shape),
                  _const_spec(w_gate.shape), _const_spec((1, 2 * N_GATE))],
        out_specs=[row(M_QK_ALL), pl.BlockSpec((nch, M_QK_ALL, LANES), lambda i: (i, 0, 0)),
                   row(D_MODEL), row(D_MODEL), pl.BlockSpec((N_FEAT, tm, N_GATE), lambda i: (0, i, 0))],
        out_shape=[jax.ShapeDtypeStruct((T, M_QK_ALL), BF16),
                   jax.ShapeDtypeStruct((T // M_CHUNK, M_QK_ALL, LANES), BF16),
                   jax.ShapeDtypeStruct((T, D_MODEL), BF16), jax.ShapeDtypeStruct((T, D_MODEL), BF16),
                   jax.ShapeDtypeStruct((N_FEAT, T, N_GATE), F32)],
        compiler_params=_params(("parallel",)),
        name="in_proj0",
    )(x2, nw, w_main, w_kt, w_gate, gate_bias)


N_PAIRS = M_HEADS // 2
HALF = LANES // 2
PAIR_W = 2 * 2 * M_V
ROW_W, ROW_A, ROW_G, ROW_WMAX, N_ROWF = 0, N_GATE // 2, N_GATE, 2 * N_GATE, 3 * N_GATE


def _lane_pair(x, col):
    lane = lax.broadcasted_iota(jnp.int32, (x.shape[0], LANES), 1)
    return jnp.where(lane < HALF, x[:, col:col + 1], x[:, col + 1:col + 2])


def _mlstm_chunk(q_ref, kt_ref, v_ref, cf_ref, rf_ref, h_ref, s_ref, sb_ref, mc_ref, mr_ref, sub, fwd):
    L = M_CHUNK
    r0 = pl.multiple_of(sub * L, L)
    rows = pl.ds(r0, L)
    lo = 0 if fwd else M_HEADS
    b_c = cf_ref[0, rows, :]
    m_r = mr_ref[...]
    mj = b_c + jnp.maximum(m_r, cf_ref[1, rows, :])
    u = b_c - mj
    iw = jnp.exp(b_c + m_r - mj)
    en = jnp.exp(-mj)
    mr_ref[...] = cf_ref[2, pl.ds(r0, 1), :] + jnp.maximum(m_r, cf_ref[3, pl.ds(r0, 1), :])
    rf = rf_ref[sub]
    m_c = mc_ref[...]
    g = rf[ROW_G + lo:ROW_G + lo + M_HEADS]
    m_new = g + jnp.maximum(m_c, rf[ROW_WMAX + lo:ROW_WMAX + lo + M_HEADS])
    decay = jnp.exp(g + m_c - m_new)
    mc_ref[...] = m_new

    low_half = lax.broadcasted_iota(jnp.int32, (1, LANES), 1) < HALF
    li = lax.broadcasted_iota(jnp.int32, (L, LANES), 0)
    si = lax.broadcasted_iota(jnp.int32, (L, LANES), 1) % HALF
    mask = (si <= li) if fwd else (si >= li)
    same_head = (lax.broadcasted_iota(jnp.int32, (LANES, LANES), 0) // HALF
                 == lax.broadcasted_iota(jnp.int32, (LANES, LANES), 1) // HALF)
    ones = jnp.ones((L, M_V), BF16)
    zeros = jnp.zeros((L, M_V), BF16)
    yield
    q2s, kbds, scores = [], [], []
    for p in range(N_PAIRS):
        q2s.append(q_ref[rows, p * LANES:(p + 1) * LANES])
        kbds.append(jnp.where(same_head, kt_ref[sub, p * LANES:(p + 1) * LANES, :], jnp.zeros((), BF16)))
        scores.append(_dot(q2s[p], kbds[p]))
    yield
    mains, inters = [], []
    for p in range(N_PAIRS):
        pr = lo // 2 + p
        w_pr = rf[ROW_W + pr:ROW_W + pr + 1]
        m_new_pr = jnp.where(low_half, m_new[2 * p:2 * p + 1], m_new[2 * p + 1:2 * p + 2])
        k_scale = jnp.exp(rf[ROW_A + pr:ROW_A + pr + 1] - m_new_pr)
        dw = jnp.exp(jnp.where(mask, _lane_pair(u, lo + 2 * p) + w_pr, -jnp.inf))
        sm = (scores[p] * dw).astype(BF16)
        kw = (kbds[p].astype(F32) * k_scale).astype(BF16)
        qi = (q2s[p].astype(F32) * _lane_pair(iw, lo + 2 * p)).astype(BF16)
        v0 = v_ref[rows, 2 * p * M_V:(2 * p + 1) * M_V]
        v1 = v_ref[rows, (2 * p + 1) * M_V:(2 * p + 2) * M_V]
        vbd = jnp.concatenate([jnp.concatenate([v0, ones, zeros, zeros], axis=1),
                               jnp.concatenate([zeros, zeros, v1, ones], axis=1)], axis=0)
        mains.append(_dot(jnp.concatenate([sm, kw], axis=0), vbd))
        inters.append(_dot(qi, sb_ref[p]))
    yield
    outs = []
    for p in range(N_PAIRS):
        out = mains[p][:L] + inters[p]
        for j in range(2):
            h = 2 * p + j
            num = out[:, j * 2 * M_V:j * 2 * M_V + M_V]
            den = out[:, j * 2 * M_V + M_V:(j + 1) * 2 * M_V]
            outs.append(num / jnp.maximum(jnp.abs(den), en[:, lo + h:lo + h + 1]))
            blk = (p, slice(j * M_QK, (j + 1) * M_QK), slice(j * 2 * M_V, (j + 1) * 2 * M_V))
            dec = jnp.concatenate([decay[h:h + 1]] * 2, axis=1)
            s_new = dec * s_ref[blk] + mains[p][L + j * M_QK:L + (j + 1) * M_QK,
                                                j * 2 * M_V:(j + 1) * 2 * M_V]
            s_ref[blk] = s_new
            sb_ref[blk] = s_new.astype(BF16)
    h_ref[rows, :] = jnp.concatenate(outs, axis=1).astype(h_ref.dtype)
    yield


N_MLSTM_STAGES = 4


def _mlstm_kernel(qf, ktf, vf, cff, rff, qb, ktb, vb, cfb, rfb, hf_ref, hb_ref,
                  sf, sbf, mcf, mrf, sb, sbb, mcb, mrb, *, nsub):
    @pl.when(pl.program_id(1) == 0)
    def _():
        for ref in (sf, sbf, mcf, mrf, sb, sbb, mcb, mrb):
            ref[...] = jnp.zeros_like(ref)

    def body(j, carry):
        chunks = [_mlstm_chunk(qf, ktf, vf, cff, rff, hf_ref, sf, sbf, mcf, mrf, j, True),
                  _mlstm_chunk(qb, ktb, vb, cfb, rfb, hb_ref, sb, sbb, mcb, mrb, nsub - 1 - j, False)]
        for _ in range(N_MLSTM_STAGES):
            for chunk in chunks:
                next(chunk)
        return carry

    lax.fori_loop(0, nsub, body, 0)


def _mlstm(q, ktd, v, cfeat, rfeat, tb):
    B, S, _ = q.shape
    n = S // tb
    nsub = tb // M_CHUNK
    fw = lambda b, c: (b, c, 0)
    bw = lambda b, c: (b, n - 1 - c, 0)
    fw4 = lambda b, c: (b, c, 0, 0)
    bw4 = lambda b, c: (b, n - 1 - c, 0, 0)

    def specs(im, im4):
        return [pl.BlockSpec((None, tb, M_QK_ALL), im), pl.BlockSpec((None, nsub, M_QK_ALL, LANES), im4),
                pl.BlockSpec((None, tb, D_MODEL), im),
                pl.BlockSpec((4, None, tb, N_GATE), lambda b, c: (0,) + im(b, c)),
                pl.BlockSpec((None, nsub, N_ROWF, LANES), im4)]

    state = [pltpu.VMEM((N_PAIRS, LANES, PAIR_W), F32), pltpu.VMEM((N_PAIRS, LANES, PAIR_W), BF16),
             pltpu.VMEM((M_HEADS, LANES), F32), pltpu.VMEM((1, N_GATE), F32)]
    return pl.pallas_call(
        functools.partial(_mlstm_kernel, nsub=nsub),
        grid=(B, n),
        in_specs=specs(fw, fw4) + specs(bw, bw4),
        out_specs=[pl.BlockSpec((None, tb, D_MODEL), fw), pl.BlockSpec((None, tb, D_MODEL), bw)],
        out_shape=[jax.ShapeDtypeStruct((B, S, D_MODEL), BF16)] * 2,
        scratch_shapes=state + state,
        compiler_params=_params(("parallel", "arbitrary")),
        name="mlstm",
    )(q, ktd, v, cfeat, rfeat, q, ktd, v, cfeat, rfeat)


def _sigmoid(x):
    return 0.5 * jnp.tanh(0.5 * x) + 0.5


def _tail_common(mix_pair, x_ref, w_out_ref, nf_ref, wg_ref, wu_ref, wd_ref, npl_ref, pg_ref, p_ref, pp_ref):
    emb = _dot(p_ref[...].astype(BF16), pp_ref[...])
    h1 = x_ref[...]
    for i in range(D_MODEL // MXU_COLS):
        h1 = h1 + _dot(mix_pair(i), w_out_ref[i * MXU_COLS:(i + 1) * MXU_COLS, :])
    hn = _rms(h1, nf_ref[...]).astype(BF16)
    gate = _dot(hn, wg_ref[...])
    act = (gate * _sigmoid(gate) * _dot(hn, wu_ref[...])).astype(BF16)
    h2 = h1 + _dot(act, wd_ref[...])
    pgate = _sigmoid(_dot(_rms(h2, npl_ref[...]).astype(BF16), pg_ref[...]))
    return h2 + pgate * emb


N_SLABS = D_MODEL // LANES


def _tail0_kernel(hf_ref, hb_ref, o_ref, hnorm_ref, x_ref, w_out_ref, nf_ref, wg_ref, wu_ref, wd_ref,
                  npl_ref, pg_ref, p_ref, pp_ref, nnext_ref, out_ref, hn_ref, hn4_ref, hn16_ref, slab_ref,
                  slab4_ref):
    def head(h):
        sl = slice(h * M_V, (h + 1) * M_V)
        hh = hf_ref[:, sl].astype(F32) + hb_ref[:, sl].astype(F32)
        hh = hh * lax.rsqrt(jnp.mean(hh * hh, axis=-1, keepdims=True) + EPS) * hnorm_ref[:, sl]
        return (_sigmoid(o_ref[:, sl].astype(F32)) * hh).astype(BF16)

    h3 = _tail_common(lambda i: jnp.concatenate([head(2 * i), head(2 * i + 1)], axis=1), x_ref, w_out_ref,
                      nf_ref, wg_ref, wu_ref, wd_ref, npl_ref, pg_ref, p_ref, pp_ref)
    out_ref[...] = h3
    hn = _rms(h3, nnext_ref[...])
    hn_ref[...] = hn.astype(BF16)
    tm = hn.shape[0]
    q4 = tm // 4
    for s in range(N_SLABS):
        slab_ref[s] = hn[:, s * LANES:(s + 1) * LANES]
    for r4 in range(4):
        parts = [slab_ref[s, pl.ds(r4, q4, stride=4), :] for s in range(N_SLABS)]
        hn4_ref[r4] = jnp.concatenate(parts, axis=1).astype(BF16)
        for s in range(N_SLABS):
            slab4_ref[s, r4 * q4:(r4 + 1) * q4, :] = parts[s]
    for r16 in range(16):
        start = (r16 % 4) * q4 + r16 // 4
        hn16_ref[r16] = jnp.concatenate([slab4_ref[s, pl.ds(start, tm // 16, stride=4), :]
                                         for s in range(N_SLABS)], axis=1).astype(BF16)


def _tail1_kernel(o0_ref, o1_ref, o2_ref, l0_ref, l1_ref, l2_ref, x_ref, w_out_ref, nf_ref, wg_ref, wu_ref,
                  wd_ref, npl_ref, pg_ref, p_ref, pp_ref, nfin_ref, out_ref, slab1_ref, slab2_ref,
                  ls1_ref, ls2_ref):
    tm = o0_ref.shape[0]
    for dil, o_ref, l_ref, slab_ref, ls_ref in ((4, o1_ref, l1_ref, slab1_ref, ls1_ref),
                                                (16, o2_ref, l2_ref, slab2_ref, ls2_ref)):
        for r in range(dil):
            rows = pl.ds(r, tm // dil, stride=dil)
            ls_ref[rows, :] = l_ref[r]
            o_r = o_ref[r].astype(F32)
            for s in range(N_SLABS):
                slab_ref[s, rows, :] = o_r[:, s * LANES:(s + 1) * LANES]
    l0 = l0_ref[...]
    l1 = ls1_ref[...]
    l2 = ls2_ref[...]
    mx = jnp.maximum(jnp.maximum(l0, l1), l2)
    e0 = jnp.exp(l0 - mx)
    e1 = jnp.exp(l1 - mx)
    e2 = jnp.exp(l2 - mx)
    inv = 1.0 / (e0 + e1 + e2)
    w0 = e0 * inv
    w1 = e1 * inv
    w2 = e2 * inv
    def head(h):
        sl = slice(h * A_DH, (h + 1) * A_DH)
        return (w0[:, h:h + 1] * o0_ref[:, sl].astype(F32) + w1[:, h:h + 1] * slab1_ref[h]
                + w2[:, h:h + 1] * slab2_ref[h]).astype(BF16)

    h3 = _tail_common(lambda i: jnp.concatenate([head(2 * i), head(2 * i + 1)], axis=1), x_ref, w_out_ref,
                      nf_ref, wg_ref, wu_ref, wd_ref, npl_ref, pg_ref, p_ref, pp_ref)
    out_ref[...] = _rms(h3, nfin_ref[...])


def _tail_weight_specs():
    return [_const_spec((D_MODEL, D_MODEL)), _const_spec((1, D_MODEL)),
            _const_spec((D_MODEL, FFN_HIDDEN)), _const_spec((D_MODEL, FFN_HIDDEN)),
            _const_spec((FFN_HIDDEN, D_MODEL)), _const_spec((1, D_MODEL)), _const_spec((D_MODEL, D_MODEL))]


def _residue_spec(dil, tm, n, seq):
    nb = seq // tm
    return pl.BlockSpec((None, dil, tm // dil, n), lambda i: (i // nb, 0, i % nb, 0))


def _tail0(hf, hb, o, hnorm, x2, w_out, nf, wg, wu, wd, npl, pg, p2, pproj, nnext, tm, batch):
    T = x2.shape[0]
    S = T // batch
    row = lambda n: pl.BlockSpec((tm, n), lambda i: (i, 0))
    return pl.pallas_call(
        _tail0_kernel,
        grid=(T // tm,),
        in_specs=[row(D_MODEL), row(D_MODEL), row(D_MODEL), _const_spec((1, D_MODEL)), row(D_MODEL)]
        + _tail_weight_specs() + [row(PLE_DIM), _const_spec((PLE_DIM, D_MODEL)), _const_spec((1, D_MODEL))],
        out_specs=[row(D_MODEL), row(D_MODEL), _residue_spec(4, tm, D_MODEL, S),
                   _residue_spec(16, tm, D_MODEL, S)],
        out_shape=[jax.ShapeDtypeStruct((T, D_MODEL), F32), jax.ShapeDtypeStruct((T, D_MODEL), BF16),
                   jax.ShapeDtypeStruct((batch, 4, S // 4, D_MODEL), BF16),
                   jax.ShapeDtypeStruct((batch, 16, S // 16, D_MODEL), BF16)],
        scratch_shapes=[pltpu.VMEM((N_SLABS, tm, LANES), F32)] * 2,
        compiler_params=_params(("parallel",)),
        name="tail0",
    )(hf, hb, o, hnorm, x2, w_out, nf, wg, wu, wd, npl, pg, p2, pproj, nnext)


def _tail1(o0, o1, o2, l0, l1, l2, x2, w_out, nf, wg, wu, wd, npl, pg, p2, pproj, nfin, tm, batch):
    T = x2.shape[0]
    S = T // batch
    row = lambda n: pl.BlockSpec((tm, n), lambda i: (i, 0))
    return pl.pallas_call(
        _tail1_kernel,
        grid=(T // tm,),
        in_specs=[row(D_MODEL), _residue_spec(4, tm, D_MODEL, S), _residue_spec(16, tm, D_MODEL, S),
                  row(LANES), _residue_spec(4, tm, LANES, S), _residue_spec(16, tm, LANES, S), row(D_MODEL)]
        + _tail_weight_specs() + [row(PLE_DIM), _const_spec((PLE_DIM, D_MODEL)), _const_spec((1, D_MODEL))],
        out_specs=row(D_MODEL),
        out_shape=jax.ShapeDtypeStruct((T, D_MODEL), F32),
        scratch_shapes=[pltpu.VMEM((N_SLABS, tm, LANES), F32),
                        pltpu.VMEM((N_SLABS, tm, LANES), F32), pltpu.VMEM((tm, LANES), F32),
                        pltpu.VMEM((tm, LANES), F32)],
        compiler_params=_params(("parallel",)),
        name="tail1",
    )(o0, o1, o2, l0, l1, l2, x2, w_out, nf, wg, wu, wd, npl, pg, p2, pproj, nfin)


def _rope_tab_kernel(pos_ref, invf_ref, cos_ref, sin_ref, cos4_ref, sin4_ref, cos16_ref, sin16_ref):
    ang = pos_ref[...].astype(F32) * invf_ref[...]
    lane = lax.broadcasted_iota(jnp.int32, ang.shape, 1)
    s = jnp.sin(ang)
    cos_ref[...] = jnp.cos(ang)
    sin_ref[...] = jnp.where(lane < LANES // 2, -s, s)
    tm = ang.shape[0]
    for r4 in range(4):
        rows = pl.ds(r4, tm // 4, stride=4)
        cos4_ref[r4] = cos_ref[rows, :]
        sin4_ref[r4] = sin_ref[rows, :]
    for r16 in range(16):
        rows = pl.ds(r16 // 4, tm // 16, stride=4)
        cos16_ref[r16] = cos4_ref[r16 % 4, rows, :]
        sin16_ref[r16] = sin4_ref[r16 % 4, rows, :]


def _rope_tables(pos2, invf_lane, tm, batch):
    T = pos2.shape[0]
    S = T // batch
    row = pl.BlockSpec((tm, LANES), lambda i: (i, 0))
    res = lambda dil: jax.ShapeDtypeStruct((batch, dil, S // dil, LANES), F32)
    return pl.pallas_call(
        _rope_tab_kernel,
        grid=(T // tm,),
        in_specs=[pl.BlockSpec((tm, 1), lambda i: (i, 0)), _const_spec((1, LANES))],
        out_specs=[row, row] + [_residue_spec(4, tm, LANES, S)] * 2 + [_residue_spec(16, tm, LANES, S)] * 2,
        out_shape=[jax.ShapeDtypeStruct((T, LANES), F32)] * 2 + [res(4)] * 2 + [res(16)] * 2,
        compiler_params=_params(("parallel",)),
        name="rope_tab",
    )(pos2, invf_lane)


MXU_COLS = 256
LOG2E = 1.4426950408889634
LN2 = 0.6931471805599453


def _in_proj1_kernel(hn_ref, w_ref, cos_ref, sin_ref, out_ref):
    j = pl.program_id(0)
    scale = jnp.where(j == 0, (A_DH ** -0.5) * LOG2E, 1.0).astype(F32)
    c = jnp.where(j == 2, 1.0, cos_ref[...] * scale)
    s = jnp.where(j == 2, 0.0, sin_ref[...] * scale)
    hn = hn_ref[...]
    for nb in range(D_MODEL // MXU_COLS):
        acc = _dot(hn, w_ref[:, nb * MXU_COLS:(nb + 1) * MXU_COLS])
        for half in range(MXU_COLS // A_DH):
            seg = acc[:, half * A_DH:(half + 1) * A_DH]
            col = nb * MXU_COLS + half * A_DH
            out_ref[:, col:col + A_DH] = (seg * c + pltpu.roll(seg, A_DH // 2, 1) * s).astype(BF16)


def _in_proj1(hn, w, g, cos_t, sin_t, tm):
    T = hn.shape[0]
    return pl.pallas_call(
        _in_proj1_kernel,
        grid=(3, T // tm),
        in_specs=[pl.BlockSpec((tm, D_MODEL), lambda c, i: (i, 0)),
                  pl.BlockSpec((D_MODEL, D_MODEL), lambda c, i: (0, 3 * g + c)),
                  pl.BlockSpec((tm, LANES), lambda c, i: (i, 0)),
                  pl.BlockSpec((tm, LANES), lambda c, i: (i, 0))],
        out_specs=pl.BlockSpec((tm, D_MODEL), lambda c, i: (i, c)),
        out_shape=jax.ShapeDtypeStruct((T, 3 * D_MODEL), BF16),
        compiler_params=_params(("parallel", "parallel")),
        name=f"in_proj1_g{g}",
    )(hn, w, cos_t, sin_t)


ATT_SUB = 128


def _attn_kernel(q_ref, kp_ref, kc_ref, kn_ref, vp_ref, vc_ref, vn_ref, o_ref, lse_ref, kall, vall,
                 *, blkq, n_keys):
    i = pl.program_id(2)
    kall[0:RADIUS, :] = kp_ref[...]
    kall[RADIUS:RADIUS + blkq, :] = kc_ref[...]
    kall[RADIUS + blkq:, :] = kn_ref[...]
    vall[0:RADIUS, :] = vp_ref[...]
    vall[RADIUS:RADIUS + blkq, :] = vc_ref[...]
    vall[RADIUS + blkq:, :] = vn_ref[...]
    nk = ATT_SUB + 2 * RADIUS
    r = lax.broadcasted_iota(jnp.int32, (ATT_SUB, nk), 0)
    c = lax.broadcasted_iota(jnp.int32, (ATT_SUB, nk), 1)
    band = jnp.abs(c - RADIUS - r) <= RADIUS
    lane = lax.broadcasted_iota(jnp.int32, (ATT_SUB, LANES), 1)
    ones = jnp.ones((nk, A_DH), BF16)
    for a in range(blkq // ATT_SUB):
        key0 = i * blkq + a * ATT_SUB - RADIUS
        valid = band & (c + key0 >= 0) & (c + key0 < n_keys)
        rows = slice(a * ATT_SUB, (a + 1) * ATT_SUB)
        krows = slice(a * ATT_SUB, a * ATT_SUB + nk)
        lse_tile = jnp.zeros((ATT_SUB, LANES), F32)
        for h in range(A_HEADS):
            sl = slice(h * A_DH, (h + 1) * A_DH)
            s = jnp.where(valid, _dot_nt(q_ref[rows, sl], kall[krows, sl]), NEG_INF)
            m = jnp.max(s, axis=1, keepdims=True)
            p = jnp.exp2(s - m).astype(BF16)
            pv = _dot(p, jnp.concatenate([vall[krows, sl], ones], axis=1))
            den = pv[:, A_DH:]
            o_ref[rows, sl] = (pv[:, :A_DH] / den).astype(BF16)
            lse_tile = jnp.where(lane == h, (m + jnp.log2(den)) * LN2, lse_tile)
        lse_ref[rows, :] = lse_tile


def _attention(proj, blkq):
    B, dil, U, _ = proj.shape
    blkq = min(blkq, U)
    hb = blkq // RADIUS
    nhalo = U // RADIUS

    def cur(j):
        return pl.BlockSpec((None, None, blkq, D_MODEL), lambda b, r, i: (b, r, i, j))

    def prev(j):
        return pl.BlockSpec((None, None, RADIUS, D_MODEL),
                            lambda b, r, i: (b, r, jnp.maximum(i * hb - 1, 0), j))

    def nxt(j):
        return pl.BlockSpec((None, None, RADIUS, D_MODEL),
                            lambda b, r, i: (b, r, jnp.minimum((i + 1) * hb, nhalo - 1), j))

    return pl.pallas_call(
        functools.partial(_attn_kernel, blkq=blkq, n_keys=U),
        grid=(B, dil, U // blkq),
        in_specs=[cur(0), prev(1), cur(1), nxt(1), prev(2), cur(2), nxt(2)],
        out_specs=[pl.BlockSpec((None, None, blkq, D_MODEL), lambda b, r, i: (b, r, i, 0)),
                   pl.BlockSpec((None, None, blkq, LANES), lambda b, r, i: (b, r, i, 0))],
        out_shape=[jax.ShapeDtypeStruct((B, dil, U, D_MODEL), BF16),
                   jax.ShapeDtypeStruct((B, dil, U, LANES), F32)],
        scratch_shapes=[pltpu.VMEM((blkq + 2 * RADIUS, D_MODEL), BF16)] * 2,
        compiler_params=_params(("parallel", "parallel", "parallel")),
        name=f"attn_d{dil}",
    )(proj, proj, proj, proj, proj, proj, proj)


def kernel(x, p, positions, norm_mix, a_w_in, a_gate_bias, a_head_norm, a_w_out, b_w_in, b_w_out,
           norm_ffn, w_gate, w_up, w_down, norm_ple, ple_gate, ple_proj, final_norm):
    B, S, _ = x.shape
    T = B * S
    bf = lambda w: w.astype(BF16)
    vec = lambda w: w.reshape(1, -1).astype(F32)
    x2 = x.reshape(T, D_MODEL)
    nc = S // M_CHUNK

    w_in = bf(a_w_in[0])
    w_main = jnp.concatenate([w_in[:, :M_QK_ALL], w_in[:, 2 * M_QK_ALL:2 * M_QK_ALL + 2 * D_MODEL]], axis=1)
    q, ktd, v, o, feat = _in_proj0(x2, vec(norm_mix[0]), w_main, w_in[:, M_QK_ALL:2 * M_QK_ALL].T,
                                   w_in[:, 2 * M_QK_ALL + 2 * D_MODEL:], vec(a_gate_bias[0]), tm=512)
    by_chunk = feat.reshape(N_FEAT, B, nc, M_CHUNK, N_GATE)
    pair_rows = lambda z: jnp.swapaxes(z, 2, 3).reshape(B, nc, N_GATE // 2, LANES)
    per_chunk = lambda z: jnp.broadcast_to(z[:, :, 0, :, None], (B, nc, N_GATE, LANES))
    rfeat = jnp.concatenate([pair_rows(by_chunk[4]), pair_rows(by_chunk[5]),
                             per_chunk(by_chunk[2]), per_chunk(by_chunk[3])], axis=2)
    hf, hb = _mlstm(q.reshape(B, S, -1), ktd.reshape(B, nc, M_QK_ALL, LANES), v.reshape(B, S, -1),
                    feat.reshape(N_FEAT, B, S, N_GATE), rfeat, tb=512)
    h, *hns = _tail0(hf.reshape(T, -1), hb.reshape(T, -1), o, vec(a_head_norm[0]), x2, bf(a_w_out[0]),
                     vec(norm_ffn[0]), bf(w_gate[0]), bf(w_up[0]), bf(w_down[0]), vec(norm_ple[0]),
                     bf(ple_gate[0]), p[0].reshape(T, PLE_DIM), bf(ple_proj[0]), vec(norm_mix[1]),
                     tm=256, batch=B)

    inv_freq = ROPE_THETA ** (-jnp.arange(0, ROPE_DIM, 2, dtype=F32) / ROPE_DIM)
    invf_lane = (jnp.zeros((1, LANES), F32).at[0, :ROPE_HALF].set(inv_freq)
                 .at[0, LANES // 2:LANES // 2 + ROPE_HALF].set(inv_freq))
    tabs = _rope_tables(positions.reshape(T, 1), invf_lane, tm=1024, batch=B)
    dim_order = jnp.concatenate([jnp.arange(0, ROPE_HALF), jnp.arange(ROPE_DIM, ROPE_DIM + 48),
                                 jnp.arange(ROPE_HALF, ROPE_DIM), jnp.arange(ROPE_DIM + 48, A_DH)])
    w1 = bf(b_w_in[0]).reshape(D_MODEL, N_GROUPS, 3, A_HEADS, A_DH)
    w1 = jnp.concatenate([w1[:, :, :2][..., dim_order], w1[:, :, 2:]], axis=2).reshape(D_MODEL, -1)
    o_g, l_g = [], []
    for g, (_, dil) in enumerate(DILATED_GROUPS):
        proj = _in_proj1(hns[g].reshape(T, D_MODEL), w1, g, tabs[2 * g].reshape(T, LANES),
                         tabs[2 * g + 1].reshape(T, LANES), tm=2048)
        og, lg = _attention(proj.reshape(B, dil, S // dil, 3 * D_MODEL), blkq=512)
        o_g.append(og)
        l_g.append(lg)
    out = _tail1(o_g[0].reshape(T, D_MODEL), o_g[1], o_g[2], l_g[0].reshape(T, LANES), l_g[1], l_g[2], h,
                 bf(b_w_out[0]), vec(norm_ffn[1]), bf(w_gate[1]), bf(w_up[1]), bf(w_down[1]),
                 vec(norm_ple[1]), bf(ple_gate[1]), p[1].reshape(T, PLE_DIM), bf(ple_proj[1]),
                 vec(final_norm), tm=256, batch=B)
    return out.reshape(B, S, D_MODEL)
```

```python
import functools

import jax
import jax.numpy as jnp
from jax import lax
from jax.experimental import pallas as pl
from jax.experimental.pallas import tpu as pltpu

F32 = jnp.float32
BF16 = jnp.bfloat16

D_MODEL = 1024
LANES = 128
EPS = 1e-6

M_HEADS = 8
M_QK = 64
M_V = 128
M_CHUNK = 64
M_QK_ALL = M_HEADS * M_QK

A_HEADS = 8
A_DH = 128
RADIUS = 64
DILATED_GROUPS = ((128, 1), (512, 4), (2048, 16))
N_GROUPS = 3
ROPE_DIM = 32
ROPE_HALF = 16
ROPE_THETA = 500000.0
NEG_INF = -1e30

FFN_HIDDEN = 2816
PLE_DIM = 256

VMEM_LIMIT = 60 * 1024 * 1024


def _dot(a, b):
    return jnp.dot(a, b, preferred_element_type=F32)


def _dot_nt(a, b):
    return lax.dot_general(a, b, (((1,), (1,)), ((), ())), preferred_element_type=F32)


def _rms(x, w):
    ms = jnp.mean(x * x, axis=-1, keepdims=True)
    return x * lax.rsqrt(ms + EPS) * w


def _const_spec(shape):
    nd = len(shape)
    return pl.BlockSpec(shape, lambda *_: (0,) * nd, pipeline_mode=pl.Buffered(1))


def _params(sem):
    return pltpu.CompilerParams(dimension_semantics=sem, vmem_limit_bytes=VMEM_LIMIT)


N_DIRS = 2
N_GATE = N_DIRS * M_HEADS
N_FEAT = 6


def _chunk_scan(x, op, fill, reverse):
    n = x.shape[0]
    pos = lax.broadcasted_iota(jnp.int32, x.shape, 0) % M_CHUNK
    shift = 1
    while shift < M_CHUNK:
        if reverse:
            shifted, ok = pltpu.roll(x, n - shift, 0), pos < M_CHUNK - shift
        else:
            shifted, ok = pltpu.roll(x, shift, 0), pos >= shift
        x = op(x, jnp.where(ok, shifted, fill))
        shift *= 2
    return x


def _in_proj0_kernel(x_ref, nw_ref, w_ref, wkt_ref, wg_ref, gb_ref, q_ref, ktd_ref, v_ref, o_ref, f_ref):
    tm = x_ref.shape[0]
    hn = _rms(x_ref[...], nw_ref[...]).astype(BF16)
    gates = _dot(hn, wg_ref[...]) + gb_ref[...]
    ig = gates[:, :N_GATE]
    fg = gates[:, N_GATE:]
    lf = jnp.minimum(fg, 0.0) - jnp.log1p(jnp.exp(-jnp.abs(fg)))
    is_fwd = lax.broadcasted_iota(jnp.int32, lf.shape, 1) < M_HEADS
    q_ref[...] = (_dot(hn, w_ref[:, 0:M_QK_ALL]) * (M_QK ** -0.5)).astype(BF16)
    pre = _chunk_scan(lf, jnp.add, 0.0, False)
    suf = _chunk_scan(lf, jnp.add, 0.0, True)
    b = jnp.where(is_fwd, pre, suf)
    g = pre + suf - lf
    w = ig - b
    v_ref[...] = _dot(hn, w_ref[:, M_QK_ALL:M_QK_ALL + D_MODEL]).astype(BF16)
    wpre = _chunk_scan(w, jnp.maximum, -jnp.inf, False)
    wsuf = _chunk_scan(w, jnp.maximum, -jnp.inf, True)
    for k, feat in enumerate((b, jnp.where(is_fwd, wpre, wsuf), g, jnp.maximum(wpre, wsuf), w, g + w)):
        f_ref[k] = feat
    o_ref[...] = _dot(hn, w_ref[:, M_QK_ALL + D_MODEL:]).astype(BF16)
    hn_dup = jnp.concatenate([hn[c * M_CHUNK:(c + 1) * M_CHUNK] for c in range(tm // M_CHUNK) for _ in (0, 1)],
                             axis=0)
    ktd = _dot_nt(wkt_ref[...], hn_dup).astype(BF16)
    for c in range(tm // M_CHUNK):
        ktd_ref[c] = ktd[:, c * LANES:(c + 1) * LANES]


def _in_proj0(x2, nw, w_main, w_kt, w_gate, gate_bias, tm):
    T = x2.shape[0]
    row = lambda n: pl.BlockSpec((tm, n), lambda i: (i, 0))
    nch = tm // M_CHUNK
    return pl.pallas_call(
        _in_proj0_kernel,
        grid=(T // tm,),
        in_specs=[row(D_MODEL), _const_spec((1, D_MODEL)), _const_spec(w_main.shape), _const_spec(w_kt.shape),
                  _const_spec(w_gate.shape), _const_spec((1, 2 * N_GATE))],
        out_specs=[row(M_QK_ALL), pl.BlockSpec((nch, M_QK_ALL, LANES), lambda i: (i, 0, 0)),
                   row(D_MODEL), row(D_MODEL), pl.BlockSpec((N_FEAT, tm, N_GATE), lambda i: (0, i, 0))],
        out_shape=[jax.ShapeDtypeStruct((T, M_QK_ALL), BF16),
                   jax.ShapeDtypeStruct((T // M_CHUNK, M_QK_ALL, LANES), BF16),
                   jax.ShapeDtypeStruct((T, D_MODEL), BF16), jax.ShapeDtypeStruct((T, D_MODEL), BF16),
                   jax.ShapeDtypeStruct((N_FEAT, T, N_GATE), F32)],
        compiler_params=_params(("parallel",)),
        name="in_proj0",
    )(x2, nw, w_main, w_kt, w_gate, gate_bias)


N_PAIRS = M_HEADS // 2
HALF = LANES // 2
PAIR_W = 2 * 2 * M_V
ROW_W, ROW_A, ROW_G, ROW_WMAX, N_ROWF = 0, N_GATE // 2, N_GATE, 2 * N_GATE, 3 * N_GATE


def _lane_pair(x, col):
    lane = lax.broadcasted_iota(jnp.int32, (x.shape[0], LANES), 1)
    return jnp.where(lane < HALF, x[:, col:col + 1], x[:, col + 1:col + 2])


def _mlstm_chunk(q_ref, kt_ref, v_ref, cf_ref, rf_ref, h_ref, s_ref, sb_ref, mc_ref, mr_ref, sub, fwd):
    L = M_CHUNK
    r0 = pl.multiple_of(sub * L, L)
    rows = pl.ds(r0, L)
    lo = 0 if fwd else M_HEADS
    b_c = cf_ref[0, rows, :]
    m_r = mr_ref[...]
    mj = b_c + jnp.maximum(m_r, cf_ref[1, rows, :])
    u = b_c - mj
    iw = jnp.exp(b_c + m_r - mj)
    en = jnp.exp(-mj)
    mr_ref[...] = cf_ref[2, pl.ds(r0, 1), :] + jnp.maximum(m_r, cf_ref[3, pl.ds(r0, 1), :])
    rf = rf_ref[sub]
    m_c = mc_ref[...]
    g = rf[ROW_G + lo:ROW_G + lo + M_HEADS]
    m_new = g + jnp.maximum(m_c, rf[ROW_WMAX + lo:ROW_WMAX + lo + M_HEADS])
    decay = jnp.exp(g + m_c - m_new)
    mc_ref[...] = m_new

    low_half = lax.broadcasted_iota(jnp.int32, (1, LANES), 1) < HALF
    li = lax.broadcasted_iota(jnp.int32, (L, LANES), 0)
    si = lax.broadcasted_iota(jnp.int32, (L, LANES), 1) % HALF
    mask = (si <= li) if fwd else (si >= li)
    same_head = (lax.broadcasted_iota(jnp.int32, (LANES, LANES), 0) // HALF
                 == lax.broadcasted_iota(jnp.int32, (LANES, LANES), 1) // HALF)
    ones = jnp.ones((L, M_V), BF16)
    zeros = jnp.zeros((L, M_V), BF16)
    yield
    q2s, kbds, scores = [], [], []
    for p in range(N_PAIRS):
        q2s.append(q_ref[rows, p * LANES:(p + 1) * LANES])
        kbds.append(jnp.where(same_head, kt_ref[sub, p * LANES:(p + 1) * LANES, :], jnp.zeros((), BF16)))
        scores.append(_dot(q2s[p], kbds[p]))
    yield
    mains, inters = [], []
    for p in range(N_PAIRS):
        pr = lo // 2 + p
        w_pr = rf[ROW_W + pr:ROW_W + pr + 1]
        m_new_pr = jnp.where(low_half, m_new[2 * p:2 * p + 1], m_new[2 * p + 1:2 * p + 2])
        k_scale = jnp.exp(rf[ROW_A + pr:ROW_A + pr + 1] - m_new_pr)
        dw = jnp.exp(jnp.where(mask, _lane_pair(u, lo + 2 * p) + w_pr, -jnp.inf))
        sm = (scores[p] * dw).astype(BF16)
        kw = (kbds[p].astype(F32) * k_scale).astype(BF16)
        qi = (q2s[p].astype(F32) * _lane_pair(iw, lo + 2 * p)).astype(BF16)
        v0 = v_ref[rows, 2 * p * M_V:(2 * p + 1) * M_V]
        v1 = v_ref[rows, (2 * p + 1) * M_V:(2 * p + 2) * M_V]
        vbd = jnp.concatenate([jnp.concatenate([v0, ones, zeros, zeros], axis=1),
                               jnp.concatenate([zeros, zeros, v1, ones], axis=1)], axis=0)
        mains.append(_dot(jnp.concatenate([sm, kw], axis=0), vbd))
        inters.append(_dot(qi, sb_ref[p]))
    yield
    outs = []
    for p in range(N_PAIRS):
        out = mains[p][:L] + inters[p]
        for j in range(2):
            h = 2 * p + j
            num = out[:, j * 2 * M_V:j * 2 * M_V + M_V]
            den = out[:, j * 2 * M_V + M_V:(j + 1) * 2 * M_V]
            outs.append(num / jnp.maximum(jnp.abs(den), en[:, lo + h:lo + h + 1]))
            blk = (p, slice(j * M_QK, (j + 1) * M_QK), slice(j * 2 * M_V, (j + 1) * 2 * M_V))
            dec = jnp.concatenate([decay[h:h + 1]] * 2, axis=1)
            s_new = dec * s_ref[blk] + mains[p][L + j * M_QK:L + (j + 1) * M_QK,
                                                j * 2 * M_V:(j + 1) * 2 * M_V]
            s_ref[blk] = s_new
            sb_ref[blk] = s_new.astype(BF16)
    h_ref[rows, :] = jnp.concatenate(outs, axis=1).astype(h_ref.dtype)
    yield


N_MLSTM_STAGES = 4


def _mlstm_kernel(qf, ktf, vf, cff, rff, qb, ktb, vb, cfb, rfb, hf_ref, hb_ref,
                  sf, sbf, mcf, mrf, sb, sbb, mcb, mrb, *, nsub):
    @pl.when(pl.program_id(1) == 0)
    def _():
        for ref in (sf, sbf, mcf, mrf, sb, sbb, mcb, mrb):
            ref[...] = jnp.zeros_like(ref)

    def body(j, carry):
        chunks = [_mlstm_chunk(qf, ktf, vf, cff, rff, hf_ref, sf, sbf, mcf, mrf, j, True),
                  _mlstm_chunk(qb, ktb, vb, cfb, rfb, hb_ref, sb, sbb, mcb, mrb, nsub - 1 - j, False)]
        for _ in range(N_MLSTM_STAGES):
            for chunk in chunks:
                next(chunk)
        return carry

    lax.fori_loop(0, nsub, body, 0)


def _mlstm(q, ktd, v, cfeat, rfeat, tb):
    B, S, _ = q.shape
    n = S // tb
    nsub = tb // M_CHUNK
    fw = lambda b, c: (b, c, 0)
    bw = lambda b, c: (b, n - 1 - c, 0)
    fw4 = lambda b, c: (b, c, 0, 0)
    bw4 = lambda b, c: (b, n - 1 - c, 0, 0)

    def specs(im, im4):
        return [pl.BlockSpec((None, tb, M_QK_ALL), im), pl.BlockSpec((None, nsub, M_QK_ALL, LANES), im4),
                pl.BlockSpec((None, tb, D_MODEL), im),
                pl.BlockSpec((4, None, tb, N_GATE), lambda b, c: (0,) + im(b, c)),
                pl.BlockSpec((None, nsub, N_ROWF, LANES), im4)]

    state = [pltpu.VMEM((N_PAIRS, LANES, PAIR_W), F32), pltpu.VMEM((N_PAIRS, LANES, PAIR_W), BF16),
             pltpu.VMEM((M_HEADS, LANES), F32), pltpu.VMEM((1, N_GATE), F32)]
    return pl.pallas_call(
        functools.partial(_mlstm_kernel, nsub=nsub),
        grid=(B, n),
        in_specs=specs(fw, fw4) + specs(bw, bw4),
        out_specs=[pl.BlockSpec((None, tb, D_MODEL), fw), pl.BlockSpec((None, tb, D_MODEL), bw)],
        out_shape=[jax.ShapeDtypeStruct((B, S, D_MODEL), BF16)] * 2,
        scratch_shapes=state + state,
        compiler_params=_params(("parallel", "arbitrary")),
        name="mlstm",
    )(q, ktd, v, cfeat, rfeat, q, ktd, v, cfeat, rfeat)


def _sigmoid(x):
    return 0.5 * jnp.tanh(0.5 * x) + 0.5


def _tail_common(mix_pair, x_ref, w_out_ref, nf_ref, wg_ref, wu_ref, wd_ref, npl_ref, pg_ref, p_ref, pp_ref):
    emb = _dot(p_ref[...].astype(BF16), pp_ref[...])
    h1 = x_ref[...]
    for i in range(D_MODEL // MXU_COLS):
        h1 = h1 + _dot(mix_pair(i), w_out_ref[i * MXU_COLS:(i + 1) * MXU_COLS, :])
    hn = _rms(h1, nf_ref[...]).astype(BF16)
    gate = _dot(hn, wg_ref[...])
    act = (gate * _sigmoid(gate) * _dot(hn, wu_ref[...])).astype(BF16)
    h2 = h1 + _dot(act, wd_ref[...])
    pgate = _sigmoid(_dot(_rms(h2, npl_ref[...]).astype(BF16), pg_ref[...]))
    return h2 + pgate * emb


N_SLABS = D_MODEL // LANES


def _tail0_kernel(hf_ref, hb_ref, o_ref, hnorm_ref, x_ref, w_out_ref, nf_ref, wg_ref, wu_ref, wd_ref,
                  npl_ref, pg_ref, p_ref, pp_ref, nnext_ref, out_ref, hn_ref, hn4_ref, hn16_ref, slab_ref,
                  slab4_ref):
    def head(h):
        sl = slice(h * M_V, (h + 1) * M_V)
        hh = hf_ref[:, sl].astype(F32) + hb_ref[:, sl].astype(F32)
        hh = hh * lax.rsqrt(jnp.mean(hh * hh, axis=-1, keepdims=True) + EPS) * hnorm_ref[:, sl]
        return (_sigmoid(o_ref[:, sl].astype(F32)) * hh).astype(BF16)

    h3 = _tail_common(lambda i: jnp.concatenate([head(2 * i), head(2 * i + 1)], axis=1), x_ref, w_out_ref,
                      nf_ref, wg_ref, wu_ref, wd_ref, npl_ref, pg_ref, p_ref, pp_ref)
    out_ref[...] = h3
    hn = _rms(h3, nnext_ref[...])
    hn_ref[...] = hn.astype(BF16)
    tm = hn.shape[0]
    q4 = tm // 4
    for s in range(N_SLABS):
        slab_ref[s] = hn[:, s * LANES:(s + 1) * LANES]
    for r4 in range(4):
        parts = [slab_ref[s, pl.ds(r4, q4, stride=4), :] for s in range(N_SLABS)]
        hn4_ref[r4] = jnp.concatenate(parts, axis=1).astype(BF16)
        for s in range(N_SLABS):
            slab4_ref[s, r4 * q4:(r4 + 1) * q4, :] = parts[s]
    for r16 in range(16):
        start = (r16 % 4) * q4 + r16 // 4
        hn16_ref[r16] = jnp.concatenate([slab4_ref[s, pl.ds(start, tm // 16, stride=4), :]
                                         for s in range(N_SLABS)], axis=1).astype(BF16)


def _tail1_kernel(o0_ref, o1_ref, o2_ref, l0_ref, l1_ref, l2_ref, x_ref, w_out_ref, nf_ref, wg_ref, wu_ref,
                  wd_ref, npl_ref, pg_ref, p_ref, pp_ref, nfin_ref, out_ref, slab1_ref, slab2_ref,
                  ls1_ref, ls2_ref):
    tm = o0_ref.shape[0]
    for dil, o_ref, l_ref, slab_ref, ls_ref in ((4, o1_ref, l1_ref, slab1_ref, ls1_ref),
                                                (16, o2_ref, l2_ref, slab2_ref, ls2_ref)):
        for r in range(dil):
            rows = pl.ds(r, tm // dil, stride=dil)
            ls_ref[rows, :] = l_ref[r]
            o_r = o_ref[r].astype(F32)
            for s in range(N_SLABS):
                slab_ref[s, rows, :] = o_r[:, s * LANES:(s + 1) * LANES]
    l0 = l0_ref[...]
    l1 = ls1_ref[...]
    l2 = ls2_ref[...]
    mx = jnp.maximum(jnp.maximum(l0, l1), l2)
    e0 = jnp.exp(l0 - mx)
    e1 = jnp.exp(l1 - mx)
    e2 = jnp.exp(l2 - mx)
    inv = 1.0 / (e0 + e1 + e2)
    w0 = e0 * inv
    w1 = e1 * inv
    w2 = e2 * inv
    def head(h):
        sl = slice(h * A_DH, (h + 1) * A_DH)
        return (w0[:, h:h + 1] * o0_ref[:, sl].astype(F32) + w1[:, h:h + 1] * slab1_ref[h]
                + w2[:, h:h + 1] * slab2_ref[h]).astype(BF16)

    h3 = _tail_common(lambda i: jnp.concatenate([head(2 * i), head(2 * i + 1)], axis=1), x_ref, w_out_ref,
                      nf_ref, wg_ref, wu_ref, wd_ref, npl_ref, pg_ref, p_ref, pp_ref)
    out_ref[...] = _rms(h3, nfin_ref[...])


def _tail_weight_specs():
    return [_const_spec((D_MODEL, D_MODEL)), _const_spec((1, D_MODEL)),
            _const_spec((D_MODEL, FFN_HIDDEN)), _const_spec((D_MODEL, FFN_HIDDEN)),
            _const_spec((FFN_HIDDEN, D_MODEL)), _const_spec((1, D_MODEL)), _const_spec((D_MODEL, D_MODEL))]


def _residue_spec(dil, tm, n, seq):
    nb = seq // tm
    return pl.BlockSpec((None, dil, tm // dil, n), lambda i: (i // nb, 0, i % nb, 0))


def _tail0(hf, hb, o, hnorm, x2, w_out, nf, wg, wu, wd, npl, pg, p2, pproj, nnext, tm, batch):
    T = x2.shape[0]
    S = T // batch
    row = lambda n: pl.BlockSpec((tm, n), lambda i: (i, 0))
    return pl.pallas_call(
        _tail0_kernel,
        grid=(T // tm,),
        in_specs=[row(D_MODEL), row(D_MODEL), row(D_MODEL), _const_spec((1, D_MODEL)), row(D_MODEL)]
        + _tail_weight_specs() + [row(PLE_DIM), _const_spec((PLE_DIM, D_MODEL)), _const_spec((1, D_MODEL))],
        out_specs=[row(D_MODEL), row(D_MODEL), _residue_spec(4, tm, D_MODEL, S),
                   _residue_spec(16, tm, D_MODEL, S)],
        out_shape=[jax.ShapeDtypeStruct((T, D_MODEL), F32), jax.ShapeDtypeStruct((T, D_MODEL), BF16),
                   jax.ShapeDtypeStruct((batch, 4, S // 4, D_MODEL), BF16),
                   jax.ShapeDtypeStruct((batch, 16, S // 16, D_MODEL), BF16)],
        scratch_shapes=[pltpu.VMEM((N_SLABS, tm, LANES), F32)] * 2,
        compiler_params=_params(("parallel",)),
        name="tail0",
    )(hf, hb, o, hnorm, x2, w_out, nf, wg, wu, wd, npl, pg, p2, pproj, nnext)


def _tail1(o0, o1, o2, l0, l1, l2, x2, w_out, nf, wg, wu, wd, npl, pg, p2, pproj, nfin, tm, batch):
    T = x2.shape[0]
    S = T // batch
    row = lambda n: pl.BlockSpec((tm, n), lambda i: (i, 0))
    return pl.pallas_call(
        _tail1_kernel,
        grid=(T // tm,),
        in_specs=[row(D_MODEL), _residue_spec(4, tm, D_MODEL, S), _residue_spec(16, tm, D_MODEL, S),
                  row(LANES), _residue_spec(4, tm, LANES, S), _residue_spec(16, tm, LANES, S), row(D_MODEL)]
        + _tail_weight_specs() + [row(PLE_DIM), _const_spec((PLE_DIM, D_MODEL)), _const_spec((1, D_MODEL))],
        out_specs=row(D_MODEL),
        out_shape=jax.ShapeDtypeStruct((T, D_MODEL), F32),
        scratch_shapes=[pltpu.VMEM((N_SLABS, tm, LANES), F32),
                        pltpu.VMEM((N_SLABS, tm, LANES), F32), pltpu.VMEM((tm, LANES), F32),
                        pltpu.VMEM((tm, LANES), F32)],
        compiler_params=_params(("parallel",)),
        name="tail1",
    )(o0, o1, o2, l0, l1, l2, x2, w_out, nf, wg, wu, wd, npl, pg, p2, pproj, nfin)


def _rope_tab_kernel(pos_ref, invf_ref, cos_ref, sin_ref, cos4_ref, sin4_ref, cos16_ref, sin16_ref):
    ang = pos_ref[...].astype(F32) * invf_ref[...]
    lane = lax.broadcasted_iota(jnp.int32, ang.shape, 1)
    s = jnp.sin(ang)
    cos_ref[...] = jnp.cos(ang)
    sin_ref[...] = jnp.where(lane < LANES // 2, -s, s)
    tm = ang.shape[0]
    for r4 in range(4):
        rows = pl.ds(r4, tm // 4, stride=4)
        cos4_ref[r4] = cos_ref[rows, :]
        sin4_ref[r4] = sin_ref[rows, :]
    for r16 in range(16):
        rows = pl.ds(r16 // 4, tm // 16, stride=4)
        cos16_ref[r16] = cos4_ref[r16 % 4, rows, :]
        sin16_ref[r16] = sin4_ref[r16 % 4, rows, :]


def _rope_tables(pos2, invf_lane, tm, batch):
    T = pos2.shape[0]
    S = T // batch
    row = pl.BlockSpec((tm, LANES), lambda i: (i, 0))
    res = lambda dil: jax.ShapeDtypeStruct((batch, dil, S // dil, LANES), F32)
    return pl.pallas_call(
        _rope_tab_kernel,
        grid=(T // tm,),
        in_specs=[pl.BlockSpec((tm, 1), lambda i: (i, 0)), _const_spec((1, LANES))],
        out_specs=[row, row] + [_residue_spec(4, tm, LANES, S)] * 2 + [_residue_spec(16, tm, LANES, S)] * 2,
        out_shape=[jax.ShapeDtypeStruct((T, LANES), F32)] * 2 + [res(4)] * 2 + [res(16)] * 2,
        compiler_params=_params(("parallel",)),
        name="rope_tab",
    )(pos2, invf_lane)


MXU_COLS = 256
LOG2E = 1.4426950408889634
LN2 = 0.6931471805599453


def _in_proj1_kernel(hn_ref, w_ref, cos_ref, sin_ref, out_ref):
    j = pl.program_id(0)
    scale = jnp.where(j == 0, (A_DH ** -0.5) * LOG2E, 1.0).astype(F32)
    c = jnp.where(j == 2, 1.0, cos_ref[...] * scale)
    s = jnp.where(j == 2, 0.0, sin_ref[...] * scale)
    hn = hn_ref[...]
    for nb in range(D_MODEL // MXU_COLS):
        acc = _dot(hn, w_ref[:, nb * MXU_COLS:(nb + 1) * MXU_COLS])
        for half in range(MXU_COLS // A_DH):
            seg = acc[:, half * A_DH:(half + 1) * A_DH]
            col = nb * MXU_COLS + half * A_DH
            out_ref[:, col:col + A_DH] = (seg * c + pltpu.roll(seg, A_DH // 2, 1) * s).astype(BF16)


def _in_proj1(hn, w, g, cos_t, sin_t, tm):
    T = hn.shape[0]
    return pl.pallas_call(
        _in_proj1_kernel,
        grid=(3, T // tm),
        in_specs=[pl.BlockSpec((tm, D_MODEL), lambda c, i: (i, 0)),
                  pl.BlockSpec((D_MODEL, D_MODEL), lambda c, i: (0, 3 * g + c)),
                  pl.BlockSpec((tm, LANES), lambda c, i: (i, 0)),
                  pl.BlockSpec((tm, LANES), lambda c, i: (i, 0))],
        out_specs=pl.BlockSpec((tm, D_MODEL), lambda c, i: (i, c)),
        out_shape=jax.ShapeDtypeStruct((T, 3 * D_MODEL), BF16),
        compiler_params=_params(("parallel", "parallel")),
        name=f"in_proj1_g{g}",
    )(hn, w, cos_t, sin_t)


ATT_SUB = 128


def _attn_kernel(q_ref, kp_ref, kc_ref, kn_ref, vp_ref, vc_ref, vn_ref, o_ref, lse_ref, kall, vall,
                 *, blkq, n_keys):
    i = pl.program_id(2)
    kall[0:RADIUS, :] = kp_ref[...]
    kall[RADIUS:RADIUS + blkq, :] = kc_ref[...]
    kall[RADIUS + blkq:, :] = kn_ref[...]
    vall[0:RADIUS, :] = vp_ref[...]
    vall[RADIUS:RADIUS + blkq, :] = vc_ref[...]
    vall[RADIUS + blkq:, :] = vn_ref[...]
    nk = ATT_SUB + 2 * RADIUS
    r = lax.broadcasted_iota(jnp.int32, (ATT_SUB, nk), 0)
    c = lax.broadcasted_iota(jnp.int32, (ATT_SUB, nk), 1)
    band = jnp.abs(c - RADIUS - r) <= RADIUS
    lane = lax.broadcasted_iota(jnp.int32, (ATT_SUB, LANES), 1)
    ones = jnp.ones((nk, A_DH), BF16)
    for a in range(blkq // ATT_SUB):
        key0 = i * blkq + a * ATT_SUB - RADIUS
        valid = band & (c + key0 >= 0) & (c + key0 < n_keys)
        rows = slice(a * ATT_SUB, (a + 1) * ATT_SUB)
        krows = slice(a * ATT_SUB, a * ATT_SUB + nk)
        lse_tile = jnp.zeros((ATT_SUB, LANES), F32)
        for h in range(A_HEADS):
            sl = slice(h * A_DH, (h + 1) * A_DH)
            s = jnp.where(valid, _dot_nt(q_ref[rows, sl], kall[krows, sl]), NEG_INF)
            m = jnp.max(s, axis=1, keepdims=True)
            p = jnp.exp2(s - m).astype(BF16)
            pv = _dot(p, jnp.concatenate([vall[krows, sl], ones], axis=1))
            den = pv[:, A_DH:]
            o_ref[rows, sl] = (pv[:, :A_DH] / den).astype(BF16)
            lse_tile = jnp.where(lane == h, (m + jnp.log2(den)) * LN2, lse_tile)
        lse_ref[rows, :] = lse_tile


def _attention(proj, blkq):
    B, dil, U, _ = proj.shape
    blkq = min(blkq, U)
    hb = blkq // RADIUS
    nhalo = U // RADIUS

    def cur(j):
        return pl.BlockSpec((None, None, blkq, D_MODEL), lambda b, r, i: (b, r, i, j))

    def prev(j):
        return pl.BlockSpec((None, None, RADIUS, D_MODEL),
                            lambda b, r, i: (b, r, jnp.maximum(i * hb - 1, 0), j))

    def nxt(j):
        return pl.BlockSpec((None, None, RADIUS, D_MODEL),
                            lambda b, r, i: (b, r, jnp.minimum((i + 1) * hb, nhalo - 1), j))

    return pl.pallas_call(
        functools.partial(_attn_kernel, blkq=blkq, n_keys=U),
        grid=(B, dil, U // blkq),
        in_specs=[cur(0), prev(1), cur(1), nxt(1), prev(2), cur(2), nxt(2)],
        out_specs=[pl.BlockSpec((None, None, blkq, D_MODEL), lambda b, r, i: (b, r, i, 0)),
                   pl.BlockSpec((None, None, blkq, LANES), lambda b, r, i: (b, r, i, 0))],
        out_shape=[jax.ShapeDtypeStruct((B, dil, U, D_MODEL), BF16),
                   jax.ShapeDtypeStruct((B, dil, U, LANES), F32)],
        scratch_shapes=[pltpu.VMEM((blkq + 2 * RADIUS, D_MODEL), BF16)] * 2,
        compiler_params=_params(("parallel", "parallel", "parallel")),
        name=f"attn_d{dil}",
    )(proj, proj, proj, proj, proj, proj, proj)


def kernel(x, p, positions, norm_mix, a_w_in, a_gate_bias, a_head_norm, a_w_out, b_w_in, b_w_out,
           norm_ffn, w_gate, w_up, w_down, norm_ple, ple_gate, ple_proj, final_norm):
    B, S, _ = x.shape
    T = B * S
    bf = lambda w: w.astype(BF16)
    vec = lambda w: w.reshape(1, -1).astype(F32)
    x2 = x.reshape(T, D_MODEL)
    nc = S // M_CHUNK

    w_in = bf(a_w_in[0])
    w_main = jnp.concatenate([w_in[:, :M_QK_ALL], w_in[:, 2 * M_QK_ALL:2 * M_QK_ALL + 2 * D_MODEL]], axis=1)
    q, ktd, v, o, feat = _in_proj0(x2, vec(norm_mix[0]), w_main, w_in[:, M_QK_ALL:2 * M_QK_ALL].T,
                                   w_in[:, 2 * M_QK_ALL + 2 * D_MODEL:], vec(a_gate_bias[0]), tm=1024)
    by_chunk = feat.reshape(N_FEAT, B, nc, M_CHUNK, N_GATE)
    pair_rows = lambda z: jnp.swapaxes(z, 2, 3).reshape(B, nc, N_GATE // 2, LANES)
    per_chunk = lambda z: jnp.broadcast_to(z[:, :, 0, :, None], (B, nc, N_GATE, LANES))
    rfeat = jnp.concatenate([pair_rows(by_chunk[4]), pair_rows(by_chunk[5]),
                             per_chunk(by_chunk[2]), per_chunk(by_chunk[3])], axis=2)
    hf, hb = _mlstm(q.reshape(B, S, -1), ktd.reshape(B, nc, M_QK_ALL, LANES), v.reshape(B, S, -1),
                    feat.reshape(N_FEAT, B, S, N_GATE), rfeat, tb=512)
    h, *hns = _tail0(hf.reshape(T, -1), hb.reshape(T, -1), o, vec(a_head_norm[0]), x2, bf(a_w_out[0]),
                     vec(norm_ffn[0]), bf(w_gate[0]), bf(w_up[0]), bf(w_down[0]), vec(norm_ple[0]),
                     bf(ple_gate[0]), p[0].reshape(T, PLE_DIM), bf(ple_proj[0]), vec(norm_mix[1]),
                     tm=512, batch=B)

    inv_freq = ROPE_THETA ** (-jnp.arange(0, ROPE_DIM, 2, dtype=F32) / ROPE_DIM)
    invf_lane = (jnp.zeros((1, LANES), F32).at[0, :ROPE_HALF].set(inv_freq)
                 .at[0, LANES // 2:LANES // 2 + ROPE_HALF].set(inv_freq))
    tabs = _rope_tables(positions.reshape(T, 1), invf_lane, tm=1024, batch=B)
    dim_order = jnp.concatenate([jnp.arange(0, ROPE_HALF), jnp.arange(ROPE_DIM, ROPE_DIM + 48),
                                 jnp.arange(ROPE_HALF, ROPE_DIM), jnp.arange(ROPE_DIM + 48, A_DH)])
    w1 = bf(b_w_in[0]).reshape(D_MODEL, N_GROUPS, 3, A_HEADS, A_DH)
    w1 = jnp.concatenate([w1[:, :, :2][..., dim_order], w1[:, :, 2:]], axis=2).reshape(D_MODEL, -1)
    o_g, l_g = [], []
    for g, (_, dil) in enumerate(DILATED_GROUPS):
        proj = _in_proj1(hns[g].reshape(T, D_MODEL), w1, g, tabs[2 * g].reshape(T, LANES),
                         tabs[2 * g + 1].reshape(T, LANES), tm=2048)
        og, lg = _attention(proj.reshape(B, dil, S // dil, 3 * D_MODEL), blkq=1024)
        o_g.append(og)
        l_g.append(lg)
    out = _tail1(o_g[0].reshape(T, D_MODEL), o_g[1], o_g[2], l_g[0].reshape(T, LANES), l_g[1], l_g[2], h,
                 bf(b_w_out[0]), vec(norm_ffn[1]), bf(w_gate[1]), bf(w_up[1]), bf(w_down[1]),
                 vec(norm_ple[1]), bf(ple_gate[1]), p[1].reshape(T, PLE_DIM), bf(ple_proj[1]),
                 vec(final_norm), tm=512, batch=B)
    return out.reshape(B, S, D_MODEL)
```

```python
import functools

import jax
import jax.numpy as jnp
from jax import lax
from jax.experimental import pallas as pl
from jax.experimental.pallas import tpu as pltpu

F32 = jnp.float32
BF16 = jnp.bfloat16

D_MODEL = 1024
LANES = 128
EPS = 1e-6

M_HEADS = 8
M_QK = 64
M_V = 128
M_CHUNK = 64
M_QK_ALL = M_HEADS * M_QK

A_HEADS = 8
A_DH = 128
RADIUS = 64
DILATED_GROUPS = ((128, 1), (512, 4), (2048, 16))
N_GROUPS = 3
ROPE_DIM = 32
ROPE_HALF = 16
ROPE_THETA = 500000.0
NEG_INF = -1e30

FFN_HIDDEN = 2816
PLE_DIM = 256

VMEM_LIMIT = 60 * 1024 * 1024


def _dot(a, b):
    return jnp.dot(a, b, preferred_element_type=F32)


def _dot_nt(a, b):
    return lax.dot_general(a, b, (((1,), (1,)), ((), ())), preferred_element_type=F32)


def _rms(x, w):
    ms = jnp.mean(x * x, axis=-1, keepdims=True)
    return x * lax.rsqrt(ms + EPS) * w


def _const_spec(shape):
    nd = len(shape)
    return pl.BlockSpec(shape, lambda *_: (0,) * nd, pipeline_mode=pl.Buffered(1))


def _params(sem):
    return pltpu.CompilerParams(dimension_semantics=sem, vmem_limit_bytes=VMEM_LIMIT)


N_DIRS = 2
N_GATE = N_DIRS * M_HEADS
N_FEAT = 6


def _chunk_scan(x, op, fill, reverse):
    n = x.shape[0]
    pos = lax.broadcasted_iota(jnp.int32, x.shape, 0) % M_CHUNK
    shift = 1
    while shift < M_CHUNK:
        if reverse:
            shifted, ok = pltpu.roll(x, n - shift, 0), pos < M_CHUNK - shift
        else:
            shifted, ok = pltpu.roll(x, shift, 0), pos >= shift
        x = op(x, jnp.where(ok, shifted, fill))
        shift *= 2
    return x


def _in_proj0_kernel(x_ref, nw_ref, w_ref, wkt_ref, wg_ref, gb_ref, q_ref, ktd_ref, v_ref, o_ref, f_ref):
    tm = x_ref.shape[0]
    hn = _rms(x_ref[...], nw_ref[...]).astype(BF16)
    gates = _dot(hn, wg_ref[...]) + gb_ref[...]
    ig = gates[:, :N_GATE]
    fg = gates[:, N_GATE:]
    lf = jnp.minimum(fg, 0.0) - jnp.log1p(jnp.exp(-jnp.abs(fg)))
    is_fwd = lax.broadcasted_iota(jnp.int32, lf.shape, 1) < M_HEADS
    q_ref[...] = (_dot(hn, w_ref[:, 0:M_QK_ALL]) * (M_QK ** -0.5)).astype(BF16)
    pre = _chunk_scan(lf, jnp.add, 0.0, False)
    suf = _chunk_scan(lf, jnp.add, 0.0, True)
    b = jnp.where(is_fwd, pre, suf)
    g = pre + suf - lf
    w = ig - b
    v_ref[...] = _dot(hn, w_ref[:, M_QK_ALL:M_QK_ALL + D_MODEL]).astype(BF16)
    wpre = _chunk_scan(w, jnp.maximum, -jnp.inf, False)
    wsuf = _chunk_scan(w, jnp.maximum, -jnp.inf, True)
    for k, feat in enumerate((b, jnp.where(is_fwd, wpre, wsuf), g, jnp.maximum(wpre, wsuf), w, g + w)):
        f_ref[k] = feat
    o_ref[...] = _dot(hn, w_ref[:, M_QK_ALL + D_MODEL:]).astype(BF16)
    kt = _dot_nt(wkt_ref[...], hn)
    low = lax.broadcasted_iota(jnp.int32, (M_QK_ALL, LANES), 1) < HALF
    for c in range(0, tm // M_CHUNK, 2):
        both = kt[:, c * M_CHUNK:(c + 2) * M_CHUNK]
        swapped = pltpu.roll(both, HALF, 1)
        ktd_ref[c] = jnp.where(low, both, swapped).astype(BF16)
        ktd_ref[c + 1] = jnp.where(low, swapped, both).astype(BF16)


def _in_proj0(x2, nw, w_main, w_kt, w_gate, gate_bias, tm):
    T = x2.shape[0]
    row = lambda n: pl.BlockSpec((tm, n), lambda i: (i, 0))
    nch = tm // M_CHUNK
    return pl.pallas_call(
        _in_proj0_kernel,
        grid=(T // tm,),
        in_specs=[row(D_MODEL), _const_spec((1, D_MODEL)), _const_spec(w_main.shape), _const_spec(w_kt.shape),
                  _const_spec(w_gate.shape), _const_spec((1, 2 * N_GATE))],
        out_specs=[row(M_QK_ALL), pl.BlockSpec((nch, M_QK_ALL, LANES), lambda i: (i, 0, 0)),
                   row(D_MODEL), row(D_MODEL), pl.BlockSpec((N_FEAT, tm, N_GATE), lambda i: (0, i, 0))],
        out_shape=[jax.ShapeDtypeStruct((T, M_QK_ALL), BF16),
                   jax.ShapeDtypeStruct((T // M_CHUNK, M_QK_ALL, LANES), BF16),
                   jax.ShapeDtypeStruct((T, D_MODEL), BF16), jax.ShapeDtypeStruct((T, D_MODEL), BF16),
                   jax.ShapeDtypeStruct((N_FEAT, T, N_GATE), F32)],
        compiler_params=_params(("parallel",)),
        name="in_proj0",
    )(x2, nw, w_main, w_kt, w_gate, gate_bias)


N_PAIRS = M_HEADS // 2
HALF = LANES // 2
PAIR_W = 2 * 2 * M_V
ROW_W, ROW_A, ROW_G, ROW_WMAX, N_ROWF = 0, N_GATE // 2, N_GATE, 2 * N_GATE, 3 * N_GATE


def _lane_pair(x, col):
    lane = lax.broadcasted_iota(jnp.int32, (x.shape[0], LANES), 1)
    return jnp.where(lane < HALF, x[:, col:col + 1], x[:, col + 1:col + 2])


def _mlstm_chunk(q_ref, kt_ref, v_ref, cf_ref, rf_ref, h_ref, s_ref, sb_ref, mc_ref, mr_ref, sub, fwd):
    L = M_CHUNK
    r0 = pl.multiple_of(sub * L, L)
    rows = pl.ds(r0, L)
    lo = 0 if fwd else M_HEADS
    b_c = cf_ref[0, rows, :]
    m_r = mr_ref[...]
    mj = b_c + jnp.maximum(m_r, cf_ref[1, rows, :])
    u = b_c - mj
    iw = jnp.exp(b_c + m_r - mj)
    en = jnp.exp(-mj)
    mr_ref[...] = cf_ref[2, pl.ds(r0, 1), :] + jnp.maximum(m_r, cf_ref[3, pl.ds(r0, 1), :])
    rf = rf_ref[sub]
    m_c = mc_ref[...]
    g = rf[ROW_G + lo:ROW_G + lo + M_HEADS]
    m_new = g + jnp.maximum(m_c, rf[ROW_WMAX + lo:ROW_WMAX + lo + M_HEADS])
    decay = jnp.exp(g + m_c - m_new)
    mc_ref[...] = m_new

    low_half = lax.broadcasted_iota(jnp.int32, (1, LANES), 1) < HALF
    li = lax.broadcasted_iota(jnp.int32, (L, LANES), 0)
    si = lax.broadcasted_iota(jnp.int32, (L, LANES), 1) % HALF
    mask = (si <= li) if fwd else (si >= li)
    same_head = (lax.broadcasted_iota(jnp.int32, (LANES, LANES), 0) // HALF
                 == lax.broadcasted_iota(jnp.int32, (LANES, LANES), 1) // HALF)
    ones = jnp.ones((L, M_V), BF16)
    zeros = jnp.zeros((L, M_V), BF16)
    yield
    q2s, kbds, scores = [], [], []
    for p in range(N_PAIRS):
        q2s.append(q_ref[rows, p * LANES:(p + 1) * LANES])
        kbds.append(jnp.where(same_head, kt_ref[sub, p * LANES:(p + 1) * LANES, :], jnp.zeros((), BF16)))
        scores.append(_dot(q2s[p], kbds[p]))
    yield
    mains, inters = [], []
    for p in range(N_PAIRS):
        pr = lo // 2 + p
        w_pr = rf[ROW_W + pr:ROW_W + pr + 1]
        m_new_pr = jnp.where(low_half, m_new[2 * p:2 * p + 1], m_new[2 * p + 1:2 * p + 2])
        k_scale = jnp.exp(rf[ROW_A + pr:ROW_A + pr + 1] - m_new_pr)
        dw = jnp.exp(jnp.where(mask, _lane_pair(u, lo + 2 * p) + w_pr, -jnp.inf))
        sm = (scores[p] * dw).astype(BF16)
        kw = (kbds[p].astype(F32) * k_scale).astype(BF16)
        qi = (q2s[p].astype(F32) * _lane_pair(iw, lo + 2 * p)).astype(BF16)
        pad = jnp.concatenate([zeros, zeros], axis=1)
        pair = []
        for j in range(2):
            h = 2 * p + j
            v_ext = jnp.concatenate([v_ref[rows, h * M_V:(h + 1) * M_V], ones], axis=1)
            rhs = jnp.concatenate([v_ext, pad] if j == 0 else [pad, v_ext], axis=0)
            pair.append(_dot(jnp.concatenate([sm, kw[j * M_QK:(j + 1) * M_QK]], axis=0), rhs))
        mains.append(pair)
        inters.append(_dot(qi, sb_ref[p]))
    yield
    outs = []
    for p in range(N_PAIRS):
        for j in range(2):
            h = 2 * p + j
            out = mains[p][j][:L] + inters[p][:, j * 2 * M_V:(j + 1) * 2 * M_V]
            outs.append(out[:, :M_V] / jnp.maximum(jnp.abs(out[:, M_V:]), en[:, lo + h:lo + h + 1]))
            blk = (p, slice(j * M_QK, (j + 1) * M_QK), slice(j * 2 * M_V, (j + 1) * 2 * M_V))
            dec = jnp.concatenate([decay[h:h + 1]] * 2, axis=1)
            s_new = dec * s_ref[blk] + mains[p][j][L:]
            s_ref[blk] = s_new
            sb_ref[blk] = s_new.astype(BF16)
    h_ref[rows, :] = jnp.concatenate(outs, axis=1).astype(h_ref.dtype)
    yield


N_MLSTM_STAGES = 4


def _mlstm_kernel(qf, ktf, vf, cff, rff, qb, ktb, vb, cfb, rfb, hf_ref, hb_ref,
                  sf, sbf, mcf, mrf, sb, sbb, mcb, mrb, *, nsub):
    @pl.when(pl.program_id(1) == 0)
    def _():
        for ref in (sf, sbf, mcf, mrf, sb, sbb, mcb, mrb):
            ref[...] = jnp.zeros_like(ref)

    def body(j, carry):
        chunks = [_mlstm_chunk(qf, ktf, vf, cff, rff, hf_ref, sf, sbf, mcf, mrf, j, True),
                  _mlstm_chunk(qb, ktb, vb, cfb, rfb, hb_ref, sb, sbb, mcb, mrb, nsub - 1 - j, False)]
        for _ in range(N_MLSTM_STAGES):
            for chunk in chunks:
                next(chunk)
        return carry

    lax.fori_loop(0, nsub, body, 0)


def _mlstm(q, ktd, v, cfeat, rfeat, tb):
    B, S, _ = q.shape
    n = S // tb
    nsub = tb // M_CHUNK
    fw = lambda b, c: (b, c, 0)
    bw = lambda b, c: (b, n - 1 - c, 0)
    fw4 = lambda b, c: (b, c, 0, 0)
    bw4 = lambda b, c: (b, n - 1 - c, 0, 0)

    def specs(im, im4):
        return [pl.BlockSpec((None, tb, M_QK_ALL), im), pl.BlockSpec((None, nsub, M_QK_ALL, LANES), im4),
                pl.BlockSpec((None, tb, D_MODEL), im),
                pl.BlockSpec((4, None, tb, N_GATE), lambda b, c: (0,) + im(b, c)),
                pl.BlockSpec((None, nsub, N_ROWF, LANES), im4)]

    state = [pltpu.VMEM((N_PAIRS, LANES, PAIR_W), F32), pltpu.VMEM((N_PAIRS, LANES, PAIR_W), BF16),
             pltpu.VMEM((M_HEADS, LANES), F32), pltpu.VMEM((1, N_GATE), F32)]
    return pl.pallas_call(
        functools.partial(_mlstm_kernel, nsub=nsub),
        grid=(B, n),
        in_specs=specs(fw, fw4) + specs(bw, bw4),
        out_specs=[pl.BlockSpec((None, tb, D_MODEL), fw), pl.BlockSpec((None, tb, D_MODEL), bw)],
        out_shape=[jax.ShapeDtypeStruct((B, S, D_MODEL), BF16)] * 2,
        scratch_shapes=state + state,
        compiler_params=_params(("parallel", "arbitrary")),
        name="mlstm",
    )(q, ktd, v, cfeat, rfeat, q, ktd, v, cfeat, rfeat)


def _sigmoid(x):
    return 0.5 * jnp.tanh(0.5 * x) + 0.5


def _tail_common(mix_pair, x_ref, w_out_ref, nf_ref, wg_ref, wu_ref, wd_ref, npl_ref, pg_ref, p_ref, pp_ref):
    emb = _dot(p_ref[...].astype(BF16), pp_ref[...])
    h1 = x_ref[...]
    for i in range(D_MODEL // MXU_COLS):
        h1 = h1 + _dot(mix_pair(i), w_out_ref[i * MXU_COLS:(i + 1) * MXU_COLS, :])
    hn = _rms(h1, nf_ref[...]).astype(BF16)
    gate = _dot(hn, wg_ref[...])
    act = (gate * _sigmoid(gate) * _dot(hn, wu_ref[...])).astype(BF16)
    h2 = h1 + _dot(act, wd_ref[...])
    pgate = _sigmoid(_dot(_rms(h2, npl_ref[...]).astype(BF16), pg_ref[...]))
    return h2 + pgate * emb


N_SLABS = D_MODEL // LANES


def _tail0_kernel(hf_ref, hb_ref, o_ref, hnorm_ref, x_ref, w_out_ref, nf_ref, wg_ref, wu_ref, wd_ref,
                  npl_ref, pg_ref, p_ref, pp_ref, nnext_ref, out_ref, hn_ref, hn4_ref, hn16_ref, slab_ref,
                  slab4_ref):
    def head(h):
        sl = slice(h * M_V, (h + 1) * M_V)
        hh = hf_ref[:, sl].astype(F32) + hb_ref[:, sl].astype(F32)
        hh = hh * lax.rsqrt(jnp.mean(hh * hh, axis=-1, keepdims=True) + EPS) * hnorm_ref[:, sl]
        return (_sigmoid(o_ref[:, sl].astype(F32)) * hh).astype(BF16)

    h3 = _tail_common(lambda i: jnp.concatenate([head(2 * i), head(2 * i + 1)], axis=1), x_ref, w_out_ref,
                      nf_ref, wg_ref, wu_ref, wd_ref, npl_ref, pg_ref, p_ref, pp_ref)
    out_ref[...] = h3
    hn = _rms(h3, nnext_ref[...])
    hn_ref[...] = hn.astype(BF16)
    tm = hn.shape[0]
    q4 = tm // 4
    for s in range(N_SLABS):
        slab_ref[s] = hn[:, s * LANES:(s + 1) * LANES]
    for r4 in range(4):
        parts = [slab_ref[s, pl.ds(r4, q4, stride=4), :] for s in range(N_SLABS)]
        hn4_ref[r4] = jnp.concatenate(parts, axis=1).astype(BF16)
        for s in range(N_SLABS):
            slab4_ref[s, r4 * q4:(r4 + 1) * q4, :] = parts[s]
    for r16 in range(16):
        start = (r16 % 4) * q4 + r16 // 4
        hn16_ref[r16] = jnp.concatenate([slab4_ref[s, pl.ds(start, tm // 16, stride=4), :]
                                         for s in range(N_SLABS)], axis=1).astype(BF16)


def _tail1_kernel(o0_ref, o1_ref, o2_ref, l0_ref, l1_ref, l2_ref, x_ref, w_out_ref, nf_ref, wg_ref, wu_ref,
                  wd_ref, npl_ref, pg_ref, p_ref, pp_ref, nfin_ref, out_ref, slab1_ref, slab2_ref,
                  ls1_ref, ls2_ref):
    tm = o0_ref.shape[0]
    for dil, o_ref, l_ref, slab_ref, ls_ref in ((4, o1_ref, l1_ref, slab1_ref, ls1_ref),
                                                (16, o2_ref, l2_ref, slab2_ref, ls2_ref)):
        for r in range(dil):
            rows = pl.ds(r, tm // dil, stride=dil)
            ls_ref[rows, :] = l_ref[r]
            o_r = o_ref[r].astype(F32)
            for s in range(N_SLABS):
                slab_ref[s, rows, :] = o_r[:, s * LANES:(s + 1) * LANES]
    l0 = l0_ref[...]
    l1 = ls1_ref[...]
    l2 = ls2_ref[...]
    mx = jnp.maximum(jnp.maximum(l0, l1), l2)
    e0 = jnp.exp(l0 - mx)
    e1 = jnp.exp(l1 - mx)
    e2 = jnp.exp(l2 - mx)
    inv = 1.0 / (e0 + e1 + e2)
    w0 = e0 * inv
    w1 = e1 * inv
    w2 = e2 * inv
    def head(h):
        sl = slice(h * A_DH, (h + 1) * A_DH)
        return (w0[:, h:h + 1] * o0_ref[:, sl].astype(F32) + w1[:, h:h + 1] * slab1_ref[h]
                + w2[:, h:h + 1] * slab2_ref[h]).astype(BF16)

    h3 = _tail_common(lambda i: jnp.concatenate([head(2 * i), head(2 * i + 1)], axis=1), x_ref, w_out_ref,
                      nf_ref, wg_ref, wu_ref, wd_ref, npl_ref, pg_ref, p_ref, pp_ref)
    out_ref[...] = _rms(h3, nfin_ref[...])


def _tail_weight_specs():
    return [_const_spec((D_MODEL, D_MODEL)), _const_spec((1, D_MODEL)),
            _const_spec((D_MODEL, FFN_HIDDEN)), _const_spec((D_MODEL, FFN_HIDDEN)),
            _const_spec((FFN_HIDDEN, D_MODEL)), _const_spec((1, D_MODEL)), _const_spec((D_MODEL, D_MODEL))]


def _residue_spec(dil, tm, n, seq):
    nb = seq // tm
    return pl.BlockSpec((None, dil, tm // dil, n), lambda i: (i // nb, 0, i % nb, 0))


def _tail0(hf, hb, o, hnorm, x2, w_out, nf, wg, wu, wd, npl, pg, p2, pproj, nnext, tm, batch):
    T = x2.shape[0]
    S = T // batch
    row = lambda n: pl.BlockSpec((tm, n), lambda i: (i, 0))
    return pl.pallas_call(
        _tail0_kernel,
        grid=(T // tm,),
        in_specs=[row(D_MODEL), row(D_MODEL), row(D_MODEL), _const_spec((1, D_MODEL)), row(D_MODEL)]
        + _tail_weight_specs() + [row(PLE_DIM), _const_spec((PLE_DIM, D_MODEL)), _const_spec((1, D_MODEL))],
        out_specs=[row(D_MODEL), row(D_MODEL), _residue_spec(4, tm, D_MODEL, S),
                   _residue_spec(16, tm, D_MODEL, S)],
        out_shape=[jax.ShapeDtypeStruct((T, D_MODEL), F32), jax.ShapeDtypeStruct((T, D_MODEL), BF16),
                   jax.ShapeDtypeStruct((batch, 4, S // 4, D_MODEL), BF16),
                   jax.ShapeDtypeStruct((batch, 16, S // 16, D_MODEL), BF16)],
        scratch_shapes=[pltpu.VMEM((N_SLABS, tm, LANES), F32)] * 2,
        compiler_params=_params(("parallel",)),
        name="tail0",
    )(hf, hb, o, hnorm, x2, w_out, nf, wg, wu, wd, npl, pg, p2, pproj, nnext)


def _tail1(o0, o1, o2, l0, l1, l2, x2, w_out, nf, wg, wu, wd, npl, pg, p2, pproj, nfin, tm, batch):
    T = x2.shape[0]
    S = T // batch
    row = lambda n: pl.BlockSpec((tm, n), lambda i: (i, 0))
    return pl.pallas_call(
        _tail1_kernel,
        grid=(T // tm,),
        in_specs=[row(D_MODEL), _residue_spec(4, tm, D_MODEL, S), _residue_spec(16, tm, D_MODEL, S),
                  row(LANES), _residue_spec(4, tm, LANES, S), _residue_spec(16, tm, LANES, S), row(D_MODEL)]
        + _tail_weight_specs() + [row(PLE_DIM), _const_spec((PLE_DIM, D_MODEL)), _const_spec((1, D_MODEL))],
        out_specs=row(D_MODEL),
        out_shape=jax.ShapeDtypeStruct((T, D_MODEL), F32),
        scratch_shapes=[pltpu.VMEM((N_SLABS, tm, LANES), F32),
                        pltpu.VMEM((N_SLABS, tm, LANES), F32), pltpu.VMEM((tm, LANES), F32),
                        pltpu.VMEM((tm, LANES), F32)],
        compiler_params=_params(("parallel",)),
        name="tail1",
    )(o0, o1, o2, l0, l1, l2, x2, w_out, nf, wg, wu, wd, npl, pg, p2, pproj, nfin)


def _rope_tab_kernel(pos_ref, invf_ref, cos_ref, sin_ref, cos4_ref, sin4_ref, cos16_ref, sin16_ref):
    ang = pos_ref[...].astype(F32) * invf_ref[...]
    lane = lax.broadcasted_iota(jnp.int32, ang.shape, 1)
    s = jnp.sin(ang)
    cos_ref[...] = jnp.cos(ang)
    sin_ref[...] = jnp.where(lane < LANES // 2, -s, s)
    tm = ang.shape[0]
    for r4 in range(4):
        rows = pl.ds(r4, tm // 4, stride=4)
        cos4_ref[r4] = cos_ref[rows, :]
        sin4_ref[r4] = sin_ref[rows, :]
    for r16 in range(16):
        rows = pl.ds(r16 // 4, tm // 16, stride=4)
        cos16_ref[r16] = cos4_ref[r16 % 4, rows, :]
        sin16_ref[r16] = sin4_ref[r16 % 4, rows, :]


def _rope_tables(pos2, invf_lane, tm, batch):
    T = pos2.shape[0]
    S = T // batch
    row = pl.BlockSpec((tm, LANES), lambda i: (i, 0))
    res = lambda dil: jax.ShapeDtypeStruct((batch, dil, S // dil, LANES), F32)
    return pl.pallas_call(
        _rope_tab_kernel,
        grid=(T // tm,),
        in_specs=[pl.BlockSpec((tm, 1), lambda i: (i, 0)), _const_spec((1, LANES))],
        out_specs=[row, row] + [_residue_spec(4, tm, LANES, S)] * 2 + [_residue_spec(16, tm, LANES, S)] * 2,
        out_shape=[jax.ShapeDtypeStruct((T, LANES), F32)] * 2 + [res(4)] * 2 + [res(16)] * 2,
        compiler_params=_params(("parallel",)),
        name="rope_tab",
    )(pos2, invf_lane)


MXU_COLS = 256
LOG2E = 1.4426950408889634
LN2 = 0.6931471805599453


def _in_proj1_kernel(hn_ref, w_ref, cos_ref, sin_ref, out_ref):
    hn = hn_ref[...]
    q_scale = (A_DH ** -0.5) * LOG2E
    tables = ((cos_ref[...] * q_scale, sin_ref[...] * q_scale), (cos_ref[...], sin_ref[...]))
    for nb in range(3 * D_MODEL // MXU_COLS):
        kind = nb * MXU_COLS // D_MODEL
        acc = _dot(hn, w_ref[:, nb * MXU_COLS:(nb + 1) * MXU_COLS])
        for half in range(MXU_COLS // A_DH):
            seg = acc[:, half * A_DH:(half + 1) * A_DH]
            if kind < 2:
                c, s = tables[kind]
                seg = seg * c + pltpu.roll(seg, A_DH // 2, 1) * s
            col = nb * MXU_COLS + half * A_DH
            out_ref[:, col:col + A_DH] = seg.astype(BF16)


def _in_proj1(hn, w, g, cos_t, sin_t, tm):
    T = hn.shape[0]
    row = lambda n: pl.BlockSpec((tm, n), lambda i: (i, 0))
    return pl.pallas_call(
        _in_proj1_kernel,
        grid=(T // tm,),
        in_specs=[row(D_MODEL),
                  pl.BlockSpec((D_MODEL, 3 * D_MODEL), lambda i: (0, g), pipeline_mode=pl.Buffered(1)),
                  row(LANES), row(LANES)],
        out_specs=row(3 * D_MODEL),
        out_shape=jax.ShapeDtypeStruct((T, 3 * D_MODEL), BF16),
        compiler_params=_params(("parallel",)),
        name=f"in_proj1_g{g}",
    )(hn, w, cos_t, sin_t)


ATT_SUB = 128


def _attn_kernel(q_ref, kp_ref, kc_ref, kn_ref, vp_ref, vc_ref, vn_ref, o_ref, lse_ref, kall, vall,
                 *, blkq, n_keys):
    i = pl.program_id(2)
    kall[0:RADIUS, :] = kp_ref[...]
    kall[RADIUS:RADIUS + blkq, :] = kc_ref[...]
    kall[RADIUS + blkq:, :] = kn_ref[...]
    vall[0:RADIUS, :] = vp_ref[...]
    vall[RADIUS:RADIUS + blkq, :] = vc_ref[...]
    vall[RADIUS + blkq:, :] = vn_ref[...]
    nk = ATT_SUB + 2 * RADIUS
    r = lax.broadcasted_iota(jnp.int32, (ATT_SUB, nk), 0)
    c = lax.broadcasted_iota(jnp.int32, (ATT_SUB, nk), 1)
    band = jnp.abs(c - RADIUS - r) <= RADIUS
    lane = lax.broadcasted_iota(jnp.int32, (ATT_SUB, LANES), 1)
    ones = jnp.ones((nk, A_DH), BF16)
    for a in range(blkq // ATT_SUB):
        key0 = i * blkq + a * ATT_SUB - RADIUS
        valid = band & (c + key0 >= 0) & (c + key0 < n_keys)
        rows = slice(a * ATT_SUB, (a + 1) * ATT_SUB)
        krows = slice(a * ATT_SUB, a * ATT_SUB + nk)
        lse_tile = jnp.zeros((ATT_SUB, LANES), F32)
        for h in range(A_HEADS):
            sl = slice(h * A_DH, (h + 1) * A_DH)
            s = jnp.where(valid, _dot_nt(q_ref[rows, sl], kall[krows, sl]), NEG_INF)
            m = jnp.max(s, axis=1, keepdims=True)
            p = jnp.exp2(s - m).astype(BF16)
            pv = _dot(p, jnp.concatenate([vall[krows, sl], ones], axis=1))
            den = pv[:, A_DH:]
            o_ref[rows, sl] = (pv[:, :A_DH] / den).astype(BF16)
            lse_tile = jnp.where(lane == h, (m + jnp.log2(den)) * LN2, lse_tile)
        lse_ref[rows, :] = lse_tile


def _attention(proj, blkq):
    B, dil, U, _ = proj.shape
    blkq = min(blkq, U)
    hb = blkq // RADIUS
    nhalo = U // RADIUS

    def cur(j):
        return pl.BlockSpec((None, None, blkq, D_MODEL), lambda b, r, i: (b, r, i, j))

    def prev(j):
        return pl.BlockSpec((None, None, RADIUS, D_MODEL),
                            lambda b, r, i: (b, r, jnp.maximum(i * hb - 1, 0), j))

    def nxt(j):
        return pl.BlockSpec((None, None, RADIUS, D_MODEL),
                            lambda b, r, i: (b, r, jnp.minimum((i + 1) * hb, nhalo - 1), j))

    return pl.pallas_call(
        functools.partial(_attn_kernel, blkq=blkq, n_keys=U),
        grid=(B, dil, U // blkq),
        in_specs=[cur(0), prev(1), cur(1), nxt(1), prev(2), cur(2), nxt(2)],
        out_specs=[pl.BlockSpec((None, None, blkq, D_MODEL), lambda b, r, i: (b, r, i, 0)),
                   pl.BlockSpec((None, None, blkq, LANES), lambda b, r, i: (b, r, i, 0))],
        out_shape=[jax.ShapeDtypeStruct((B, dil, U, D_MODEL), BF16),
                   jax.ShapeDtypeStruct((B, dil, U, LANES), F32)],
        scratch_shapes=[pltpu.VMEM((blkq + 2 * RADIUS, D_MODEL), BF16)] * 2,
        compiler_params=_params(("parallel", "parallel", "parallel")),
        name=f"attn_d{dil}",
    )(proj, proj, proj, proj, proj, proj, proj)


def kernel(x, p, positions, norm_mix, a_w_in, a_gate_bias, a_head_norm, a_w_out, b_w_in, b_w_out,
           norm_ffn, w_gate, w_up, w_down, norm_ple, ple_gate, ple_proj, final_norm):
    B, S, _ = x.shape
    T = B * S
    bf = lambda w: w.astype(BF16)
    vec = lambda w: w.reshape(1, -1).astype(F32)
    x2 = x.reshape(T, D_MODEL)
    nc = S // M_CHUNK

    w_in = bf(a_w_in[0])
    w_main = jnp.concatenate([w_in[:, :M_QK_ALL], w_in[:, 2 * M_QK_ALL:2 * M_QK_ALL + 2 * D_MODEL]], axis=1)
    q, ktd, v, o, feat = _in_proj0(x2, vec(norm_mix[0]), w_main, w_in[:, M_QK_ALL:2 * M_QK_ALL].T,
                                   w_in[:, 2 * M_QK_ALL + 2 * D_MODEL:], vec(a_gate_bias[0]), tm=1024)
    by_chunk = feat.reshape(N_FEAT, B, nc, M_CHUNK, N_GATE)
    pair_rows = lambda z: jnp.swapaxes(z, 2, 3).reshape(B, nc, N_GATE // 2, LANES)
    per_chunk = lambda z: jnp.broadcast_to(z[:, :, 0, :, None], (B, nc, N_GATE, LANES))
    rfeat = jnp.concatenate([pair_rows(by_chunk[4]), pair_rows(by_chunk[5]),
                             per_chunk(by_chunk[2]), per_chunk(by_chunk[3])], axis=2)
    hf, hb = _mlstm(q.reshape(B, S, -1), ktd.reshape(B, nc, M_QK_ALL, LANES), v.reshape(B, S, -1),
                    feat.reshape(N_FEAT, B, S, N_GATE), rfeat, tb=512)
    h, *hns = _tail0(hf.reshape(T, -1), hb.reshape(T, -1), o, vec(a_head_norm[0]), x2, bf(a_w_out[0]),
                     vec(norm_ffn[0]), bf(w_gate[0]), bf(w_up[0]), bf(w_down[0]), vec(norm_ple[0]),
                     bf(ple_gate[0]), p[0].reshape(T, PLE_DIM), bf(ple_proj[0]), vec(norm_mix[1]),
                     tm=512, batch=B)

    inv_freq = ROPE_THETA ** (-jnp.arange(0, ROPE_DIM, 2, dtype=F32) / ROPE_DIM)
    invf_lane = (jnp.zeros((1, LANES), F32).at[0, :ROPE_HALF].set(inv_freq)
                 .at[0, LANES // 2:LANES // 2 + ROPE_HALF].set(inv_freq))
    tabs = _rope_tables(positions.reshape(T, 1), invf_lane, tm=1024, batch=B)
    dim_order = jnp.concatenate([jnp.arange(0, ROPE_HALF), jnp.arange(ROPE_DIM, ROPE_DIM + 48),
                                 jnp.arange(ROPE_HALF, ROPE_DIM), jnp.arange(ROPE_DIM + 48, A_DH)])
    w1 = bf(b_w_in[0]).reshape(D_MODEL, N_GROUPS, 3, A_HEADS, A_DH)
    w1 = jnp.concatenate([w1[:, :, :2][..., dim_order], w1[:, :, 2:]], axis=2).reshape(D_MODEL, -1)
    o_g, l_g = [], []
    for g, (_, dil) in enumerate(DILATED_GROUPS):
        proj = _in_proj1(hns[g].reshape(T, D_MODEL), w1, g, tabs[2 * g].reshape(T, LANES),
                         tabs[2 * g + 1].reshape(T, LANES), tm=2048)
        og, lg = _attention(proj.reshape(B, dil, S // dil, 3 * D_MODEL), blkq=1024)
        o_g.append(og)
        l_g.append(lg)
    out = _tail1(o_g[0].reshape(T, D_MODEL), o_g[1], o_g[2], l_g[0].reshape(T, LANES), l_g[1], l_g[2], h,
                 bf(b_w_out[0]), vec(norm_ffn[1]), bf(w_gate[1]), bf(w_up[1]), bf(w_down[1]),
                 vec(norm_ple[1]), bf(ple_gate[1]), p[1].reshape(T, PLE_DIM), bf(ple_proj[1]),
                 vec(final_norm), tm=512, batch=B)
    return out.reshape(B, S, D_MODEL)
```

```python
import functools

import jax
import jax.numpy as jnp
from jax import lax
from jax.experimental import pallas as pl
from jax.experimental.pallas import tpu as pltpu

F32 = jnp.float32
BF16 = jnp.bfloat16

D_MODEL = 1024
LANES = 128
EPS = 1e-6

M_HEADS = 8
M_QK = 64
M_V = 128
M_CHUNK = 64
M_QK_ALL = M_HEADS * M_QK

A_HEADS = 8
A_DH = 128
RADIUS = 64
DILATED_GROUPS = ((128, 1), (512, 4), (2048, 16))
N_GROUPS = 3
ROPE_DIM = 32
ROPE_HALF = 16
ROPE_THETA = 500000.0
NEG_INF = -1e30

FFN_HIDDEN = 2816
PLE_DIM = 256

VMEM_LIMIT = 60 * 1024 * 1024


def _dot(a, b):
    return jnp.dot(a, b, preferred_element_type=F32)


def _dot_nt(a, b):
    return lax.dot_general(a, b, (((1,), (1,)), ((), ())), preferred_element_type=F32)


def _rms(x, w):
    ms = jnp.mean(x * x, axis=-1, keepdims=True)
    return x * lax.rsqrt(ms + EPS) * w


def _const_spec(shape):
    nd = len(shape)
    return pl.BlockSpec(shape, lambda *_: (0,) * nd, pipeline_mode=pl.Buffered(1))


def _params(sem):
    return pltpu.CompilerParams(dimension_semantics=sem, vmem_limit_bytes=VMEM_LIMIT)


N_DIRS = 2
N_GATE = N_DIRS * M_HEADS
N_FEAT = 6


def _chunk_scan(x, op, fill, reverse):
    n = x.shape[0]
    pos = lax.broadcasted_iota(jnp.int32, x.shape, 0) % M_CHUNK
    shift = 1
    while shift < M_CHUNK:
        if reverse:
            shifted, ok = pltpu.roll(x, n - shift, 0), pos < M_CHUNK - shift
        else:
            shifted, ok = pltpu.roll(x, shift, 0), pos >= shift
        x = op(x, jnp.where(ok, shifted, fill))
        shift *= 2
    return x


W0_Q, W0_V, W0_O, W0_G = 0, 2 * M_QK_ALL, 2 * M_QK_ALL + D_MODEL, 2 * M_QK_ALL + 2 * D_MODEL


def _in_proj0_kernel(x_ref, nw_ref, w_ref, wkt_ref, gb_ref, q_ref, ktd_ref, v_ref, o_ref, f_ref):
    tm = x_ref.shape[0]
    hn = _rms(x_ref[...], nw_ref[...]).astype(BF16)
    gates = _dot(hn, w_ref[:, W0_G:]) + gb_ref[...]
    ig = gates[:, :N_GATE]
    fg = gates[:, N_GATE:]
    lf = jnp.minimum(fg, 0.0) - jnp.log1p(jnp.exp(-jnp.abs(fg)))
    is_fwd = lax.broadcasted_iota(jnp.int32, lf.shape, 1) < M_HEADS
    q_ref[...] = (_dot(hn, w_ref[:, W0_Q:W0_Q + M_QK_ALL]) * (M_QK ** -0.5)).astype(BF16)
    pre = _chunk_scan(lf, jnp.add, 0.0, False)
    suf = _chunk_scan(lf, jnp.add, 0.0, True)
    b = jnp.where(is_fwd, pre, suf)
    g = pre + suf - lf
    w = ig - b
    v_ref[...] = _dot(hn, w_ref[:, W0_V:W0_O]).astype(BF16)
    wpre = _chunk_scan(w, jnp.maximum, -jnp.inf, False)
    wsuf = _chunk_scan(w, jnp.maximum, -jnp.inf, True)
    for k, feat in enumerate((b, jnp.where(is_fwd, wpre, wsuf), g, jnp.maximum(wpre, wsuf), w, g + w)):
        f_ref[k] = feat
    o_ref[...] = _dot(hn, w_ref[:, W0_O:W0_G]).astype(BF16)
    kt = _dot_nt(wkt_ref[...], hn)
    low = lax.broadcasted_iota(jnp.int32, (M_QK_ALL, LANES), 1) < HALF
    for c in range(0, tm // M_CHUNK, 2):
        both = kt[:, c * M_CHUNK:(c + 2) * M_CHUNK]
        swapped = pltpu.roll(both, HALF, 1)
        ktd_ref[c] = jnp.where(low, both, swapped).astype(BF16)
        ktd_ref[c + 1] = jnp.where(low, swapped, both).astype(BF16)


def _in_proj0(x2, nw, w_in, w_kt, gate_bias, tm):
    T = x2.shape[0]
    row = lambda n: pl.BlockSpec((tm, n), lambda i: (i, 0))
    nch = tm // M_CHUNK
    return pl.pallas_call(
        _in_proj0_kernel,
        grid=(T // tm,),
        in_specs=[row(D_MODEL), _const_spec((1, D_MODEL)), _layer_spec(w_in.shape[1:], 0),
                  _const_spec(w_kt.shape), _const_spec((1, 2 * N_GATE))],
        out_specs=[row(M_QK_ALL), pl.BlockSpec((nch, M_QK_ALL, LANES), lambda i: (i, 0, 0)),
                   row(D_MODEL), row(D_MODEL), pl.BlockSpec((N_FEAT, tm, N_GATE), lambda i: (0, i, 0))],
        out_shape=[jax.ShapeDtypeStruct((T, M_QK_ALL), BF16),
                   jax.ShapeDtypeStruct((T // M_CHUNK, M_QK_ALL, LANES), BF16),
                   jax.ShapeDtypeStruct((T, D_MODEL), BF16), jax.ShapeDtypeStruct((T, D_MODEL), BF16),
                   jax.ShapeDtypeStruct((N_FEAT, T, N_GATE), F32)],
        compiler_params=_params(("parallel",)),
        name="in_proj0",
    )(x2, nw, w_in, w_kt, gate_bias)


N_PAIRS = M_HEADS // 2
HALF = LANES // 2
PAIR_W = 2 * 2 * M_V
ROW_W, ROW_A, ROW_G, ROW_WMAX, N_ROWF = 0, N_GATE // 2, N_GATE, 2 * N_GATE, 3 * N_GATE


def _lane_pair(x, col):
    lane = lax.broadcasted_iota(jnp.int32, (x.shape[0], LANES), 1)
    return jnp.where(lane < HALF, x[:, col:col + 1], x[:, col + 1:col + 2])


def _mlstm_chunk(q_ref, kt_ref, v_ref, cf_ref, rf_ref, h_ref, s_ref, sb_ref, mc_ref, mr_ref, sub, fwd):
    L = M_CHUNK
    r0 = pl.multiple_of(sub * L, L)
    rows = pl.ds(r0, L)
    lo = 0 if fwd else M_HEADS
    b_c = cf_ref[0, rows, :]
    m_r = mr_ref[...]
    mj = b_c + jnp.maximum(m_r, cf_ref[1, rows, :])
    u = b_c - mj
    iw = jnp.exp(b_c + m_r - mj)
    en = jnp.exp(-mj)
    mr_ref[...] = cf_ref[2, pl.ds(r0, 1), :] + jnp.maximum(m_r, cf_ref[3, pl.ds(r0, 1), :])
    rf = rf_ref[sub]
    m_c = mc_ref[...]
    g = rf[ROW_G + lo:ROW_G + lo + M_HEADS]
    m_new = g + jnp.maximum(m_c, rf[ROW_WMAX + lo:ROW_WMAX + lo + M_HEADS])
    decay = jnp.exp(g + m_c - m_new)
    mc_ref[...] = m_new

    low_half = lax.broadcasted_iota(jnp.int32, (1, LANES), 1) < HALF
    li = lax.broadcasted_iota(jnp.int32, (L, LANES), 0)
    si = lax.broadcasted_iota(jnp.int32, (L, LANES), 1) % HALF
    mask = (si <= li) if fwd else (si >= li)
    same_head = (lax.broadcasted_iota(jnp.int32, (LANES, LANES), 0) // HALF
                 == lax.broadcasted_iota(jnp.int32, (LANES, LANES), 1) // HALF)
    ones = jnp.ones((L, M_V), BF16)
    zeros = jnp.zeros((L, M_V), BF16)
    yield
    q2s, kbds, scores = [], [], []
    for p in range(N_PAIRS):
        q2s.append(q_ref[rows, p * LANES:(p + 1) * LANES])
        kbds.append(jnp.where(same_head, kt_ref[sub, p * LANES:(p + 1) * LANES, :], jnp.zeros((), BF16)))
        scores.append(_dot(q2s[p], kbds[p]))
    yield
    mains, inters = [], []
    for p in range(N_PAIRS):
        pr = lo // 2 + p
        w_pr = rf[ROW_W + pr:ROW_W + pr + 1]
        m_new_pr = jnp.where(low_half, m_new[2 * p:2 * p + 1], m_new[2 * p + 1:2 * p + 2])
        k_scale = jnp.exp(rf[ROW_A + pr:ROW_A + pr + 1] - m_new_pr)
        dw = jnp.exp(jnp.where(mask, _lane_pair(u, lo + 2 * p) + w_pr, -jnp.inf))
        sm = (scores[p] * dw).astype(BF16)
        kw = (kbds[p].astype(F32) * k_scale).astype(BF16)
        qi = (q2s[p].astype(F32) * _lane_pair(iw, lo + 2 * p)).astype(BF16)
        pad = jnp.concatenate([zeros, zeros], axis=1)
        pair = []
        for j in range(2):
            h = 2 * p + j
            v_ext = jnp.concatenate([v_ref[rows, h * M_V:(h + 1) * M_V], ones], axis=1)
            rhs = jnp.concatenate([v_ext, pad] if j == 0 else [pad, v_ext], axis=0)
            pair.append(_dot(jnp.concatenate([sm, kw[j * M_QK:(j + 1) * M_QK]], axis=0), rhs))
        mains.append(pair)
        inters.append(_dot(qi, sb_ref[p]))
    yield
    outs = []
    for p in range(N_PAIRS):
        for j in range(2):
            h = 2 * p + j
            out = mains[p][j][:L] + inters[p][:, j * 2 * M_V:(j + 1) * 2 * M_V]
            outs.append(out[:, :M_V] / jnp.maximum(jnp.abs(out[:, M_V:]), en[:, lo + h:lo + h + 1]))
            blk = (p, slice(j * M_QK, (j + 1) * M_QK), slice(j * 2 * M_V, (j + 1) * 2 * M_V))
            dec = jnp.concatenate([decay[h:h + 1]] * 2, axis=1)
            s_new = dec * s_ref[blk] + mains[p][j][L:]
            s_ref[blk] = s_new
            sb_ref[blk] = s_new.astype(BF16)
    h_ref[rows, :] = jnp.concatenate(outs, axis=1).astype(h_ref.dtype)
    yield


N_MLSTM_STAGES = 4


def _mlstm_kernel(qf, ktf, vf, cff, rff, qb, ktb, vb, cfb, rfb, hf_ref, hb_ref,
                  sf, sbf, mcf, mrf, sb, sbb, mcb, mrb, *, nsub):
    @pl.when(pl.program_id(1) == 0)
    def _():
        for ref in (sf, sbf, mcf, mrf, sb, sbb, mcb, mrb):
            ref[...] = jnp.zeros_like(ref)

    def body(j, carry):
        chunks = [_mlstm_chunk(qf, ktf, vf, cff, rff, hf_ref, sf, sbf, mcf, mrf, j, True),
                  _mlstm_chunk(qb, ktb, vb, cfb, rfb, hb_ref, sb, sbb, mcb, mrb, nsub - 1 - j, False)]
        for _ in range(N_MLSTM_STAGES):
            for chunk in chunks:
                next(chunk)
        return carry

    lax.fori_loop(0, nsub, body, 0)


def _mlstm(q, ktd, v, cfeat, rfeat, tb):
    B, S, _ = q.shape
    n = S // tb
    nsub = tb // M_CHUNK
    fw = lambda b, c: (b, c, 0)
    bw = lambda b, c: (b, n - 1 - c, 0)
    fw4 = lambda b, c: (b, c, 0, 0)
    bw4 = lambda b, c: (b, n - 1 - c, 0, 0)

    def specs(im, im4):
        return [pl.BlockSpec((None, tb, M_QK_ALL), im), pl.BlockSpec((None, nsub, M_QK_ALL, LANES), im4),
                pl.BlockSpec((None, tb, D_MODEL), im),
                pl.BlockSpec((4, None, tb, N_GATE), lambda b, c: (0,) + im(b, c)),
                pl.BlockSpec((None, nsub, N_ROWF, LANES), im4)]

    state = [pltpu.VMEM((N_PAIRS, LANES, PAIR_W), F32), pltpu.VMEM((N_PAIRS, LANES, PAIR_W), BF16),
             pltpu.VMEM((M_HEADS, LANES), F32), pltpu.VMEM((1, N_GATE), F32)]
    return pl.pallas_call(
        functools.partial(_mlstm_kernel, nsub=nsub),
        grid=(B, n),
        in_specs=specs(fw, fw4) + specs(bw, bw4),
        out_specs=[pl.BlockSpec((None, tb, D_MODEL), fw), pl.BlockSpec((None, tb, D_MODEL), bw)],
        out_shape=[jax.ShapeDtypeStruct((B, S, D_MODEL), BF16)] * 2,
        scratch_shapes=state + state,
        compiler_params=_params(("parallel", "arbitrary")),
        name="mlstm",
    )(q, ktd, v, cfeat, rfeat, q, ktd, v, cfeat, rfeat)


def _sigmoid(x):
    return 0.5 * jnp.tanh(0.5 * x) + 0.5


def _tail_common(mix_pair, x_ref, w_out_ref, nf_ref, wg_ref, wu_ref, wd_ref, npl_ref, pg_ref, p_ref, pp_ref):
    emb = _dot(p_ref[...].astype(BF16), pp_ref[...])
    h1 = x_ref[...]
    for i in range(D_MODEL // MXU_COLS):
        h1 = h1 + _dot(mix_pair(i), w_out_ref[i * MXU_COLS:(i + 1) * MXU_COLS, :])
    hn = _rms(h1, nf_ref[...]).astype(BF16)
    gate = _dot(hn, wg_ref[...])
    act = (gate * _sigmoid(gate) * _dot(hn, wu_ref[...])).astype(BF16)
    h2 = h1 + _dot(act, wd_ref[...])
    pgate = _sigmoid(_dot(_rms(h2, npl_ref[...]).astype(BF16), pg_ref[...]))
    return h2 + pgate * emb


N_SLABS = D_MODEL // LANES


def _tail0_kernel(hf_ref, hb_ref, o_ref, hnorm_ref, x_ref, w_out_ref, nf_ref, wg_ref, wu_ref, wd_ref,
                  npl_ref, pg_ref, p_ref, pp_ref, nnext_ref, out_ref, hn_ref, hn4_ref, hn16_ref, slab_ref,
                  slab4_ref):
    def head(h):
        sl = slice(h * M_V, (h + 1) * M_V)
        hh = hf_ref[:, sl].astype(F32) + hb_ref[:, sl].astype(F32)
        hh = hh * lax.rsqrt(jnp.mean(hh * hh, axis=-1, keepdims=True) + EPS) * hnorm_ref[:, sl]
        return (_sigmoid(o_ref[:, sl].astype(F32)) * hh).astype(BF16)

    h3 = _tail_common(lambda i: jnp.concatenate([head(2 * i), head(2 * i + 1)], axis=1), x_ref, w_out_ref,
                      nf_ref, wg_ref, wu_ref, wd_ref, npl_ref, pg_ref, p_ref, pp_ref)
    out_ref[...] = h3
    hn = _rms(h3, nnext_ref[...])
    hn_ref[...] = hn.astype(BF16)
    tm = hn.shape[0]
    q4 = tm // 4
    for s in range(N_SLABS):
        slab_ref[s] = hn[:, s * LANES:(s + 1) * LANES]
    for r4 in range(4):
        parts = [slab_ref[s, pl.ds(r4, q4, stride=4), :] for s in range(N_SLABS)]
        hn4_ref[r4] = jnp.concatenate(parts, axis=1).astype(BF16)
        for s in range(N_SLABS):
            slab4_ref[s, r4 * q4:(r4 + 1) * q4, :] = parts[s]
    for r16 in range(16):
        start = (r16 % 4) * q4 + r16 // 4
        hn16_ref[r16] = jnp.concatenate([slab4_ref[s, pl.ds(start, tm // 16, stride=4), :]
                                         for s in range(N_SLABS)], axis=1).astype(BF16)


def _tail1_kernel(o0_ref, o1_ref, o2_ref, l0_ref, l1_ref, l2_ref, x_ref, w_out_ref, nf_ref, wg_ref, wu_ref,
                  wd_ref, npl_ref, pg_ref, p_ref, pp_ref, nfin_ref, out_ref, slab1_ref, slab2_ref,
                  ls1_ref, ls2_ref, slabt_ref, lst_ref):
    tm = o0_ref.shape[0]
    q4 = tm // 4
    for r16 in range(16):
        rows = pl.ds((r16 % 4) * q4 + r16 // 4, tm // 16, stride=4)
        lst_ref[rows, :] = l2_ref[r16]
        o_r = o2_ref[r16].astype(F32)
        for s in range(N_SLABS):
            slabt_ref[s, rows, :] = o_r[:, s * LANES:(s + 1) * LANES]
    for r4 in range(4):
        rows = pl.ds(r4, q4, stride=4)
        block = slice(r4 * q4, (r4 + 1) * q4)
        ls1_ref[rows, :] = l1_ref[r4]
        ls2_ref[rows, :] = lst_ref[block, :]
        o_r = o1_ref[r4].astype(F32)
        for s in range(N_SLABS):
            slab1_ref[s, rows, :] = o_r[:, s * LANES:(s + 1) * LANES]
            slab2_ref[s, rows, :] = slabt_ref[s, block, :]
    l0 = l0_ref[...]
    l1 = ls1_ref[...]
    l2 = ls2_ref[...]
    mx = jnp.maximum(jnp.maximum(l0, l1), l2)
    e0 = jnp.exp(l0 - mx)
    e1 = jnp.exp(l1 - mx)
    e2 = jnp.exp(l2 - mx)
    inv = 1.0 / (e0 + e1 + e2)
    w1 = e1 * inv
    w2 = e2 * inv

    def head(h):
        o0 = o0_ref[:, h * A_DH:(h + 1) * A_DH].astype(F32)
        return (o0 + w1[:, h:h + 1] * (slab1_ref[h] - o0) + w2[:, h:h + 1] * (slab2_ref[h] - o0)).astype(BF16)

    h3 = _tail_common(lambda i: jnp.concatenate([head(2 * i), head(2 * i + 1)], axis=1), x_ref, w_out_ref,
                      nf_ref, wg_ref, wu_ref, wd_ref, npl_ref, pg_ref, p_ref, pp_ref)
    out_ref[...] = _rms(h3, nfin_ref[...])


def _layer_spec(shape, layer):
    return pl.BlockSpec((None,) + shape, lambda *_: (layer, 0, 0), pipeline_mode=pl.Buffered(1))


def _tail_weight_specs(layer):
    return [_layer_spec((D_MODEL, D_MODEL), 0), _const_spec((1, D_MODEL)),
            _layer_spec((D_MODEL, FFN_HIDDEN), layer), _layer_spec((D_MODEL, FFN_HIDDEN), layer),
            _layer_spec((FFN_HIDDEN, D_MODEL), layer), _const_spec((1, D_MODEL)),
            _layer_spec((D_MODEL, D_MODEL), layer)]


def _residue_spec(dil, tm, n, seq):
    nb = seq // tm
    return pl.BlockSpec((None, dil, tm // dil, n), lambda i: (i // nb, 0, i % nb, 0))


def _tail0(hf, hb, o, hnorm, x2, w_out, nf, wg, wu, wd, npl, pg, p2, pproj, nnext, tm, batch):
    T = x2.shape[0]
    S = T // batch
    row = lambda n: pl.BlockSpec((tm, n), lambda i: (i, 0))
    return pl.pallas_call(
        _tail0_kernel,
        grid=(T // tm,),
        in_specs=[row(D_MODEL), row(D_MODEL), row(D_MODEL), _const_spec((1, D_MODEL)), row(D_MODEL)]
        + _tail_weight_specs(0) + [pl.BlockSpec((None, tm, PLE_DIM), lambda i: (0, i, 0)),
                                   _layer_spec((PLE_DIM, D_MODEL), 0), _const_spec((1, D_MODEL))],
        out_specs=[row(D_MODEL), row(D_MODEL), _residue_spec(4, tm, D_MODEL, S),
                   _residue_spec(16, tm, D_MODEL, S)],
        out_shape=[jax.ShapeDtypeStruct((T, D_MODEL), F32), jax.ShapeDtypeStruct((T, D_MODEL), BF16),
                   jax.ShapeDtypeStruct((batch, 4, S // 4, D_MODEL), BF16),
                   jax.ShapeDtypeStruct((batch, 16, S // 16, D_MODEL), BF16)],
        scratch_shapes=[pltpu.VMEM((N_SLABS, tm, LANES), F32)] * 2,
        compiler_params=_params(("parallel",)),
        name="tail0",
    )(hf, hb, o, hnorm, x2, w_out, nf, wg, wu, wd, npl, pg, p2, pproj, nnext)


def _tail1(o0, o1, o2, l0, l1, l2, x2, w_out, nf, wg, wu, wd, npl, pg, p2, pproj, nfin, tm, batch):
    T = x2.shape[0]
    S = T // batch
    row = lambda n: pl.BlockSpec((tm, n), lambda i: (i, 0))
    return pl.pallas_call(
        _tail1_kernel,
        grid=(T // tm,),
        in_specs=[row(D_MODEL), _residue_spec(4, tm, D_MODEL, S), _residue_spec(16, tm, D_MODEL, S),
                  row(LANES), _residue_spec(4, tm, LANES, S), _residue_spec(16, tm, LANES, S), row(D_MODEL)]
        + _tail_weight_specs(1) + [pl.BlockSpec((None, tm, PLE_DIM), lambda i: (1, i, 0)),
                                   _layer_spec((PLE_DIM, D_MODEL), 1), _const_spec((1, D_MODEL))],
        out_specs=row(D_MODEL),
        out_shape=jax.ShapeDtypeStruct((T, D_MODEL), F32),
        scratch_shapes=[pltpu.VMEM((N_SLABS, tm, LANES), F32),
                        pltpu.VMEM((N_SLABS, tm, LANES), F32), pltpu.VMEM((tm, LANES), F32),
                        pltpu.VMEM((tm, LANES), F32), pltpu.VMEM((N_SLABS, tm, LANES), F32),
                        pltpu.VMEM((tm, LANES), F32)],
        compiler_params=_params(("parallel",)),
        name="tail1",
    )(o0, o1, o2, l0, l1, l2, x2, w_out, nf, wg, wu, wd, npl, pg, p2, pproj, nfin)


def _rope_tab_kernel(pos_ref, invf_ref, cos_ref, sin_ref, cos4_ref, sin4_ref, cos16_ref, sin16_ref):
    ang = pos_ref[...].astype(F32) * invf_ref[...]
    lane = lax.broadcasted_iota(jnp.int32, ang.shape, 1)
    s = jnp.sin(ang)
    cos_ref[...] = jnp.cos(ang)
    sin_ref[...] = jnp.where(lane < LANES // 2, -s, s)
    tm = ang.shape[0]
    for r4 in range(4):
        rows = pl.ds(r4, tm // 4, stride=4)
        cos4_ref[r4] = cos_ref[rows, :]
        sin4_ref[r4] = sin_ref[rows, :]
    for r16 in range(16):
        rows = pl.ds(r16 // 4, tm // 16, stride=4)
        cos16_ref[r16] = cos4_ref[r16 % 4, rows, :]
        sin16_ref[r16] = sin4_ref[r16 % 4, rows, :]


def _rope_tables(pos2, invf_lane, tm, batch):
    T = pos2.shape[0]
    S = T // batch
    row = pl.BlockSpec((tm, LANES), lambda i: (i, 0))
    res = lambda dil: jax.ShapeDtypeStruct((batch, dil, S // dil, LANES), F32)
    return pl.pallas_call(
        _rope_tab_kernel,
        grid=(T // tm,),
        in_specs=[pl.BlockSpec((tm, 1), lambda i: (i, 0)), _const_spec((1, LANES))],
        out_specs=[row, row] + [_residue_spec(4, tm, LANES, S)] * 2 + [_residue_spec(16, tm, LANES, S)] * 2,
        out_shape=[jax.ShapeDtypeStruct((T, LANES), F32)] * 2 + [res(4)] * 2 + [res(16)] * 2,
        compiler_params=_params(("parallel",)),
        name="rope_tab",
    )(pos2, invf_lane)


MXU_COLS = 256
LOG2E = 1.4426950408889634
LN2 = 0.6931471805599453


def _in_proj1_kernel(hn_ref, w_ref, cos_ref, sin_ref, out_ref):
    hn = hn_ref[...]
    q_scale = (A_DH ** -0.5) * LOG2E
    tables = ((cos_ref[...] * q_scale, sin_ref[...] * q_scale), (cos_ref[...], sin_ref[...]))
    for nb in range(3 * D_MODEL // MXU_COLS):
        kind = nb * MXU_COLS // D_MODEL
        acc = _dot(hn, w_ref[:, nb * MXU_COLS:(nb + 1) * MXU_COLS])
        for half in range(MXU_COLS // A_DH):
            seg = acc[:, half * A_DH:(half + 1) * A_DH]
            if kind < 2:
                c, s = tables[kind]
                seg = seg * c + pltpu.roll(seg, A_DH // 2, 1) * s
            col = nb * MXU_COLS + half * A_DH
            out_ref[:, col:col + A_DH] = seg.astype(BF16)


def _in_proj1(hn, w, g, cos_t, sin_t, tm):
    T = hn.shape[0]
    row = lambda n: pl.BlockSpec((tm, n), lambda i: (i, 0))
    return pl.pallas_call(
        _in_proj1_kernel,
        grid=(T // tm,),
        in_specs=[row(D_MODEL),
                  pl.BlockSpec((D_MODEL, 3 * D_MODEL), lambda i: (0, g), pipeline_mode=pl.Buffered(1)),
                  row(LANES), row(LANES)],
        out_specs=row(3 * D_MODEL),
        out_shape=jax.ShapeDtypeStruct((T, 3 * D_MODEL), BF16),
        compiler_params=_params(("parallel",)),
        name=f"in_proj1_g{g}",
    )(hn, w, cos_t, sin_t)


ATT_SUB = 128


def _attn_kernel(q_ref, kp_ref, kc_ref, kn_ref, vp_ref, vc_ref, vn_ref, o_ref, lse_ref, kall, vall,
                 *, blkq, n_keys):
    i = pl.program_id(2)
    kall[0:RADIUS, :] = kp_ref[...]
    kall[RADIUS:RADIUS + blkq, :] = kc_ref[...]
    kall[RADIUS + blkq:, :] = kn_ref[...]
    vall[0:RADIUS, :] = vp_ref[...]
    vall[RADIUS:RADIUS + blkq, :] = vc_ref[...]
    vall[RADIUS + blkq:, :] = vn_ref[...]
    nk = ATT_SUB + 2 * RADIUS
    r = lax.broadcasted_iota(jnp.int32, (ATT_SUB, nk), 0)
    c = lax.broadcasted_iota(jnp.int32, (ATT_SUB, nk), 1)
    band = jnp.abs(c - RADIUS - r) <= RADIUS
    lane = lax.broadcasted_iota(jnp.int32, (ATT_SUB, LANES), 1)
    ones = jnp.ones((nk, A_DH), BF16)
    for a in range(blkq // ATT_SUB):
        key0 = i * blkq + a * ATT_SUB - RADIUS
        valid = band & (c + key0 >= 0) & (c + key0 < n_keys)
        rows = slice(a * ATT_SUB, (a + 1) * ATT_SUB)
        krows = slice(a * ATT_SUB, a * ATT_SUB + nk)
        lse_tile = jnp.zeros((ATT_SUB, LANES), F32)
        for h in range(A_HEADS):
            sl = slice(h * A_DH, (h + 1) * A_DH)
            s = jnp.where(valid, _dot_nt(q_ref[rows, sl], kall[krows, sl]), NEG_INF)
            m = jnp.max(s, axis=1, keepdims=True)
            p = jnp.exp2(s - m).astype(BF16)
            pv = _dot(p, jnp.concatenate([vall[krows, sl], ones], axis=1))
            den = pv[:, A_DH:]
            o_ref[rows, sl] = (pv[:, :A_DH] / den).astype(BF16)
            lse_tile = jnp.where(lane == h, (m + jnp.log2(den)) * LN2, lse_tile)
        lse_ref[rows, :] = lse_tile


def _attention(proj, blkq):
    B, dil, U, _ = proj.shape
    blkq = min(blkq, U)
    hb = blkq // RADIUS
    nhalo = U // RADIUS

    def cur(j):
        return pl.BlockSpec((None, None, blkq, D_MODEL), lambda b, r, i: (b, r, i, j))

    def prev(j):
        return pl.BlockSpec((None, None, RADIUS, D_MODEL),
                            lambda b, r, i: (b, r, jnp.maximum(i * hb - 1, 0), j))

    def nxt(j):
        return pl.BlockSpec((None, None, RADIUS, D_MODEL),
                            lambda b, r, i: (b, r, jnp.minimum((i + 1) * hb, nhalo - 1), j))

    return pl.pallas_call(
        functools.partial(_attn_kernel, blkq=blkq, n_keys=U),
        grid=(B, dil, U // blkq),
        in_specs=[cur(0), prev(1), cur(1), nxt(1), prev(2), cur(2), nxt(2)],
        out_specs=[pl.BlockSpec((None, None, blkq, D_MODEL), lambda b, r, i: (b, r, i, 0)),
                   pl.BlockSpec((None, None, blkq, LANES), lambda b, r, i: (b, r, i, 0))],
        out_shape=[jax.ShapeDtypeStruct((B, dil, U, D_MODEL), BF16),
                   jax.ShapeDtypeStruct((B, dil, U, LANES), F32)],
        scratch_shapes=[pltpu.VMEM((blkq + 2 * RADIUS, D_MODEL), BF16)] * 2,
        compiler_params=_params(("parallel", "parallel", "parallel")),
        name=f"attn_d{dil}",
    )(proj, proj, proj, proj, proj, proj, proj)


def kernel(x, p, positions, norm_mix, a_w_in, a_gate_bias, a_head_norm, a_w_out, b_w_in, b_w_out,
           norm_ffn, w_gate, w_up, w_down, norm_ple, ple_gate, ple_proj, final_norm):
    B, S, _ = x.shape
    T = B * S
    bf = lambda w: w.astype(BF16)
    vec = lambda w: w.reshape(1, -1).astype(F32)
    x2 = x.reshape(T, D_MODEL)
    nc = S // M_CHUNK

    w_in = bf(a_w_in)
    q, ktd, v, o, feat = _in_proj0(x2, vec(norm_mix[0]), w_in, w_in[0, :, M_QK_ALL:2 * M_QK_ALL].T,
                                   vec(a_gate_bias[0]), tm=1024)
    by_chunk = feat.reshape(N_FEAT, B, nc, M_CHUNK, N_GATE)
    pair_rows = lambda z: jnp.swapaxes(z, 2, 3).reshape(B, nc, N_GATE // 2, LANES)
    per_chunk = lambda z: jnp.broadcast_to(z[:, :, 0, :, None], (B, nc, N_GATE, LANES))
    rfeat = jnp.concatenate([pair_rows(by_chunk[4]), pair_rows(by_chunk[5]),
                             per_chunk(by_chunk[2]), per_chunk(by_chunk[3])], axis=2)
    hf, hb = _mlstm(q.reshape(B, S, -1), ktd.reshape(B, nc, M_QK_ALL, LANES), v.reshape(B, S, -1),
                    feat.reshape(N_FEAT, B, S, N_GATE), rfeat, tb=512)
    wg_all, wu_all, wd_all, pg_all, pp_all = bf(w_gate), bf(w_up), bf(w_down), bf(ple_gate), bf(ple_proj)
    p_all = p.reshape(p.shape[0], T, PLE_DIM)
    h, *hns = _tail0(hf.reshape(T, -1), hb.reshape(T, -1), o, vec(a_head_norm[0]), x2, bf(a_w_out),
                     vec(norm_ffn[0]), wg_all, wu_all, wd_all, vec(norm_ple[0]), pg_all, p_all, pp_all,
                     vec(norm_mix[1]), tm=512, batch=B)

    inv_freq = ROPE_THETA ** (-jnp.arange(0, ROPE_DIM, 2, dtype=F32) / ROPE_DIM)
    invf_lane = (jnp.zeros((1, LANES), F32).at[0, :ROPE_HALF].set(inv_freq)
                 .at[0, LANES // 2:LANES // 2 + ROPE_HALF].set(inv_freq))
    tabs = _rope_tables(positions.reshape(T, 1), invf_lane, tm=1024, batch=B)
    dim_order = jnp.concatenate([jnp.arange(0, ROPE_HALF), jnp.arange(ROPE_DIM, ROPE_DIM + 48),
                                 jnp.arange(ROPE_HALF, ROPE_DIM), jnp.arange(ROPE_DIM + 48, A_DH)])
    w1 = bf(b_w_in[0]).reshape(D_MODEL, N_GROUPS, 3, A_HEADS, A_DH)
    w1 = jnp.concatenate([w1[:, :, :2][..., dim_order], w1[:, :, 2:]], axis=2).reshape(D_MODEL, -1)
    o_g, l_g = [], []
    for g, (_, dil) in enumerate(DILATED_GROUPS):
        proj = _in_proj1(hns[g].reshape(T, D_MODEL), w1, g, tabs[2 * g].reshape(T, LANES),
                         tabs[2 * g + 1].reshape(T, LANES), tm=2048)
        og, lg = _attention(proj.reshape(B, dil, S // dil, 3 * D_MODEL), blkq=1024)
        o_g.append(og)
        l_g.append(lg)
    out = _tail1(o_g[0].reshape(T, D_MODEL), o_g[1], o_g[2], l_g[0].reshape(T, LANES), l_g[1], l_g[2], h,
                 bf(b_w_out), vec(norm_ffn[1]), wg_all, wu_all, wd_all, vec(norm_ple[1]), pg_all, p_all, pp_all,
                 vec(final_norm), tm=512, batch=B)
    return out.reshape(B, S, D_MODEL)
```

```python
import functools

import jax
import jax.numpy as jnp
from jax import lax
from jax.experimental import pallas as pl
from jax.experimental.pallas import tpu as pltpu

F32 = jnp.float32
BF16 = jnp.bfloat16

D_MODEL = 1024
LANES = 128
EPS = 1e-6

M_HEADS = 8
M_QK = 64
M_V = 128
M_CHUNK = 64
M_QK_ALL = M_HEADS * M_QK

A_HEADS = 8
A_DH = 128
RADIUS = 64
DILATED_GROUPS = ((128, 1), (512, 4), (2048, 16))
N_GROUPS = 3
ROPE_DIM = 32
ROPE_HALF = 16
ROPE_THETA = 500000.0
NEG_INF = -1e30

FFN_HIDDEN = 2816
PLE_DIM = 256

VMEM_LIMIT = 60 * 1024 * 1024


def _dot(a, b):
    return jnp.dot(a, b, preferred_element_type=F32)


def _dot_nt(a, b):
    return lax.dot_general(a, b, (((1,), (1,)), ((), ())), preferred_element_type=F32)


def _rms(x, w):
    ms = jnp.mean(x * x, axis=-1, keepdims=True)
    return x * lax.rsqrt(ms + EPS) * w


def _const_spec(shape):
    nd = len(shape)
    return pl.BlockSpec(shape, lambda *_: (0,) * nd, pipeline_mode=pl.Buffered(1))


def _params(sem):
    return pltpu.CompilerParams(dimension_semantics=sem, vmem_limit_bytes=VMEM_LIMIT)


N_DIRS = 2
N_GATE = N_DIRS * M_HEADS
N_FEAT = 4


def _chunk_scan(x, op, fill, reverse):
    n = x.shape[0]
    pos = lax.broadcasted_iota(jnp.int32, x.shape, 0) % M_CHUNK
    shift = 1
    while shift < M_CHUNK:
        if reverse:
            shifted, ok = pltpu.roll(x, n - shift, 0), pos < M_CHUNK - shift
        else:
            shifted, ok = pltpu.roll(x, shift, 0), pos >= shift
        x = op(x, jnp.where(ok, shifted, fill))
        shift *= 2
    return x


W0_Q, W0_V, W0_O, W0_G = 0, 2 * M_QK_ALL, 2 * M_QK_ALL + D_MODEL, 2 * M_QK_ALL + 2 * D_MODEL


def _chunk_first(x):
    pos = lax.broadcasted_iota(jnp.int32, x.shape, 0) % M_CHUNK
    shift = 1
    while shift < M_CHUNK:
        x = jnp.where(pos >= shift, pltpu.roll(x, shift, 0), x)
        shift *= 2
    return x


def _in_proj0_kernel(x_ref, nw_ref, w_ref, wkt_ref, gb_ref, q_ref, ktd_ref, v_ref, o_ref, f_ref, rf_ref):
    tm = x_ref.shape[0]
    hn = _rms(x_ref[...], nw_ref[...]).astype(BF16)
    gates = _dot(hn, w_ref[:, W0_G:]) + gb_ref[...]
    ig = gates[:, :N_GATE]
    fg = gates[:, N_GATE:]
    lf = jnp.minimum(fg, 0.0) - jnp.log1p(jnp.exp(-jnp.abs(fg)))
    is_fwd = lax.broadcasted_iota(jnp.int32, lf.shape, 1) < M_HEADS
    q_ref[...] = (_dot(hn, w_ref[:, W0_Q:W0_Q + M_QK_ALL]) * (M_QK ** -0.5)).astype(BF16)
    pre = _chunk_scan(lf, jnp.add, 0.0, False)
    suf = _chunk_scan(lf, jnp.add, 0.0, True)
    b = jnp.where(is_fwd, pre, suf)
    g_tok = pre + suf - lf
    w = ig - b
    v_ref[...] = _dot(hn, w_ref[:, W0_V:W0_O]).astype(BF16)
    wpre = _chunk_scan(w, jnp.maximum, -jnp.inf, False)
    wsuf = _chunk_scan(w, jnp.maximum, -jnp.inf, True)
    wmax = jnp.maximum(wpre, wsuf)
    g = _chunk_first(g_tok)
    for k, feat in enumerate((b, jnp.where(is_fwd, wpre, wsuf), g, wmax)):
        f_ref[k] = feat
    o_ref[...] = _dot(hn, w_ref[:, W0_O:W0_G]).astype(BF16)
    for k, feat in enumerate((w, g + w, g, wmax)):
        feat_t = feat.T
        for t in range(tm // LANES):
            rf_ref[t, k * N_GATE:(k + 1) * N_GATE, :] = feat_t[:, t * LANES:(t + 1) * LANES]
    kt = _dot_nt(wkt_ref[...], hn)
    low = lax.broadcasted_iota(jnp.int32, (M_QK_ALL, LANES), 1) < HALF
    for c in range(0, tm // M_CHUNK, 2):
        both = kt[:, c * M_CHUNK:(c + 2) * M_CHUNK]
        swapped = pltpu.roll(both, HALF, 1)
        ktd_ref[c] = jnp.where(low, both, swapped).astype(BF16)
        ktd_ref[c + 1] = jnp.where(low, swapped, both).astype(BF16)


def _in_proj0(x2, nw, w_in, w_kt, gate_bias, tm):
    T = x2.shape[0]
    row = lambda n: pl.BlockSpec((tm, n), lambda i: (i, 0))
    nch = tm // M_CHUNK
    return pl.pallas_call(
        _in_proj0_kernel,
        grid=(T // tm,),
        in_specs=[row(D_MODEL), _const_spec((1, D_MODEL)), _layer_spec(w_in.shape[1:], 0),
                  _const_spec(w_kt.shape), _const_spec((1, 2 * N_GATE))],
        out_specs=[row(M_QK_ALL), pl.BlockSpec((nch, M_QK_ALL, LANES), lambda i: (i, 0, 0)),
                   row(D_MODEL), row(D_MODEL), pl.BlockSpec((N_FEAT, tm, N_GATE), lambda i: (0, i, 0)),
                   pl.BlockSpec((tm // LANES, N_FEAT * N_GATE, LANES), lambda i: (i, 0, 0))],
        out_shape=[jax.ShapeDtypeStruct((T, M_QK_ALL), BF16),
                   jax.ShapeDtypeStruct((T // M_CHUNK, M_QK_ALL, LANES), BF16),
                   jax.ShapeDtypeStruct((T, D_MODEL), BF16), jax.ShapeDtypeStruct((T, D_MODEL), BF16),
                   jax.ShapeDtypeStruct((N_FEAT, T, N_GATE), F32),
                   jax.ShapeDtypeStruct((T // LANES, N_FEAT * N_GATE, LANES), F32)],
        compiler_params=_params(("parallel",)),
        name="in_proj0",
    )(x2, nw, w_in, w_kt, gate_bias)


N_PAIRS = M_HEADS // 2
HALF = LANES // 2
PAIR_W = 2 * 2 * M_V
ROW_W, ROW_A, ROW_G, ROW_WMAX, N_ROWF = 0, N_GATE, 2 * N_GATE, 3 * N_GATE, 4 * N_GATE


def _lane_pair(x, col):
    lane = lax.broadcasted_iota(jnp.int32, (x.shape[0], LANES), 1)
    return jnp.where(lane < HALF, x[:, col:col + 1], x[:, col + 1:col + 2])


def _mlstm_chunk(q_ref, kt_ref, v_ref, cf_ref, rf_ref, h_ref, s_ref, sb_ref, mc_ref, mr_ref, sub, fwd):
    L = M_CHUNK
    r0 = pl.multiple_of(sub * L, L)
    rows = pl.ds(r0, L)
    lo = 0 if fwd else M_HEADS
    b_c = cf_ref[0, rows, :]
    m_r = mr_ref[...]
    mj = b_c + jnp.maximum(m_r, cf_ref[1, rows, :])
    u = b_c - mj
    iw = jnp.exp(b_c + m_r - mj)
    en = jnp.exp(-mj)
    mr_ref[...] = cf_ref[2, pl.ds(r0, 1), :] + jnp.maximum(m_r, cf_ref[3, pl.ds(r0, 1), :])
    low_half = lax.broadcasted_iota(jnp.int32, (1, LANES), 1) < HALF
    tile = rf_ref[sub // 2]
    swapped = pltpu.roll(tile, HALF, 1)
    first = sub % 2 == 0
    own_lo = jnp.where(first, tile, swapped)
    own_hi = jnp.where(first, swapped, tile)
    both = jnp.where(low_half, own_lo, own_hi)
    m_c = mc_ref[...]
    g = both[ROW_G + lo:ROW_G + lo + M_HEADS]
    m_new = g + jnp.maximum(m_c, both[ROW_WMAX + lo:ROW_WMAX + lo + M_HEADS])
    decay = jnp.exp(g + m_c - m_new)
    mc_ref[...] = m_new

    li = lax.broadcasted_iota(jnp.int32, (L, LANES), 0)
    si = lax.broadcasted_iota(jnp.int32, (L, LANES), 1) % HALF
    mask = (si <= li) if fwd else (si >= li)
    same_head = (lax.broadcasted_iota(jnp.int32, (LANES, LANES), 0) // HALF
                 == lax.broadcasted_iota(jnp.int32, (LANES, LANES), 1) // HALF)
    ones = jnp.ones((L, M_V), BF16)
    zeros = jnp.zeros((L, M_V), BF16)
    yield
    q2s, kbds, scores = [], [], []
    for p in range(N_PAIRS):
        q2s.append(q_ref[rows, p * LANES:(p + 1) * LANES])
        kbds.append(jnp.where(same_head, kt_ref[sub, p * LANES:(p + 1) * LANES, :], jnp.zeros((), BF16)))
        scores.append(_dot(q2s[p], kbds[p]))
    yield
    mains, inters = [], []
    for p in range(N_PAIRS):
        def pair_row(base):
            r = base + lo + 2 * p
            return jnp.where(low_half, own_lo[r:r + 1], own_hi[r + 1:r + 2])

        w_pr = pair_row(ROW_W)
        m_new_pr = jnp.where(low_half, m_new[2 * p:2 * p + 1], m_new[2 * p + 1:2 * p + 2])
        k_scale = jnp.exp(pair_row(ROW_A) - m_new_pr)
        dw = jnp.exp(jnp.where(mask, _lane_pair(u, lo + 2 * p) + w_pr, -jnp.inf))
        sm = (scores[p] * dw).astype(BF16)
        kw = (kbds[p].astype(F32) * k_scale).astype(BF16)
        qi = (q2s[p].astype(F32) * _lane_pair(iw, lo + 2 * p)).astype(BF16)
        pad = jnp.concatenate([zeros, zeros], axis=1)
        pair = []
        for j in range(2):
            h = 2 * p + j
            v_ext = jnp.concatenate([v_ref[rows, h * M_V:(h + 1) * M_V], ones], axis=1)
            rhs = jnp.concatenate([v_ext, pad] if j == 0 else [pad, v_ext], axis=0)
            pair.append(_dot(jnp.concatenate([sm, kw[j * M_QK:(j + 1) * M_QK]], axis=0), rhs))
        mains.append(pair)
        inters.append(_dot(qi, sb_ref[p]))
    yield
    outs = []
    for p in range(N_PAIRS):
        for j in range(2):
            h = 2 * p + j
            out = mains[p][j][:L] + inters[p][:, j * 2 * M_V:(j + 1) * 2 * M_V]
            outs.append(out[:, :M_V] / jnp.maximum(jnp.abs(out[:, M_V:]), en[:, lo + h:lo + h + 1]))
            blk = (p, slice(j * M_QK, (j + 1) * M_QK), slice(j * 2 * M_V, (j + 1) * 2 * M_V))
            dec = jnp.concatenate([decay[h:h + 1]] * 2, axis=1)
            s_new = dec * s_ref[blk] + mains[p][j][L:]
            s_ref[blk] = s_new
            sb_ref[blk] = s_new.astype(BF16)
    h_ref[rows, :] = jnp.concatenate(outs, axis=1).astype(h_ref.dtype)
    yield


N_MLSTM_STAGES = 4


def _mlstm_kernel(qf, ktf, vf, cff, rff, qb, ktb, vb, cfb, rfb, hf_ref, hb_ref,
                  sf, sbf, mcf, mrf, sb, sbb, mcb, mrb, *, nsub):
    @pl.when(pl.program_id(1) == 0)
    def _():
        for ref in (sf, sbf, mcf, mrf, sb, sbb, mcb, mrb):
            ref[...] = jnp.zeros_like(ref)

    def body(j, carry):
        chunks = [_mlstm_chunk(qf, ktf, vf, cff, rff, hf_ref, sf, sbf, mcf, mrf, j, True),
                  _mlstm_chunk(qb, ktb, vb, cfb, rfb, hb_ref, sb, sbb, mcb, mrb, nsub - 1 - j, False)]
        for _ in range(N_MLSTM_STAGES):
            for chunk in chunks:
                next(chunk)
        return carry

    lax.fori_loop(0, nsub, body, 0)


def _mlstm(q, ktd, v, cfeat, rfeat, tb):
    B, S, _ = q.shape
    n = S // tb
    nsub = tb // M_CHUNK
    fw = lambda b, c: (b, c, 0)
    bw = lambda b, c: (b, n - 1 - c, 0)
    fw4 = lambda b, c: (b, c, 0, 0)
    bw4 = lambda b, c: (b, n - 1 - c, 0, 0)

    def specs(im, im4):
        return [pl.BlockSpec((None, tb, M_QK_ALL), im), pl.BlockSpec((None, nsub, M_QK_ALL, LANES), im4),
                pl.BlockSpec((None, tb, D_MODEL), im),
                pl.BlockSpec((4, None, tb, N_GATE), lambda b, c: (0,) + im(b, c)),
                pl.BlockSpec((None, nsub // 2, N_ROWF, LANES), im4)]

    state = [pltpu.VMEM((N_PAIRS, LANES, PAIR_W), F32), pltpu.VMEM((N_PAIRS, LANES, PAIR_W), BF16),
             pltpu.VMEM((M_HEADS, LANES), F32), pltpu.VMEM((1, N_GATE), F32)]
    return pl.pallas_call(
        functools.partial(_mlstm_kernel, nsub=nsub),
        grid=(B, n),
        in_specs=specs(fw, fw4) + specs(bw, bw4),
        out_specs=[pl.BlockSpec((None, tb, D_MODEL), fw), pl.BlockSpec((None, tb, D_MODEL), bw)],
        out_shape=[jax.ShapeDtypeStruct((B, S, D_MODEL), BF16)] * 2,
        scratch_shapes=state + state,
        compiler_params=_params(("parallel", "arbitrary")),
        name="mlstm",
    )(q, ktd, v, cfeat, rfeat, q, ktd, v, cfeat, rfeat)


def _sigmoid(x):
    return 0.5 * jnp.tanh(0.5 * x) + 0.5


def _tail_common(mix_pair, x_ref, w_out_ref, nf_ref, wg_ref, wu_ref, wd_ref, npl_ref, pg_ref, p_ref, pp_ref):
    emb = _dot(p_ref[...].astype(BF16), pp_ref[...])
    h1 = x_ref[...]
    for i in range(D_MODEL // MXU_COLS):
        h1 = h1 + _dot(mix_pair(i), w_out_ref[i * MXU_COLS:(i + 1) * MXU_COLS, :])
    hn = _rms(h1, nf_ref[...]).astype(BF16)
    gate = _dot(hn, wg_ref[...])
    act = (gate * _sigmoid(gate) * _dot(hn, wu_ref[...])).astype(BF16)
    h2 = h1 + _dot(act, wd_ref[...])
    pgate = _sigmoid(_dot(_rms(h2, npl_ref[...]).astype(BF16), pg_ref[...]))
    return h2 + pgate * emb


N_SLABS = D_MODEL // LANES


def _tail0_kernel(hf_ref, hb_ref, o_ref, hnorm_ref, x_ref, w_out_ref, nf_ref, wg_ref, wu_ref, wd_ref,
                  npl_ref, pg_ref, p_ref, pp_ref, nnext_ref, out_ref, hn_ref, hn4_ref, hn16_ref, slab_ref,
                  slab4_ref):
    def head(h):
        sl = slice(h * M_V, (h + 1) * M_V)
        hh = hf_ref[:, sl].astype(F32) + hb_ref[:, sl].astype(F32)
        hh = hh * lax.rsqrt(jnp.mean(hh * hh, axis=-1, keepdims=True) + EPS) * hnorm_ref[:, sl]
        return (_sigmoid(o_ref[:, sl].astype(F32)) * hh).astype(BF16)

    h3 = _tail_common(lambda i: jnp.concatenate([head(2 * i), head(2 * i + 1)], axis=1), x_ref, w_out_ref,
                      nf_ref, wg_ref, wu_ref, wd_ref, npl_ref, pg_ref, p_ref, pp_ref)
    out_ref[...] = h3
    hn = _rms(h3, nnext_ref[...])
    hn_ref[...] = hn.astype(BF16)
    tm = hn.shape[0]
    q4 = tm // 4
    for s in range(N_SLABS):
        slab_ref[s] = hn[:, s * LANES:(s + 1) * LANES]
    for r4 in range(4):
        parts = [slab_ref[s, pl.ds(r4, q4, stride=4), :] for s in range(N_SLABS)]
        hn4_ref[r4] = jnp.concatenate(parts, axis=1).astype(BF16)
        for s in range(N_SLABS):
            slab4_ref[s, r4 * q4:(r4 + 1) * q4, :] = parts[s]
    for r16 in range(16):
        start = (r16 % 4) * q4 + r16 // 4
        hn16_ref[r16] = jnp.concatenate([slab4_ref[s, pl.ds(start, tm // 16, stride=4), :]
                                         for s in range(N_SLABS)], axis=1).astype(BF16)


def _tail1_kernel(o0_ref, o1_ref, o2_ref, l0_ref, l1_ref, l2_ref, x_ref, w_out_ref, nf_ref, wg_ref, wu_ref,
                  wd_ref, npl_ref, pg_ref, p_ref, pp_ref, nfin_ref, out_ref, slab1_ref, slab2_ref,
                  ls1_ref, ls2_ref, slabt_ref, lst_ref):
    tm = o0_ref.shape[0]
    q4 = tm // 4
    for r16 in range(16):
        rows = pl.ds((r16 % 4) * q4 + r16 // 4, tm // 16, stride=4)
        lst_ref[rows, :] = l2_ref[r16]
        o_r = o2_ref[r16].astype(F32)
        for s in range(N_SLABS):
            slabt_ref[s, rows, :] = o_r[:, s * LANES:(s + 1) * LANES]
    for r4 in range(4):
        rows = pl.ds(r4, q4, stride=4)
        block = slice(r4 * q4, (r4 + 1) * q4)
        ls1_ref[rows, :] = l1_ref[r4]
        ls2_ref[rows, :] = lst_ref[block, :]
        o_r = o1_ref[r4].astype(F32)
        for s in range(N_SLABS):
            slab1_ref[s, rows, :] = o_r[:, s * LANES:(s + 1) * LANES]
            slab2_ref[s, rows, :] = slabt_ref[s, block, :]
    l0 = l0_ref[...]
    l1 = ls1_ref[...]
    l2 = ls2_ref[...]
    mx = jnp.maximum(jnp.maximum(l0, l1), l2)
    e0 = jnp.exp(l0 - mx)
    e1 = jnp.exp(l1 - mx)
    e2 = jnp.exp(l2 - mx)
    inv = 1.0 / (e0 + e1 + e2)
    w1 = e1 * inv
    w2 = e2 * inv

    def head(h):
        o0 = o0_ref[:, h * A_DH:(h + 1) * A_DH].astype(F32)
        return (o0 + w1[:, h:h + 1] * (slab1_ref[h] - o0) + w2[:, h:h + 1] * (slab2_ref[h] - o0)).astype(BF16)

    h3 = _tail_common(lambda i: jnp.concatenate([head(2 * i), head(2 * i + 1)], axis=1), x_ref, w_out_ref,
                      nf_ref, wg_ref, wu_ref, wd_ref, npl_ref, pg_ref, p_ref, pp_ref)
    out_ref[...] = _rms(h3, nfin_ref[...])


def _layer_spec(shape, layer):
    return pl.BlockSpec((None,) + shape, lambda *_: (layer, 0, 0), pipeline_mode=pl.Buffered(1))


def _tail_weight_specs(layer):
    return [_layer_spec((D_MODEL, D_MODEL), 0), _const_spec((1, D_MODEL)),
            _layer_spec((D_MODEL, FFN_HIDDEN), layer), _layer_spec((D_MODEL, FFN_HIDDEN), layer),
            _layer_spec((FFN_HIDDEN, D_MODEL), layer), _const_spec((1, D_MODEL)),
            _layer_spec((D_MODEL, D_MODEL), layer)]


def _residue_spec(dil, tm, n, seq):
    nb = seq // tm
    return pl.BlockSpec((None, dil, tm // dil, n), lambda i: (i // nb, 0, i % nb, 0))


def _tail0(hf, hb, o, hnorm, x2, w_out, nf, wg, wu, wd, npl, pg, p2, pproj, nnext, tm, batch):
    T = x2.shape[0]
    S = T // batch
    row = lambda n: pl.BlockSpec((tm, n), lambda i: (i, 0))
    return pl.pallas_call(
        _tail0_kernel,
        grid=(T // tm,),
        in_specs=[row(D_MODEL), row(D_MODEL), row(D_MODEL), _const_spec((1, D_MODEL)), row(D_MODEL)]
        + _tail_weight_specs(0) + [pl.BlockSpec((None, tm, PLE_DIM), lambda i: (0, i, 0)),
                                   _layer_spec((PLE_DIM, D_MODEL), 0), _const_spec((1, D_MODEL))],
        out_specs=[row(D_MODEL), row(D_MODEL), _residue_spec(4, tm, D_MODEL, S),
                   _residue_spec(16, tm, D_MODEL, S)],
        out_shape=[jax.ShapeDtypeStruct((T, D_MODEL), F32), jax.ShapeDtypeStruct((T, D_MODEL), BF16),
                   jax.ShapeDtypeStruct((batch, 4, S // 4, D_MODEL), BF16),
                   jax.ShapeDtypeStruct((batch, 16, S // 16, D_MODEL), BF16)],
        scratch_shapes=[pltpu.VMEM((N_SLABS, tm, LANES), F32)] * 2,
        compiler_params=_params(("parallel",)),
        name="tail0",
    )(hf, hb, o, hnorm, x2, w_out, nf, wg, wu, wd, npl, pg, p2, pproj, nnext)


def _tail1(o0, o1, o2, l0, l1, l2, x2, w_out, nf, wg, wu, wd, npl, pg, p2, pproj, nfin, tm, batch):
    T = x2.shape[0]
    S = T // batch
    row = lambda n: pl.BlockSpec((tm, n), lambda i: (i, 0))
    return pl.pallas_call(
        _tail1_kernel,
        grid=(T // tm,),
        in_specs=[row(D_MODEL), _residue_spec(4, tm, D_MODEL, S), _residue_spec(16, tm, D_MODEL, S),
                  row(LANES), _residue_spec(4, tm, LANES, S), _residue_spec(16, tm, LANES, S), row(D_MODEL)]
        + _tail_weight_specs(1) + [pl.BlockSpec((None, tm, PLE_DIM), lambda i: (1, i, 0)),
                                   _layer_spec((PLE_DIM, D_MODEL), 1), _const_spec((1, D_MODEL))],
        out_specs=row(D_MODEL),
        out_shape=jax.ShapeDtypeStruct((T, D_MODEL), F32),
        scratch_shapes=[pltpu.VMEM((N_SLABS, tm, LANES), F32),
                        pltpu.VMEM((N_SLABS, tm, LANES), F32), pltpu.VMEM((tm, LANES), F32),
                        pltpu.VMEM((tm, LANES), F32), pltpu.VMEM((N_SLABS, tm, LANES), F32),
                        pltpu.VMEM((tm, LANES), F32)],
        compiler_params=_params(("parallel",)),
        name="tail1",
    )(o0, o1, o2, l0, l1, l2, x2, w_out, nf, wg, wu, wd, npl, pg, p2, pproj, nfin)


def _rope_tab_kernel(pos_ref, invf_ref, cos_ref, sin_ref, cos4_ref, sin4_ref, cos16_ref, sin16_ref):
    ang = pos_ref[...].astype(F32) * invf_ref[...]
    lane = lax.broadcasted_iota(jnp.int32, ang.shape, 1)
    s = jnp.sin(ang)
    cos_ref[...] = jnp.cos(ang)
    sin_ref[...] = jnp.where(lane < LANES // 2, -s, s)
    tm = ang.shape[0]
    for r4 in range(4):
        rows = pl.ds(r4, tm // 4, stride=4)
        cos4_ref[r4] = cos_ref[rows, :]
        sin4_ref[r4] = sin_ref[rows, :]
    for r16 in range(16):
        rows = pl.ds(r16 // 4, tm // 16, stride=4)
        cos16_ref[r16] = cos4_ref[r16 % 4, rows, :]
        sin16_ref[r16] = sin4_ref[r16 % 4, rows, :]


def _rope_tables(pos2, invf_lane, tm, batch):
    T = pos2.shape[0]
    S = T // batch
    row = pl.BlockSpec((tm, LANES), lambda i: (i, 0))
    res = lambda dil: jax.ShapeDtypeStruct((batch, dil, S // dil, LANES), F32)
    return pl.pallas_call(
        _rope_tab_kernel,
        grid=(T // tm,),
        in_specs=[pl.BlockSpec((tm, 1), lambda i: (i, 0)), _const_spec((1, LANES))],
        out_specs=[row, row] + [_residue_spec(4, tm, LANES, S)] * 2 + [_residue_spec(16, tm, LANES, S)] * 2,
        out_shape=[jax.ShapeDtypeStruct((T, LANES), F32)] * 2 + [res(4)] * 2 + [res(16)] * 2,
        compiler_params=_params(("parallel",)),
        name="rope_tab",
    )(pos2, invf_lane)


MXU_COLS = 256
LOG2E = 1.4426950408889634
LN2 = 0.6931471805599453


def _in_proj1_kernel(hn_ref, wqk_ref, wv_ref, cos_ref, sin_ref, out_ref):
    hn = hn_ref[...]
    q_scale = (A_DH ** -0.5) * LOG2E
    tables = ((cos_ref[...] * q_scale, sin_ref[...] * q_scale), (cos_ref[...], sin_ref[...]))
    for nb in range(3 * D_MODEL // MXU_COLS):
        kind = nb * MXU_COLS // D_MODEL
        w_ref, c0 = (wqk_ref, nb * MXU_COLS) if kind < 2 else (wv_ref, nb * MXU_COLS - 2 * D_MODEL)
        acc = _dot(hn, w_ref[:, c0:c0 + MXU_COLS])
        for half in range(MXU_COLS // A_DH):
            seg = acc[:, half * A_DH:(half + 1) * A_DH]
            if kind < 2:
                c, s = tables[kind]
                seg = seg * c + pltpu.roll(seg, A_DH // 2, 1) * s
            col = nb * MXU_COLS + half * A_DH
            out_ref[:, col:col + A_DH] = seg.astype(BF16)


def _in_proj1(hn, w_qk, w_v, g, cos_t, sin_t, tm):
    T = hn.shape[0]
    row = lambda n: pl.BlockSpec((tm, n), lambda i: (i, 0))
    group_cols = lambda n: pl.BlockSpec((D_MODEL, n), lambda i: (0, g), pipeline_mode=pl.Buffered(1))
    return pl.pallas_call(
        _in_proj1_kernel,
        grid=(T // tm,),
        in_specs=[row(D_MODEL), group_cols(2 * D_MODEL), group_cols(D_MODEL), row(LANES), row(LANES)],
        out_specs=row(3 * D_MODEL),
        out_shape=jax.ShapeDtypeStruct((T, 3 * D_MODEL), BF16),
        compiler_params=_params(("parallel",)),
        name=f"in_proj1_g{g}",
    )(hn, w_qk, w_v, cos_t, sin_t)


ATT_SUB = 128


def _attn_kernel(q_ref, kp_ref, kc_ref, kn_ref, vp_ref, vc_ref, vn_ref, o_ref, lse_ref, kall, vall,
                 *, blkq, n_keys):
    i = pl.program_id(2)
    kall[0:RADIUS, :] = kp_ref[...]
    kall[RADIUS:RADIUS + blkq, :] = kc_ref[...]
    kall[RADIUS + blkq:, :] = kn_ref[...]
    vall[0:RADIUS, :] = vp_ref[...]
    vall[RADIUS:RADIUS + blkq, :] = vc_ref[...]
    vall[RADIUS + blkq:, :] = vn_ref[...]
    nk = ATT_SUB + 2 * RADIUS
    r = lax.broadcasted_iota(jnp.int32, (ATT_SUB, nk), 0)
    c = lax.broadcasted_iota(jnp.int32, (ATT_SUB, nk), 1)
    band = jnp.abs(c - RADIUS - r) <= RADIUS
    lane = lax.broadcasted_iota(jnp.int32, (ATT_SUB, LANES), 1)
    ones = jnp.ones((nk, A_DH), BF16)
    for a in range(blkq // ATT_SUB):
        key0 = i * blkq + a * ATT_SUB - RADIUS
        valid = band & (c + key0 >= 0) & (c + key0 < n_keys)
        rows = slice(a * ATT_SUB, (a + 1) * ATT_SUB)
        krows = slice(a * ATT_SUB, a * ATT_SUB + nk)
        lse_tile = jnp.zeros((ATT_SUB, LANES), F32)
        for h in range(A_HEADS):
            sl = slice(h * A_DH, (h + 1) * A_DH)
            s = jnp.where(valid, _dot_nt(q_ref[rows, sl], kall[krows, sl]), NEG_INF)
            m = jnp.max(s, axis=1, keepdims=True)
            p = jnp.exp2(s - m).astype(BF16)
            pv = _dot(p, jnp.concatenate([vall[krows, sl], ones], axis=1))
            den = pv[:, A_DH:]
            o_ref[rows, sl] = (pv[:, :A_DH] / den).astype(BF16)
            lse_tile = jnp.where(lane == h, (m + jnp.log2(den)) * LN2, lse_tile)
        lse_ref[rows, :] = lse_tile


def _attention(proj, blkq):
    B, dil, U, _ = proj.shape
    blkq = min(blkq, U)
    hb = blkq // RADIUS
    nhalo = U // RADIUS

    def cur(j):
        return pl.BlockSpec((None, None, blkq, D_MODEL), lambda b, r, i: (b, r, i, j))

    def prev(j):
        return pl.BlockSpec((None, None, RADIUS, D_MODEL),
                            lambda b, r, i: (b, r, jnp.maximum(i * hb - 1, 0), j))

    def nxt(j):
        return pl.BlockSpec((None, None, RADIUS, D_MODEL),
                            lambda b, r, i: (b, r, jnp.minimum((i + 1) * hb, nhalo - 1), j))

    return pl.pallas_call(
        functools.partial(_attn_kernel, blkq=blkq, n_keys=U),
        grid=(B, dil, U // blkq),
        in_specs=[cur(0), prev(1), cur(1), nxt(1), prev(2), cur(2), nxt(2)],
        out_specs=[pl.BlockSpec((None, None, blkq, D_MODEL), lambda b, r, i: (b, r, i, 0)),
                   pl.BlockSpec((None, None, blkq, LANES), lambda b, r, i: (b, r, i, 0))],
        out_shape=[jax.ShapeDtypeStruct((B, dil, U, D_MODEL), BF16),
                   jax.ShapeDtypeStruct((B, dil, U, LANES), F32)],
        scratch_shapes=[pltpu.VMEM((blkq + 2 * RADIUS, D_MODEL), BF16)] * 2,
        compiler_params=_params(("parallel", "parallel", "parallel")),
        name=f"attn_d{dil}",
    )(proj, proj, proj, proj, proj, proj, proj)


def kernel(x, p, positions, norm_mix, a_w_in, a_gate_bias, a_head_norm, a_w_out, b_w_in, b_w_out,
           norm_ffn, w_gate, w_up, w_down, norm_ple, ple_gate, ple_proj, final_norm):
    B, S, _ = x.shape
    T = B * S
    bf = lambda w: w.astype(BF16)
    vec = lambda w: w.reshape(1, -1).astype(F32)
    x2 = x.reshape(T, D_MODEL)
    nc = S // M_CHUNK

    w_in = bf(a_w_in)
    q, ktd, v, o, feat, rfeat = _in_proj0(x2, vec(norm_mix[0]), w_in, w_in[0, :, M_QK_ALL:2 * M_QK_ALL].T,
                                          vec(a_gate_bias[0]), tm=1024)
    hf, hb = _mlstm(q.reshape(B, S, -1), ktd.reshape(B, nc, M_QK_ALL, LANES), v.reshape(B, S, -1),
                    feat.reshape(N_FEAT, B, S, N_GATE), rfeat.reshape(B, S // LANES, N_ROWF, LANES), tb=512)
    wg_all, wu_all, wd_all, pg_all, pp_all = bf(w_gate), bf(w_up), bf(w_down), bf(ple_gate), bf(ple_proj)
    p_all = p.reshape(p.shape[0], T, PLE_DIM)
    h, *hns = _tail0(hf.reshape(T, -1), hb.reshape(T, -1), o, vec(a_head_norm[0]), x2, bf(a_w_out),
                     vec(norm_ffn[0]), wg_all, wu_all, wd_all, vec(norm_ple[0]), pg_all, p_all, pp_all,
                     vec(norm_mix[1]), tm=512, batch=B)

    inv_freq = ROPE_THETA ** (-jnp.arange(0, ROPE_DIM, 2, dtype=F32) / ROPE_DIM)
    invf_lane = (jnp.zeros((1, LANES), F32).at[0, :ROPE_HALF].set(inv_freq)
                 .at[0, LANES // 2:LANES // 2 + ROPE_HALF].set(inv_freq))
    tabs = _rope_tables(positions.reshape(T, 1), invf_lane, tm=1024, batch=B)
    dim_order = jnp.concatenate([jnp.arange(0, ROPE_HALF), jnp.arange(ROPE_DIM, ROPE_DIM + 48),
                                 jnp.arange(ROPE_HALF, ROPE_DIM), jnp.arange(ROPE_DIM + 48, A_DH)])
    reorder = (jnp.arange(A_DH)[:, None] == dim_order[None, :]).astype(BF16)
    w1 = bf(b_w_in[0]).reshape(D_MODEL, N_GROUPS, 3, A_HEADS * A_DH)
    w1_qk = jnp.dot(w1[:, :, :2].reshape(-1, A_DH), reorder, preferred_element_type=F32).astype(BF16)
    w1_qk = w1_qk.reshape(D_MODEL, N_GROUPS * 2 * A_HEADS * A_DH)
    w1_v = w1[:, :, 2].reshape(D_MODEL, N_GROUPS * A_HEADS * A_DH)
    o_g, l_g = [], []
    for g, (_, dil) in enumerate(DILATED_GROUPS):
        proj = _in_proj1(hns[g].reshape(T, D_MODEL), w1_qk, w1_v, g, tabs[2 * g].reshape(T, LANES),
                         tabs[2 * g + 1].reshape(T, LANES), tm=2048)
        og, lg = _attention(proj.reshape(B, dil, S // dil, 3 * D_MODEL), blkq=1024)
        o_g.append(og)
        l_g.append(lg)
    out = _tail1(o_g[0].reshape(T, D_MODEL), o_g[1], o_g[2], l_g[0].reshape(T, LANES), l_g[1], l_g[2], h,
                 bf(b_w_out), vec(norm_ffn[1]), wg_all, wu_all, wd_all, vec(norm_ple[1]), pg_all, p_all, pp_all,
                 vec(final_norm), tm=512, batch=B)
    return out.reshape(B, S, D_MODEL)
```

```python
import functools

import jax
import jax.numpy as jnp
from jax import lax
from jax.experimental import pallas as pl
from jax.experimental.pallas import tpu as pltpu

F32 = jnp.float32
BF16 = jnp.bfloat16

D_MODEL = 1024
LANES = 128
EPS = 1e-6

M_HEADS = 8
M_QK = 64
M_V = 128
M_CHUNK = 64
M_QK_ALL = M_HEADS * M_QK

A_HEADS = 8
A_DH = 128
RADIUS = 64
DILATED_GROUPS = ((128, 1), (512, 4), (2048, 16))
N_GROUPS = 3
ROPE_DIM = 32
ROPE_HALF = 16
ROPE_THETA = 500000.0
NEG_INF = -1e30

FFN_HIDDEN = 2816
PLE_DIM = 256

VMEM_LIMIT = 60 * 1024 * 1024

ROWS_IN_PROJ0 = 1024
ROWS_MLSTM = 1024
ROWS_TAIL = 512
ROWS_ROPE = 1024
ROWS_IN_PROJ1 = 2048
ROWS_ATTN = 1024


def _dot(a, b):
    return jnp.dot(a, b, preferred_element_type=F32)


def _dot_nt(a, b):
    return lax.dot_general(a, b, (((1,), (1,)), ((), ())), preferred_element_type=F32)


def _rms(x, w):
    ms = jnp.mean(x * x, axis=-1, keepdims=True)
    return x * lax.rsqrt(ms + EPS) * w


def _const_spec(shape):
    nd = len(shape)
    return pl.BlockSpec(shape, lambda *_: (0,) * nd, pipeline_mode=pl.Buffered(1))


def _params(sem):
    return pltpu.CompilerParams(dimension_semantics=sem, vmem_limit_bytes=VMEM_LIMIT)


N_DIRS = 2
N_GATE = N_DIRS * M_HEADS
N_FEAT = 4


def _chunk_scan(x, op, fill, reverse):
    n = x.shape[0]
    pos = lax.broadcasted_iota(jnp.int32, x.shape, 0) % M_CHUNK
    shift = 1
    while shift < M_CHUNK:
        if reverse:
            shifted, ok = pltpu.roll(x, n - shift, 0), pos < M_CHUNK - shift
        else:
            shifted, ok = pltpu.roll(x, shift, 0), pos >= shift
        x = op(x, jnp.where(ok, shifted, fill))
        shift *= 2
    return x


W0_Q, W0_V, W0_O, W0_G = 0, 2 * M_QK_ALL, 2 * M_QK_ALL + D_MODEL, 2 * M_QK_ALL + 2 * D_MODEL


def _chunk_first(x):
    pos = lax.broadcasted_iota(jnp.int32, x.shape, 0) % M_CHUNK
    shift = 1
    while shift < M_CHUNK:
        x = jnp.where(pos >= shift, pltpu.roll(x, shift, 0), x)
        shift *= 2
    return x


def _in_proj0_kernel(x_ref, nw_ref, w_ref, wkt_ref, gb_ref, q_ref, ktd_ref, v_ref, o_ref, f_ref, rf_ref):
    tm = x_ref.shape[0]
    hn = _rms(x_ref[...], nw_ref[...]).astype(BF16)
    gates = _dot(hn, w_ref[:, W0_G:]) + gb_ref[...]
    ig = gates[:, :N_GATE]
    fg = gates[:, N_GATE:]
    lf = jnp.minimum(fg, 0.0) - jnp.log1p(jnp.exp(-jnp.abs(fg)))
    is_fwd = lax.broadcasted_iota(jnp.int32, lf.shape, 1) < M_HEADS
    q_ref[...] = (_dot(hn, w_ref[:, W0_Q:W0_Q + M_QK_ALL]) * (M_QK ** -0.5)).astype(BF16)
    pre = _chunk_scan(lf, jnp.add, 0.0, False)
    suf = _chunk_scan(lf, jnp.add, 0.0, True)
    b = jnp.where(is_fwd, pre, suf)
    g_tok = pre + suf - lf
    w = ig - b
    v_ref[...] = _dot(hn, w_ref[:, W0_V:W0_O]).astype(BF16)
    wpre = _chunk_scan(w, jnp.maximum, -jnp.inf, False)
    wsuf = _chunk_scan(w, jnp.maximum, -jnp.inf, True)
    wmax = jnp.maximum(wpre, wsuf)
    g = _chunk_first(g_tok)
    for k, feat in enumerate((b, jnp.where(is_fwd, wpre, wsuf), g, wmax)):
        f_ref[k] = feat
    o_ref[...] = _dot(hn, w_ref[:, W0_O:W0_G]).astype(BF16)
    for k, feat in enumerate((w, g + w, g, wmax)):
        feat_t = feat.T
        for t in range(tm // LANES):
            rf_ref[t, k * N_GATE:(k + 1) * N_GATE, :] = feat_t[:, t * LANES:(t + 1) * LANES]
    kt = _dot_nt(wkt_ref[...], hn)
    low = lax.broadcasted_iota(jnp.int32, (M_QK_ALL, LANES), 1) < HALF
    for c in range(0, tm // M_CHUNK, 2):
        both = kt[:, c * M_CHUNK:(c + 2) * M_CHUNK]
        swapped = pltpu.roll(both, HALF, 1)
        ktd_ref[c] = jnp.where(low, both, swapped).astype(BF16)
        ktd_ref[c + 1] = jnp.where(low, swapped, both).astype(BF16)


def _in_proj0(x2, nw, w_in, w_kt, gate_bias, tm):
    T = x2.shape[0]
    row = lambda n: pl.BlockSpec((tm, n), lambda i: (i, 0))
    nch = tm // M_CHUNK
    return pl.pallas_call(
        _in_proj0_kernel,
        grid=(T // tm,),
        in_specs=[row(D_MODEL), _const_spec((1, D_MODEL)), _layer_spec(w_in.shape[1:], 0),
                  _const_spec(w_kt.shape), _const_spec((1, 2 * N_GATE))],
        out_specs=[row(M_QK_ALL), pl.BlockSpec((nch, M_QK_ALL, LANES), lambda i: (i, 0, 0)),
                   row(D_MODEL), row(D_MODEL), pl.BlockSpec((N_FEAT, tm, N_GATE), lambda i: (0, i, 0)),
                   pl.BlockSpec((tm // LANES, N_FEAT * N_GATE, LANES), lambda i: (i, 0, 0))],
        out_shape=[jax.ShapeDtypeStruct((T, M_QK_ALL), BF16),
                   jax.ShapeDtypeStruct((T // M_CHUNK, M_QK_ALL, LANES), BF16),
                   jax.ShapeDtypeStruct((T, D_MODEL), BF16), jax.ShapeDtypeStruct((T, D_MODEL), BF16),
                   jax.ShapeDtypeStruct((N_FEAT, T, N_GATE), F32),
                   jax.ShapeDtypeStruct((T // LANES, N_FEAT * N_GATE, LANES), F32)],
        compiler_params=_params(("parallel",)),
        name="in_proj0",
    )(x2, nw, w_in, w_kt, gate_bias)


N_PAIRS = M_HEADS // 2
HALF = LANES // 2
PAIR_W = 2 * 2 * M_V
ROW_W, ROW_A, ROW_G, ROW_WMAX, N_ROWF = 0, N_GATE, 2 * N_GATE, 3 * N_GATE, 4 * N_GATE


def _lane_pair(x, col):
    lane = lax.broadcasted_iota(jnp.int32, (x.shape[0], LANES), 1)
    return jnp.where(lane < HALF, x[:, col:col + 1], x[:, col + 1:col + 2])


def _mlstm_chunk(q_ref, kt_ref, v_ref, cf_ref, rf_ref, h_ref, s_ref, sb_ref, mc_ref, mr_ref, sub, fwd):
    L = M_CHUNK
    r0 = pl.multiple_of(sub * L, L)
    rows = pl.ds(r0, L)
    lo = 0 if fwd else M_HEADS
    b_c = cf_ref[0, rows, :]
    m_r = mr_ref[...]
    mj = b_c + jnp.maximum(m_r, cf_ref[1, rows, :])
    u = b_c - mj
    iw = jnp.exp(b_c + m_r - mj)
    en = jnp.exp(-mj)
    mr_ref[...] = cf_ref[2, pl.ds(r0, 1), :] + jnp.maximum(m_r, cf_ref[3, pl.ds(r0, 1), :])
    low_half = lax.broadcasted_iota(jnp.int32, (1, LANES), 1) < HALF
    tile = rf_ref[sub // 2]
    swapped = pltpu.roll(tile, HALF, 1)
    first = sub % 2 == 0
    own_lo = jnp.where(first, tile, swapped)
    own_hi = jnp.where(first, swapped, tile)
    both = jnp.where(low_half, own_lo, own_hi)
    m_c = mc_ref[...]
    g = both[ROW_G + lo:ROW_G + lo + M_HEADS]
    m_new = g + jnp.maximum(m_c, both[ROW_WMAX + lo:ROW_WMAX + lo + M_HEADS])
    decay = jnp.exp(g + m_c - m_new)
    mc_ref[...] = m_new

    li = lax.broadcasted_iota(jnp.int32, (L, LANES), 0)
    si = lax.broadcasted_iota(jnp.int32, (L, LANES), 1) % HALF
    mask = (si <= li) if fwd else (si >= li)
    same_head = (lax.broadcasted_iota(jnp.int32, (LANES, LANES), 0) // HALF
                 == lax.broadcasted_iota(jnp.int32, (LANES, LANES), 1) // HALF)
    ones = jnp.ones((L, M_V), BF16)
    zeros = jnp.zeros((L, M_V), BF16)
    yield
    q2s, kbds, scores = [], [], []
    for p in range(N_PAIRS):
        q2s.append(q_ref[rows, p * LANES:(p + 1) * LANES])
        kbds.append(jnp.where(same_head, kt_ref[sub, p * LANES:(p + 1) * LANES, :], jnp.zeros((), BF16)))
        scores.append(_dot(q2s[p], kbds[p]))
    yield
    mains, inters = [], []
    for p in range(N_PAIRS):
        def pair_row(base):
            r = base + lo + 2 * p
            return jnp.where(low_half, own_lo[r:r + 1], own_hi[r + 1:r + 2])

        w_pr = pair_row(ROW_W)
        m_new_pr = jnp.where(low_half, m_new[2 * p:2 * p + 1], m_new[2 * p + 1:2 * p + 2])
        k_scale = jnp.exp(pair_row(ROW_A) - m_new_pr)
        dw = jnp.exp(jnp.where(mask, _lane_pair(u, lo + 2 * p) + w_pr, -jnp.inf))
        sm = (scores[p] * dw).astype(BF16)
        kw = (kbds[p].astype(F32) * k_scale).astype(BF16)
        qi = (q2s[p].astype(F32) * _lane_pair(iw, lo + 2 * p)).astype(BF16)
        pad = jnp.concatenate([zeros, zeros], axis=1)
        pair = []
        for j in range(2):
            h = 2 * p + j
            v_ext = jnp.concatenate([v_ref[rows, h * M_V:(h + 1) * M_V], ones], axis=1)
            rhs = jnp.concatenate([v_ext, pad] if j == 0 else [pad, v_ext], axis=0)
            pair.append(_dot(jnp.concatenate([sm, kw[j * M_QK:(j + 1) * M_QK]], axis=0), rhs))
        mains.append(pair)
        inters.append(_dot(qi, sb_ref[p]))
    yield
    outs = []
    for p in range(N_PAIRS):
        for j in range(2):
            h = 2 * p + j
            out = mains[p][j][:L] + inters[p][:, j * 2 * M_V:(j + 1) * 2 * M_V]
            outs.append(out[:, :M_V] / jnp.maximum(jnp.abs(out[:, M_V:]), en[:, lo + h:lo + h + 1]))
            blk = (p, slice(j * M_QK, (j + 1) * M_QK), slice(j * 2 * M_V, (j + 1) * 2 * M_V))
            dec = jnp.concatenate([decay[h:h + 1]] * 2, axis=1)
            s_new = dec * s_ref[blk] + mains[p][j][L:]
            s_ref[blk] = s_new
            sb_ref[blk] = s_new.astype(BF16)
    h_ref[rows, :] = jnp.concatenate(outs, axis=1).astype(h_ref.dtype)
    yield


N_MLSTM_STAGES = 4


def _mlstm_kernel(qf, ktf, vf, cff, rff, qb, ktb, vb, cfb, rfb, hf_ref, hb_ref,
                  sf, sbf, mcf, mrf, sb, sbb, mcb, mrb, *, nsub):
    @pl.when(pl.program_id(1) == 0)
    def _():
        for ref in (sf, sbf, mcf, mrf, sb, sbb, mcb, mrb):
            ref[...] = jnp.zeros_like(ref)

    def body(j, carry):
        chunks = [_mlstm_chunk(qf, ktf, vf, cff, rff, hf_ref, sf, sbf, mcf, mrf, j, True),
                  _mlstm_chunk(qb, ktb, vb, cfb, rfb, hb_ref, sb, sbb, mcb, mrb, nsub - 1 - j, False)]
        for _ in range(N_MLSTM_STAGES):
            for chunk in chunks:
                next(chunk)
        return carry

    lax.fori_loop(0, nsub, body, 0)


def _mlstm(q, ktd, v, cfeat, rfeat, tb):
    B, S, _ = q.shape
    n = S // tb
    nsub = tb // M_CHUNK
    fw = lambda b, c: (b, c, 0)
    bw = lambda b, c: (b, n - 1 - c, 0)
    fw4 = lambda b, c: (b, c, 0, 0)
    bw4 = lambda b, c: (b, n - 1 - c, 0, 0)

    def specs(im, im4):
        return [pl.BlockSpec((None, tb, M_QK_ALL), im), pl.BlockSpec((None, nsub, M_QK_ALL, LANES), im4),
                pl.BlockSpec((None, tb, D_MODEL), im),
                pl.BlockSpec((4, None, tb, N_GATE), lambda b, c: (0,) + im(b, c)),
                pl.BlockSpec((None, nsub // 2, N_ROWF, LANES), im4)]

    state = [pltpu.VMEM((N_PAIRS, LANES, PAIR_W), F32), pltpu.VMEM((N_PAIRS, LANES, PAIR_W), BF16),
             pltpu.VMEM((M_HEADS, LANES), F32), pltpu.VMEM((1, N_GATE), F32)]
    return pl.pallas_call(
        functools.partial(_mlstm_kernel, nsub=nsub),
        grid=(B, n),
        in_specs=specs(fw, fw4) + specs(bw, bw4),
        out_specs=[pl.BlockSpec((None, tb, D_MODEL), fw), pl.BlockSpec((None, tb, D_MODEL), bw)],
        out_shape=[jax.ShapeDtypeStruct((B, S, D_MODEL), BF16)] * 2,
        scratch_shapes=state + state,
        compiler_params=_params(("parallel", "arbitrary")),
        name="mlstm",
    )(q, ktd, v, cfeat, rfeat, q, ktd, v, cfeat, rfeat)


def _sigmoid(x):
    return 0.5 * jnp.tanh(0.5 * x) + 0.5


def _tail_common(mix_pair, x_ref, w_out_ref, nf_ref, wg_ref, wu_ref, wd_ref, npl_ref, pg_ref, p_ref, pp_ref):
    emb = _dot(p_ref[...].astype(BF16), pp_ref[...])
    h1 = x_ref[...]
    for i in range(D_MODEL // MXU_COLS):
        h1 = h1 + _dot(mix_pair(i), w_out_ref[i * MXU_COLS:(i + 1) * MXU_COLS, :])
    hn = _rms(h1, nf_ref[...]).astype(BF16)
    gate = _dot(hn, wg_ref[...])
    act = (gate * _sigmoid(gate) * _dot(hn, wu_ref[...])).astype(BF16)
    h2 = h1 + _dot(act, wd_ref[...])
    pgate = _sigmoid(_dot(_rms(h2, npl_ref[...]).astype(BF16), pg_ref[...]))
    return h2 + pgate * emb


N_SLABS = D_MODEL // LANES


def _tail0_kernel(hf_ref, hb_ref, o_ref, hnorm_ref, x_ref, w_out_ref, nf_ref, wg_ref, wu_ref, wd_ref,
                  npl_ref, pg_ref, p_ref, pp_ref, nnext_ref, out_ref, hn_ref, hn4_ref, hn16_ref, slab_ref,
                  slab4_ref):
    def head(h):
        sl = slice(h * M_V, (h + 1) * M_V)
        hh = hf_ref[:, sl].astype(F32) + hb_ref[:, sl].astype(F32)
        hh = hh * lax.rsqrt(jnp.mean(hh * hh, axis=-1, keepdims=True) + EPS) * hnorm_ref[:, sl]
        return (_sigmoid(o_ref[:, sl].astype(F32)) * hh).astype(BF16)

    h3 = _tail_common(lambda i: jnp.concatenate([head(2 * i), head(2 * i + 1)], axis=1), x_ref, w_out_ref,
                      nf_ref, wg_ref, wu_ref, wd_ref, npl_ref, pg_ref, p_ref, pp_ref)
    out_ref[...] = h3
    hn = _rms(h3, nnext_ref[...])
    hn_ref[...] = hn.astype(BF16)
    tm = hn.shape[0]
    q4 = tm // 4
    for s in range(N_SLABS):
        slab_ref[s] = hn[:, s * LANES:(s + 1) * LANES]
    for r4 in range(4):
        parts = [slab_ref[s, pl.ds(r4, q4, stride=4), :] for s in range(N_SLABS)]
        hn4_ref[r4] = jnp.concatenate(parts, axis=1).astype(BF16)
        for s in range(N_SLABS):
            slab4_ref[s, r4 * q4:(r4 + 1) * q4, :] = parts[s]
    for r16 in range(16):
        start = (r16 % 4) * q4 + r16 // 4
        hn16_ref[r16] = jnp.concatenate([slab4_ref[s, pl.ds(start, tm // 16, stride=4), :]
                                         for s in range(N_SLABS)], axis=1).astype(BF16)


def _tail1_kernel(o0_ref, o1_ref, o2_ref, l0_ref, l1_ref, l2_ref, x_ref, w_out_ref, nf_ref, wg_ref, wu_ref,
                  wd_ref, npl_ref, pg_ref, p_ref, pp_ref, nfin_ref, out_ref, slab1_ref, slab2_ref,
                  ls1_ref, ls2_ref, slabt_ref, lst_ref):
    tm = o0_ref.shape[0]
    q4 = tm // 4
    for r16 in range(16):
        rows = pl.ds((r16 % 4) * q4 + r16 // 4, tm // 16, stride=4)
        lst_ref[rows, :] = l2_ref[r16]
        o_r = o2_ref[r16].astype(F32)
        for s in range(N_SLABS):
            slabt_ref[s, rows, :] = o_r[:, s * LANES:(s + 1) * LANES]
    for r4 in range(4):
        rows = pl.ds(r4, q4, stride=4)
        block = slice(r4 * q4, (r4 + 1) * q4)
        ls1_ref[rows, :] = l1_ref[r4]
        ls2_ref[rows, :] = lst_ref[block, :]
        o_r = o1_ref[r4].astype(F32)
        for s in range(N_SLABS):
            slab1_ref[s, rows, :] = o_r[:, s * LANES:(s + 1) * LANES]
            slab2_ref[s, rows, :] = slabt_ref[s, block, :]
    l0 = l0_ref[...]
    l1 = ls1_ref[...]
    l2 = ls2_ref[...]
    mx = jnp.maximum(jnp.maximum(l0, l1), l2)
    e0 = jnp.exp(l0 - mx)
    e1 = jnp.exp(l1 - mx)
    e2 = jnp.exp(l2 - mx)
    inv = 1.0 / (e0 + e1 + e2)
    w1 = e1 * inv
    w2 = e2 * inv

    def head(h):
        o0 = o0_ref[:, h * A_DH:(h + 1) * A_DH].astype(F32)
        return (o0 + w1[:, h:h + 1] * (slab1_ref[h] - o0) + w2[:, h:h + 1] * (slab2_ref[h] - o0)).astype(BF16)

    h3 = _tail_common(lambda i: jnp.concatenate([head(2 * i), head(2 * i + 1)], axis=1), x_ref, w_out_ref,
                      nf_ref, wg_ref, wu_ref, wd_ref, npl_ref, pg_ref, p_ref, pp_ref)
    out_ref[...] = _rms(h3, nfin_ref[...])


def _layer_spec(shape, layer):
    return pl.BlockSpec((None,) + shape, lambda *_: (layer, 0, 0), pipeline_mode=pl.Buffered(1))


def _tail_weight_specs(layer):
    return [_layer_spec((D_MODEL, D_MODEL), 0), _const_spec((1, D_MODEL)),
            _layer_spec((D_MODEL, FFN_HIDDEN), layer), _layer_spec((D_MODEL, FFN_HIDDEN), layer),
            _layer_spec((FFN_HIDDEN, D_MODEL), layer), _const_spec((1, D_MODEL)),
            _layer_spec((D_MODEL, D_MODEL), layer)]


def _residue_spec(dil, tm, n, seq):
    nb = seq // tm
    return pl.BlockSpec((None, dil, tm // dil, n), lambda i: (i // nb, 0, i % nb, 0))


def _tail0(hf, hb, o, hnorm, x2, w_out, nf, wg, wu, wd, npl, pg, p2, pproj, nnext, tm, batch):
    T = x2.shape[0]
    S = T // batch
    row = lambda n: pl.BlockSpec((tm, n), lambda i: (i, 0))
    return pl.pallas_call(
        _tail0_kernel,
        grid=(T // tm,),
        in_specs=[row(D_MODEL), row(D_MODEL), row(D_MODEL), _const_spec((1, D_MODEL)), row(D_MODEL)]
        + _tail_weight_specs(0) + [pl.BlockSpec((None, tm, PLE_DIM), lambda i: (0, i, 0)),
                                   _layer_spec((PLE_DIM, D_MODEL), 0), _const_spec((1, D_MODEL))],
        out_specs=[row(D_MODEL), row(D_MODEL), _residue_spec(4, tm, D_MODEL, S),
                   _residue_spec(16, tm, D_MODEL, S)],
        out_shape=[jax.ShapeDtypeStruct((T, D_MODEL), F32), jax.ShapeDtypeStruct((T, D_MODEL), BF16),
                   jax.ShapeDtypeStruct((batch, 4, S // 4, D_MODEL), BF16),
                   jax.ShapeDtypeStruct((batch, 16, S // 16, D_MODEL), BF16)],
        scratch_shapes=[pltpu.VMEM((N_SLABS, tm, LANES), F32)] * 2,
        compiler_params=_params(("parallel",)),
        name="tail0",
    )(hf, hb, o, hnorm, x2, w_out, nf, wg, wu, wd, npl, pg, p2, pproj, nnext)


def _tail1(o0, o1, o2, l0, l1, l2, x2, w_out, nf, wg, wu, wd, npl, pg, p2, pproj, nfin, tm, batch):
    T = x2.shape[0]
    S = T // batch
    row = lambda n: pl.BlockSpec((tm, n), lambda i: (i, 0))
    return pl.pallas_call(
        _tail1_kernel,
        grid=(T // tm,),
        in_specs=[row(D_MODEL), _residue_spec(4, tm, D_MODEL, S), _residue_spec(16, tm, D_MODEL, S),
                  row(LANES), _residue_spec(4, tm, LANES, S), _residue_spec(16, tm, LANES, S), row(D_MODEL)]
        + _tail_weight_specs(1) + [pl.BlockSpec((None, tm, PLE_DIM), lambda i: (1, i, 0)),
                                   _layer_spec((PLE_DIM, D_MODEL), 1), _const_spec((1, D_MODEL))],
        out_specs=row(D_MODEL),
        out_shape=jax.ShapeDtypeStruct((T, D_MODEL), F32),
        scratch_shapes=[pltpu.VMEM((N_SLABS, tm, LANES), F32),
                        pltpu.VMEM((N_SLABS, tm, LANES), F32), pltpu.VMEM((tm, LANES), F32),
                        pltpu.VMEM((tm, LANES), F32), pltpu.VMEM((N_SLABS, tm, LANES), F32),
                        pltpu.VMEM((tm, LANES), F32)],
        compiler_params=_params(("parallel",)),
        name="tail1",
    )(o0, o1, o2, l0, l1, l2, x2, w_out, nf, wg, wu, wd, npl, pg, p2, pproj, nfin)


def _rope_tab_kernel(pos4_ref, invf_ref, cos_ref, sin_ref, cos4_ref, sin4_ref, cos16_ref, sin16_ref):
    ang = pos4_ref[...].astype(F32) * invf_ref[...]
    lane = lax.broadcasted_iota(jnp.int32, ang.shape, 1)
    rotary = lane % HALF < ROPE_HALF
    cos_all = jnp.cos(ang)
    sin_all = jnp.sin(ang)
    tm = cos_ref.shape[0]
    for j in range(4):
        c, s = cos_all, sin_all
        if j:
            c = pltpu.roll(c, LANES - ROPE_HALF * j, 1)
            s = pltpu.roll(s, LANES - ROPE_HALF * j, 1)
        c = jnp.where(rotary, c, 1.0)
        s = jnp.where(rotary, jnp.where(lane < HALF, -s, s), 0.0)
        cos4_ref[j] = c
        sin4_ref[j] = s
        rows = pl.ds(j, tm // 4, stride=4)
        cos_ref[rows, :] = c
        sin_ref[rows, :] = s
    for r16 in range(16):
        rows = pl.ds(r16 // 4, tm // 16, stride=4)
        cos16_ref[r16] = cos4_ref[r16 % 4, rows, :]
        sin16_ref[r16] = sin4_ref[r16 % 4, rows, :]


def _rope_tables(pos4, invf_lane, tm, batch):
    T = pos4.shape[0] * 4
    S = T // batch
    row = pl.BlockSpec((tm, LANES), lambda i: (i, 0))
    res = lambda dil: jax.ShapeDtypeStruct((batch, dil, S // dil, LANES), F32)
    return pl.pallas_call(
        _rope_tab_kernel,
        grid=(T // tm,),
        in_specs=[pl.BlockSpec((tm // 4, LANES), lambda i: (i, 0)), _const_spec((1, LANES))],
        out_specs=[row, row] + [_residue_spec(4, tm, LANES, S)] * 2 + [_residue_spec(16, tm, LANES, S)] * 2,
        out_shape=[jax.ShapeDtypeStruct((T, LANES), F32)] * 2 + [res(4)] * 2 + [res(16)] * 2,
        compiler_params=_params(("parallel",)),
        name="rope_tab",
    )(pos4, invf_lane)


MXU_COLS = 256
LOG2E = 1.4426950408889634
LN2 = 0.6931471805599453


def _in_proj1_kernel(hn_ref, wqk_ref, wv_ref, cos_ref, sin_ref, out_ref):
    hn = hn_ref[...]
    q_scale = (A_DH ** -0.5) * LOG2E
    tables = ((cos_ref[...] * q_scale, sin_ref[...] * q_scale), (cos_ref[...], sin_ref[...]))
    for nb in range(3 * D_MODEL // MXU_COLS):
        kind = nb * MXU_COLS // D_MODEL
        w_ref, c0 = (wqk_ref, nb * MXU_COLS) if kind < 2 else (wv_ref, nb * MXU_COLS - 2 * D_MODEL)
        acc = _dot(hn, w_ref[:, c0:c0 + MXU_COLS])
        for half in range(MXU_COLS // A_DH):
            seg = acc[:, half * A_DH:(half + 1) * A_DH]
            if kind < 2:
                c, s = tables[kind]
                seg = seg * c + pltpu.roll(seg, A_DH // 2, 1) * s
            col = nb * MXU_COLS + half * A_DH
            out_ref[:, col:col + A_DH] = seg.astype(BF16)


def _in_proj1(hn, w_qk, w_v, g, cos_t, sin_t, tm):
    T = hn.shape[0]
    row = lambda n: pl.BlockSpec((tm, n), lambda i: (i, 0))
    group_cols = lambda n: pl.BlockSpec((D_MODEL, n), lambda i: (0, g), pipeline_mode=pl.Buffered(1))
    return pl.pallas_call(
        _in_proj1_kernel,
        grid=(T // tm,),
        in_specs=[row(D_MODEL), group_cols(2 * D_MODEL), group_cols(D_MODEL), row(LANES), row(LANES)],
        out_specs=row(3 * D_MODEL),
        out_shape=jax.ShapeDtypeStruct((T, 3 * D_MODEL), BF16),
        compiler_params=_params(("parallel",)),
        name=f"in_proj1_g{g}",
    )(hn, w_qk, w_v, cos_t, sin_t)


ATT_SUB = 128


def _attn_kernel(q_ref, kp_ref, kc_ref, kn_ref, vp_ref, vc_ref, vn_ref, o_ref, lse_ref, kall, vall,
                 *, blkq, n_keys):
    i = pl.program_id(2)
    kall[0:RADIUS, :] = kp_ref[...]
    kall[RADIUS:RADIUS + blkq, :] = kc_ref[...]
    kall[RADIUS + blkq:, :] = kn_ref[...]
    vall[0:RADIUS, :] = vp_ref[...]
    vall[RADIUS:RADIUS + blkq, :] = vc_ref[...]
    vall[RADIUS + blkq:, :] = vn_ref[...]
    nk = ATT_SUB + 2 * RADIUS
    r = lax.broadcasted_iota(jnp.int32, (ATT_SUB, nk), 0)
    c = lax.broadcasted_iota(jnp.int32, (ATT_SUB, nk), 1)
    band = jnp.abs(c - RADIUS - r) <= RADIUS
    lane = lax.broadcasted_iota(jnp.int32, (ATT_SUB, LANES), 1)
    ones = jnp.ones((nk, A_DH), BF16)
    for a in range(blkq // ATT_SUB):
        key0 = i * blkq + a * ATT_SUB - RADIUS
        valid = band & (c + key0 >= 0) & (c + key0 < n_keys)
        rows = slice(a * ATT_SUB, (a + 1) * ATT_SUB)
        krows = slice(a * ATT_SUB, a * ATT_SUB + nk)
        lse_tile = jnp.zeros((ATT_SUB, LANES), F32)
        for h in range(A_HEADS):
            sl = slice(h * A_DH, (h + 1) * A_DH)
            s = jnp.where(valid, _dot_nt(q_ref[rows, sl], kall[krows, sl]), NEG_INF)
            m = jnp.max(s, axis=1, keepdims=True)
            p = jnp.exp2(s - m).astype(BF16)
            pv = _dot(p, jnp.concatenate([vall[krows, sl], ones], axis=1))
            den = pv[:, A_DH:]
            o_ref[rows, sl] = (pv[:, :A_DH] / den).astype(BF16)
            lse_tile = jnp.where(lane == h, (m + jnp.log2(den)) * LN2, lse_tile)
        lse_ref[rows, :] = lse_tile


def _attention(proj, blkq):
    B, dil, U, _ = proj.shape
    blkq = min(blkq, U)
    hb = blkq // RADIUS
    nhalo = U // RADIUS

    def cur(j):
        return pl.BlockSpec((None, None, blkq, D_MODEL), lambda b, r, i: (b, r, i, j))

    def prev(j):
        return pl.BlockSpec((None, None, RADIUS, D_MODEL),
                            lambda b, r, i: (b, r, jnp.maximum(i * hb - 1, 0), j))

    def nxt(j):
        return pl.BlockSpec((None, None, RADIUS, D_MODEL),
                            lambda b, r, i: (b, r, jnp.minimum((i + 1) * hb, nhalo - 1), j))

    return pl.pallas_call(
        functools.partial(_attn_kernel, blkq=blkq, n_keys=U),
        grid=(B, dil, U // blkq),
        in_specs=[cur(0), prev(1), cur(1), nxt(1), prev(2), cur(2), nxt(2)],
        out_specs=[pl.BlockSpec((None, None, blkq, D_MODEL), lambda b, r, i: (b, r, i, 0)),
                   pl.BlockSpec((None, None, blkq, LANES), lambda b, r, i: (b, r, i, 0))],
        out_shape=[jax.ShapeDtypeStruct((B, dil, U, D_MODEL), BF16),
                   jax.ShapeDtypeStruct((B, dil, U, LANES), F32)],
        scratch_shapes=[pltpu.VMEM((blkq + 2 * RADIUS, D_MODEL), BF16)] * 2,
        compiler_params=_params(("parallel", "parallel", "parallel")),
        name=f"attn_d{dil}",
    )(proj, proj, proj, proj, proj, proj, proj)


def kernel(x, p, positions, norm_mix, a_w_in, a_gate_bias, a_head_norm, a_w_out, b_w_in, b_w_out,
           norm_ffn, w_gate, w_up, w_down, norm_ple, ple_gate, ple_proj, final_norm):
    B, S, _ = x.shape
    T = B * S
    bf = lambda w: w.astype(BF16)
    vec = lambda w: w.reshape(1, -1).astype(F32)
    x2 = x.reshape(T, D_MODEL)
    nc = S // M_CHUNK

    w_in = bf(a_w_in)
    q, ktd, v, o, feat, rfeat = _in_proj0(x2, vec(norm_mix[0]), w_in, w_in[0, :, M_QK_ALL:2 * M_QK_ALL].T,
                                          vec(a_gate_bias[0]), tm=ROWS_IN_PROJ0)
    hf, hb = _mlstm(q.reshape(B, S, -1), ktd.reshape(B, nc, M_QK_ALL, LANES), v.reshape(B, S, -1),
                    feat.reshape(N_FEAT, B, S, N_GATE), rfeat.reshape(B, S // LANES, N_ROWF, LANES),
                    tb=ROWS_MLSTM)
    wg_all, wu_all, wd_all, pg_all, pp_all = bf(w_gate), bf(w_up), bf(w_down), bf(ple_gate), bf(ple_proj)
    p_all = p.reshape(p.shape[0], T, PLE_DIM)
    h, *hns = _tail0(hf.reshape(T, -1), hb.reshape(T, -1), o, vec(a_head_norm[0]), x2, bf(a_w_out),
                     vec(norm_ffn[0]), wg_all, wu_all, wd_all, vec(norm_ple[0]), pg_all, p_all, pp_all,
                     vec(norm_mix[1]), tm=ROWS_TAIL, batch=B)

    inv_freq = ROPE_THETA ** (-jnp.arange(0, ROPE_DIM, 2, dtype=F32) / ROPE_DIM)
    invf_lane = jnp.tile(inv_freq, LANES // ROPE_HALF).reshape(1, LANES)
    pos4 = jnp.tile(jnp.repeat(positions.reshape(T // 4, 4), ROPE_HALF, axis=1), (1, 2))
    tabs = _rope_tables(pos4, invf_lane, tm=ROWS_ROPE, batch=B)
    dim_order = jnp.concatenate([jnp.arange(0, ROPE_HALF), jnp.arange(ROPE_DIM, ROPE_DIM + 48),
                                 jnp.arange(ROPE_HALF, ROPE_DIM), jnp.arange(ROPE_DIM + 48, A_DH)])
    reorder = (jnp.arange(A_DH)[:, None] == dim_order[None, :]).astype(BF16)
    w1 = bf(b_w_in[0]).reshape(D_MODEL, N_GROUPS, 3, A_HEADS * A_DH)
    w1_qk = jnp.dot(w1[:, :, :2].reshape(-1, A_DH), reorder, preferred_element_type=F32).astype(BF16)
    w1_qk = w1_qk.reshape(D_MODEL, N_GROUPS * 2 * A_HEADS * A_DH)
    w1_v = w1[:, :, 2].reshape(D_MODEL, N_GROUPS * A_HEADS * A_DH)
    o_g, l_g = [], []
    for g, (_, dil) in enumerate(DILATED_GROUPS):
        proj = _in_proj1(hns[g].reshape(T, D_MODEL), w1_qk, w1_v, g, tabs[2 * g].reshape(T, LANES),
                         tabs[2 * g + 1].reshape(T, LANES), tm=ROWS_IN_PROJ1)
        og, lg = _attention(proj.reshape(B, dil, S // dil, 3 * D_MODEL), blkq=ROWS_ATTN)
        o_g.append(og)
        l_g.append(lg)
    out = _tail1(o_g[0].reshape(T, D_MODEL), o_g[1], o_g[2], l_g[0].reshape(T, LANES), l_g[1], l_g[2], h,
                 bf(b_w_out), vec(norm_ffn[1]), wg_all, wu_all, wd_all, vec(norm_ple[1]), pg_all, p_all, pp_all,
                 vec(final_norm), tm=ROWS_TAIL, batch=B)
    return out.reshape(B, S, D_MODEL)
```

```python
import functools

import jax
import jax.numpy as jnp
from jax import lax
from jax.experimental import pallas as pl
from jax.experimental.pallas import tpu as pltpu

F32 = jnp.float32
BF16 = jnp.bfloat16

D_MODEL = 1024
LANES = 128
EPS = 1e-6

M_HEADS = 8
M_QK = 64
M_V = 128
M_CHUNK = 64
M_QK_ALL = M_HEADS * M_QK

A_HEADS = 8
A_DH = 128
RADIUS = 64
DILATED_GROUPS = ((128, 1), (512, 4), (2048, 16))
N_GROUPS = 3
ROPE_DIM = 32
ROPE_HALF = 16
ROPE_THETA = 500000.0
NEG_INF = -1e30

FFN_HIDDEN = 2816
PLE_DIM = 256

VMEM_LIMIT = 60 * 1024 * 1024

ROWS_IN_PROJ0 = 1024
ROWS_MLSTM = 1024
ROWS_TAIL = 512
ROWS_ROPE = 1024
ROWS_IN_PROJ1 = 2048
ROWS_ATTN = 1024


def _dot(a, b):
    return jnp.dot(a, b, preferred_element_type=F32)


def _dot_nt(a, b):
    return lax.dot_general(a, b, (((1,), (1,)), ((), ())), preferred_element_type=F32)


def _rms(x, w):
    ms = jnp.mean(x * x, axis=-1, keepdims=True)
    return x * lax.rsqrt(ms + EPS) * w


def _const_spec(shape):
    nd = len(shape)
    return pl.BlockSpec(shape, lambda *_: (0,) * nd, pipeline_mode=pl.Buffered(1))


def _params(sem):
    return pltpu.CompilerParams(dimension_semantics=sem, vmem_limit_bytes=VMEM_LIMIT)


N_DIRS = 2
N_GATE = N_DIRS * M_HEADS
N_FEAT = 4


def _chunk_scan(x, op, fill, reverse):
    n = x.shape[0]
    pos = lax.broadcasted_iota(jnp.int32, x.shape, 0) % M_CHUNK
    shift = 1
    while shift < M_CHUNK:
        if reverse:
            shifted, ok = pltpu.roll(x, n - shift, 0), pos < M_CHUNK - shift
        else:
            shifted, ok = pltpu.roll(x, shift, 0), pos >= shift
        x = op(x, jnp.where(ok, shifted, fill))
        shift *= 2
    return x


W0_Q, W0_V, W0_O, W0_G = 0, 2 * M_QK_ALL, 2 * M_QK_ALL + D_MODEL, 2 * M_QK_ALL + 2 * D_MODEL


def _chunk_first(x):
    pos = lax.broadcasted_iota(jnp.int32, x.shape, 0) % M_CHUNK
    shift = 1
    while shift < M_CHUNK:
        x = jnp.where(pos >= shift, pltpu.roll(x, shift, 0), x)
        shift *= 2
    return x


def _in_proj0_kernel(x_ref, nw_ref, w_ref, wkt_ref, gb_ref, q_ref, ktd_ref, v_ref, o_ref, f_ref, rf_ref):
    tm = x_ref.shape[0]
    hn = _rms(x_ref[...], nw_ref[...]).astype(BF16)
    gates = _dot(hn, w_ref[:, W0_G:]) + gb_ref[...]
    ig = gates[:, :N_GATE]
    fg = gates[:, N_GATE:]
    lf = jnp.minimum(fg, 0.0) - jnp.log1p(jnp.exp(-jnp.abs(fg)))
    is_fwd = lax.broadcasted_iota(jnp.int32, lf.shape, 1) < M_HEADS
    q_ref[...] = (_dot(hn, w_ref[:, W0_Q:W0_Q + M_QK_ALL]) * (M_QK ** -0.5)).astype(BF16)
    pre = _chunk_scan(lf, jnp.add, 0.0, False)
    suf = _chunk_scan(lf, jnp.add, 0.0, True)
    b = jnp.where(is_fwd, pre, suf)
    g_tok = pre + suf - lf
    w = ig - b
    v_ref[...] = _dot(hn, w_ref[:, W0_V:W0_O]).astype(BF16)
    wpre = _chunk_scan(w, jnp.maximum, -jnp.inf, False)
    wsuf = _chunk_scan(w, jnp.maximum, -jnp.inf, True)
    wmax = jnp.maximum(wpre, wsuf)
    g = _chunk_first(g_tok)
    for k, feat in enumerate((b, jnp.where(is_fwd, wpre, wsuf), g, wmax)):
        f_ref[k] = feat
    o_ref[...] = _dot(hn, w_ref[:, W0_O:W0_G]).astype(BF16)
    for k, feat in enumerate((w, g + w, g, wmax)):
        feat_t = feat.T
        for t in range(tm // LANES):
            rf_ref[t, k * N_GATE:(k + 1) * N_GATE, :] = feat_t[:, t * LANES:(t + 1) * LANES]
    kt = _dot_nt(wkt_ref[...], hn)
    low = lax.broadcasted_iota(jnp.int32, (M_QK_ALL, LANES), 1) < HALF
    for c in range(0, tm // M_CHUNK, 2):
        both = kt[:, c * M_CHUNK:(c + 2) * M_CHUNK]
        swapped = pltpu.roll(both, HALF, 1)
        ktd_ref[c] = jnp.where(low, both, swapped).astype(BF16)
        ktd_ref[c + 1] = jnp.where(low, swapped, both).astype(BF16)


def _in_proj0(x2, nw, w_in, w_kt, gate_bias, tm):
    T = x2.shape[0]
    row = lambda n: pl.BlockSpec((tm, n), lambda i: (i, 0))
    nch = tm // M_CHUNK
    return pl.pallas_call(
        _in_proj0_kernel,
        grid=(T // tm,),
        in_specs=[row(D_MODEL), _const_spec((1, D_MODEL)), _layer_spec(w_in.shape[1:], 0),
                  _const_spec(w_kt.shape), _const_spec((1, 2 * N_GATE))],
        out_specs=[row(M_QK_ALL), pl.BlockSpec((nch, M_QK_ALL, LANES), lambda i: (i, 0, 0)),
                   row(D_MODEL), row(D_MODEL), pl.BlockSpec((N_FEAT, tm, N_GATE), lambda i: (0, i, 0)),
                   pl.BlockSpec((tm // LANES, N_FEAT * N_GATE, LANES), lambda i: (i, 0, 0))],
        out_shape=[jax.ShapeDtypeStruct((T, M_QK_ALL), BF16),
                   jax.ShapeDtypeStruct((T // M_CHUNK, M_QK_ALL, LANES), BF16),
                   jax.ShapeDtypeStruct((T, D_MODEL), BF16), jax.ShapeDtypeStruct((T, D_MODEL), BF16),
                   jax.ShapeDtypeStruct((N_FEAT, T, N_GATE), F32),
                   jax.ShapeDtypeStruct((T // LANES, N_FEAT * N_GATE, LANES), F32)],
        compiler_params=_params(("parallel",)),
        name="in_proj0",
    )(x2, nw, w_in, w_kt, gate_bias)


N_PAIRS = M_HEADS // 2
HALF = LANES // 2
PAIR_W = 2 * 2 * M_V
ROW_W, ROW_A, ROW_G, ROW_WMAX, N_ROWF = 0, N_GATE, 2 * N_GATE, 3 * N_GATE, 4 * N_GATE


def _lane_pair(x, col):
    lane = lax.broadcasted_iota(jnp.int32, (x.shape[0], LANES), 1)
    return jnp.where(lane < HALF, x[:, col:col + 1], x[:, col + 1:col + 2])


def _mlstm_chunk(q_ref, kt_ref, v_ref, cf_ref, rf_ref, h_ref, s_ref, sb_ref, mc_ref, mr_ref, sub, fwd):
    L = M_CHUNK
    r0 = pl.multiple_of(sub * L, L)
    rows = pl.ds(r0, L)
    lo = 0 if fwd else M_HEADS
    b_c = cf_ref[0, rows, :]
    m_r = mr_ref[...]
    mj = b_c + jnp.maximum(m_r, cf_ref[1, rows, :])
    u = b_c - mj
    iw = jnp.exp(b_c + m_r - mj)
    en = jnp.exp(-mj)
    mr_ref[...] = cf_ref[2, pl.ds(r0, 1), :] + jnp.maximum(m_r, cf_ref[3, pl.ds(r0, 1), :])
    low_half = lax.broadcasted_iota(jnp.int32, (1, LANES), 1) < HALF
    tile = rf_ref[sub // 2]
    swapped = pltpu.roll(tile, HALF, 1)
    first = sub % 2 == 0
    own_lo = jnp.where(first, tile, swapped)
    own_hi = jnp.where(first, swapped, tile)
    both = jnp.where(low_half, own_lo, own_hi)
    m_c = mc_ref[...]
    g = both[ROW_G + lo:ROW_G + lo + M_HEADS]
    m_new = g + jnp.maximum(m_c, both[ROW_WMAX + lo:ROW_WMAX + lo + M_HEADS])
    decay = jnp.exp(g + m_c - m_new)
    mc_ref[...] = m_new

    li = lax.broadcasted_iota(jnp.int32, (L, LANES), 0)
    si = lax.broadcasted_iota(jnp.int32, (L, LANES), 1) % HALF
    mask = (si <= li) if fwd else (si >= li)
    same_head = (lax.broadcasted_iota(jnp.int32, (LANES, LANES), 0) // HALF
                 == lax.broadcasted_iota(jnp.int32, (LANES, LANES), 1) // HALF)
    ones = jnp.ones((L, M_V), BF16)
    zeros = jnp.zeros((L, M_V), BF16)
    yield
    q2s, kbds, scores = [], [], []
    for p in range(N_PAIRS):
        q2s.append(q_ref[rows, p * LANES:(p + 1) * LANES])
        kbds.append(jnp.where(same_head, kt_ref[sub, p * LANES:(p + 1) * LANES, :], jnp.zeros((), BF16)))
        scores.append(_dot(q2s[p], kbds[p]))
    yield
    mains, inters = [], []
    for p in range(N_PAIRS):
        def pair_row(base):
            r = base + lo + 2 * p
            return jnp.where(low_half, own_lo[r:r + 1], own_hi[r + 1:r + 2])

        w_pr = pair_row(ROW_W)
        m_new_pr = jnp.where(low_half, m_new[2 * p:2 * p + 1], m_new[2 * p + 1:2 * p + 2])
        k_scale = jnp.exp(pair_row(ROW_A) - m_new_pr)
        dw = jnp.exp(jnp.where(mask, _lane_pair(u, lo + 2 * p) + w_pr, -jnp.inf))
        sm = (scores[p] * dw).astype(BF16)
        kw = (kbds[p].astype(F32) * k_scale).astype(BF16)
        qi = (q2s[p].astype(F32) * _lane_pair(iw, lo + 2 * p)).astype(BF16)
        pad = jnp.concatenate([zeros, zeros], axis=1)
        pair = []
        for j in range(2):
            h = 2 * p + j
            v_ext = jnp.concatenate([v_ref[rows, h * M_V:(h + 1) * M_V], ones], axis=1)
            rhs = jnp.concatenate([v_ext, pad] if j == 0 else [pad, v_ext], axis=0)
            pair.append(_dot(jnp.concatenate([sm, kw[j * M_QK:(j + 1) * M_QK]], axis=0), rhs))
        mains.append(pair)
        inters.append(_dot(qi, sb_ref[p]))
    yield
    outs = []
    for p in range(N_PAIRS):
        for j in range(2):
            h = 2 * p + j
            out = mains[p][j][:L] + inters[p][:, j * 2 * M_V:(j + 1) * 2 * M_V]
            outs.append(out[:, :M_V] / jnp.maximum(jnp.abs(out[:, M_V:]), en[:, lo + h:lo + h + 1]))
            blk = (p, slice(j * M_QK, (j + 1) * M_QK), slice(j * 2 * M_V, (j + 1) * 2 * M_V))
            dec = jnp.concatenate([decay[h:h + 1]] * 2, axis=1)
            s_new = dec * s_ref[blk] + mains[p][j][L:]
            s_ref[blk] = s_new
            sb_ref[blk] = s_new.astype(BF16)
    h_ref[rows, :] = jnp.concatenate(outs, axis=1).astype(h_ref.dtype)
    yield


N_MLSTM_STAGES = 4


def _mlstm_kernel(qf, ktf, vf, cff, rff, qb, ktb, vb, cfb, rfb, hf_ref, hb_ref,
                  sf, sbf, mcf, mrf, sb, sbb, mcb, mrb, *, nsub):
    @pl.when(pl.program_id(1) == 0)
    def _():
        for ref in (sf, sbf, mcf, mrf, sb, sbb, mcb, mrb):
            ref[...] = jnp.zeros_like(ref)

    def body(j, carry):
        chunks = [_mlstm_chunk(qf, ktf, vf, cff, rff, hf_ref, sf, sbf, mcf, mrf, j, True),
                  _mlstm_chunk(qb, ktb, vb, cfb, rfb, hb_ref, sb, sbb, mcb, mrb, nsub - 1 - j, False)]
        for _ in range(N_MLSTM_STAGES):
            for chunk in chunks:
                next(chunk)
        return carry

    lax.fori_loop(0, nsub, body, 0)


def _mlstm(q, ktd, v, cfeat, rfeat, tb):
    B, S, _ = q.shape
    n = S // tb
    nsub = tb // M_CHUNK
    fw = lambda b, c: (b, c, 0)
    bw = lambda b, c: (b, n - 1 - c, 0)
    fw4 = lambda b, c: (b, c, 0, 0)
    bw4 = lambda b, c: (b, n - 1 - c, 0, 0)

    def specs(im, im4):
        return [pl.BlockSpec((None, tb, M_QK_ALL), im), pl.BlockSpec((None, nsub, M_QK_ALL, LANES), im4),
                pl.BlockSpec((None, tb, D_MODEL), im),
                pl.BlockSpec((4, None, tb, N_GATE), lambda b, c: (0,) + im(b, c)),
                pl.BlockSpec((None, nsub // 2, N_ROWF, LANES), im4)]

    state = [pltpu.VMEM((N_PAIRS, LANES, PAIR_W), F32), pltpu.VMEM((N_PAIRS, LANES, PAIR_W), BF16),
             pltpu.VMEM((M_HEADS, LANES), F32), pltpu.VMEM((1, N_GATE), F32)]
    return pl.pallas_call(
        functools.partial(_mlstm_kernel, nsub=nsub),
        grid=(B, n),
        in_specs=specs(fw, fw4) + specs(bw, bw4),
        out_specs=[pl.BlockSpec((None, tb, D_MODEL), fw), pl.BlockSpec((None, tb, D_MODEL), bw)],
        out_shape=[jax.ShapeDtypeStruct((B, S, D_MODEL), BF16)] * 2,
        scratch_shapes=state + state,
        compiler_params=_params(("parallel", "arbitrary")),
        name="mlstm",
    )(q, ktd, v, cfeat, rfeat, q, ktd, v, cfeat, rfeat)


def _sigmoid(x):
    return 0.5 * jnp.tanh(0.5 * x) + 0.5


def _tail_common(mix_pair, x_ref, w_out_ref, nf_ref, wg_ref, wu_ref, wd_ref, npl_ref, pg_ref, p_ref, pp_ref):
    emb = _dot(p_ref[...].astype(BF16), pp_ref[...])
    h1 = x_ref[...]
    for i in range(D_MODEL // MXU_COLS):
        h1 = h1 + _dot(mix_pair(i), w_out_ref[i * MXU_COLS:(i + 1) * MXU_COLS, :])
    hn = _rms(h1, nf_ref[...]).astype(BF16)
    gate = _dot(hn, wg_ref[...])
    act = (gate * _sigmoid(gate) * _dot(hn, wu_ref[...])).astype(BF16)
    h2 = h1 + _dot(act, wd_ref[...])
    pgate = _sigmoid(_dot(_rms(h2, npl_ref[...]).astype(BF16), pg_ref[...]))
    return h2 + pgate * emb


N_SLABS = D_MODEL // LANES


def _tail0_kernel(hf_ref, hb_ref, o_ref, hnorm_ref, x_ref, w_out_ref, nf_ref, wg_ref, wu_ref, wd_ref,
                  npl_ref, pg_ref, p_ref, pp_ref, nnext_ref, out_ref, hn_ref, hn4_ref, hn16_ref, slab_ref,
                  slab4_ref):
    def head(h):
        sl = slice(h * M_V, (h + 1) * M_V)
        hh = hf_ref[:, sl].astype(F32) + hb_ref[:, sl].astype(F32)
        hh = hh * lax.rsqrt(jnp.mean(hh * hh, axis=-1, keepdims=True) + EPS) * hnorm_ref[:, sl]
        return (_sigmoid(o_ref[:, sl].astype(F32)) * hh).astype(BF16)

    h3 = _tail_common(lambda i: jnp.concatenate([head(2 * i), head(2 * i + 1)], axis=1), x_ref, w_out_ref,
                      nf_ref, wg_ref, wu_ref, wd_ref, npl_ref, pg_ref, p_ref, pp_ref)
    out_ref[...] = h3
    hn = _rms(h3, nnext_ref[...])
    hn_ref[...] = hn.astype(BF16)
    tm = hn.shape[0]
    q4 = tm // 4
    for s in range(N_SLABS):
        slab_ref[s] = hn[:, s * LANES:(s + 1) * LANES]
    for r4 in range(4):
        parts = [slab_ref[s, pl.ds(r4, q4, stride=4), :] for s in range(N_SLABS)]
        hn4_ref[r4] = jnp.concatenate(parts, axis=1).astype(BF16)
        for s in range(N_SLABS):
            slab4_ref[s, r4 * q4:(r4 + 1) * q4, :] = parts[s]
    for r16 in range(16):
        start = (r16 % 4) * q4 + r16 // 4
        hn16_ref[r16] = jnp.concatenate([slab4_ref[s, pl.ds(start, tm // 16, stride=4), :]
                                         for s in range(N_SLABS)], axis=1).astype(BF16)


def _tail1_kernel(o0_ref, o1_ref, o2_ref, l0_ref, l1_ref, l2_ref, x_ref, w_out_ref, nf_ref, wg_ref, wu_ref,
                  wd_ref, npl_ref, pg_ref, p_ref, pp_ref, nfin_ref, out_ref, slab1_ref, slab2_ref,
                  ls1_ref, ls2_ref, slabt_ref, lst_ref):
    tm = o0_ref.shape[0]
    q4 = tm // 4
    for r16 in range(16):
        rows = pl.ds((r16 % 4) * q4 + r16 // 4, tm // 16, stride=4)
        lst_ref[rows, :] = l2_ref[r16]
        o_r = o2_ref[r16].astype(F32)
        for s in range(N_SLABS):
            slabt_ref[s, rows, :] = o_r[:, s * LANES:(s + 1) * LANES]
    for r4 in range(4):
        rows = pl.ds(r4, q4, stride=4)
        block = slice(r4 * q4, (r4 + 1) * q4)
        ls1_ref[rows, :] = l1_ref[r4]
        ls2_ref[rows, :] = lst_ref[block, :]
        o_r = o1_ref[r4].astype(F32)
        for s in range(N_SLABS):
            slab1_ref[s, rows, :] = o_r[:, s * LANES:(s + 1) * LANES]
            slab2_ref[s, rows, :] = slabt_ref[s, block, :]
    l0 = l0_ref[...]
    l1 = ls1_ref[...]
    l2 = ls2_ref[...]
    mx = jnp.maximum(jnp.maximum(l0, l1), l2)
    e0 = jnp.exp(l0 - mx)
    e1 = jnp.exp(l1 - mx)
    e2 = jnp.exp(l2 - mx)
    inv = 1.0 / (e0 + e1 + e2)
    w1 = e1 * inv
    w2 = e2 * inv

    def head(h):
        o0 = o0_ref[:, h * A_DH:(h + 1) * A_DH].astype(F32)
        return (o0 + w1[:, h:h + 1] * (slab1_ref[h] - o0) + w2[:, h:h + 1] * (slab2_ref[h] - o0)).astype(BF16)

    h3 = _tail_common(lambda i: jnp.concatenate([head(2 * i), head(2 * i + 1)], axis=1), x_ref, w_out_ref,
                      nf_ref, wg_ref, wu_ref, wd_ref, npl_ref, pg_ref, p_ref, pp_ref)
    out_ref[...] = _rms(h3, nfin_ref[...])


def _layer_spec(shape, layer):
    return pl.BlockSpec((None,) + shape, lambda *_: (layer, 0, 0), pipeline_mode=pl.Buffered(1))


def _tail_weight_specs(layer):
    return [_layer_spec((D_MODEL, D_MODEL), 0), _const_spec((1, D_MODEL)),
            _layer_spec((D_MODEL, FFN_HIDDEN), layer), _layer_spec((D_MODEL, FFN_HIDDEN), layer),
            _layer_spec((FFN_HIDDEN, D_MODEL), layer), _const_spec((1, D_MODEL)),
            _layer_spec((D_MODEL, D_MODEL), layer)]


def _residue_spec(dil, tm, n, seq):
    nb = seq // tm
    return pl.BlockSpec((None, dil, tm // dil, n), lambda i: (i // nb, 0, i % nb, 0))


def _tail0(hf, hb, o, hnorm, x2, w_out, nf, wg, wu, wd, npl, pg, p2, pproj, nnext, tm, batch):
    T = x2.shape[0]
    S = T // batch
    row = lambda n: pl.BlockSpec((tm, n), lambda i: (i, 0))
    return pl.pallas_call(
        _tail0_kernel,
        grid=(T // tm,),
        in_specs=[row(D_MODEL), row(D_MODEL), row(D_MODEL), _const_spec((1, D_MODEL)), row(D_MODEL)]
        + _tail_weight_specs(0) + [pl.BlockSpec((None, tm, PLE_DIM), lambda i: (0, i, 0)),
                                   _layer_spec((PLE_DIM, D_MODEL), 0), _const_spec((1, D_MODEL))],
        out_specs=[row(D_MODEL), row(D_MODEL), _residue_spec(4, tm, D_MODEL, S),
                   _residue_spec(16, tm, D_MODEL, S)],
        out_shape=[jax.ShapeDtypeStruct((T, D_MODEL), F32), jax.ShapeDtypeStruct((T, D_MODEL), BF16),
                   jax.ShapeDtypeStruct((batch, 4, S // 4, D_MODEL), BF16),
                   jax.ShapeDtypeStruct((batch, 16, S // 16, D_MODEL), BF16)],
        scratch_shapes=[pltpu.VMEM((N_SLABS, tm, LANES), F32)] * 2,
        compiler_params=_params(("parallel",)),
        name="tail0",
    )(hf, hb, o, hnorm, x2, w_out, nf, wg, wu, wd, npl, pg, p2, pproj, nnext)


def _tail1(o0, o1, o2, l0, l1, l2, x2, w_out, nf, wg, wu, wd, npl, pg, p2, pproj, nfin, tm, batch):
    T = x2.shape[0]
    S = T // batch
    row = lambda n: pl.BlockSpec((tm, n), lambda i: (i, 0))
    return pl.pallas_call(
        _tail1_kernel,
        grid=(T // tm,),
        in_specs=[row(D_MODEL), _residue_spec(4, tm, D_MODEL, S), _residue_spec(16, tm, D_MODEL, S),
                  row(LANES), _residue_spec(4, tm, LANES, S), _residue_spec(16, tm, LANES, S), row(D_MODEL)]
        + _tail_weight_specs(1) + [pl.BlockSpec((None, tm, PLE_DIM), lambda i: (1, i, 0)),
                                   _layer_spec((PLE_DIM, D_MODEL), 1), _const_spec((1, D_MODEL))],
        out_specs=row(D_MODEL),
        out_shape=jax.ShapeDtypeStruct((T, D_MODEL), F32),
        scratch_shapes=[pltpu.VMEM((N_SLABS, tm, LANES), F32),
                        pltpu.VMEM((N_SLABS, tm, LANES), F32), pltpu.VMEM((tm, LANES), F32),
                        pltpu.VMEM((tm, LANES), F32), pltpu.VMEM((N_SLABS, tm, LANES), F32),
                        pltpu.VMEM((tm, LANES), F32)],
        compiler_params=_params(("parallel",)),
        name="tail1",
    )(o0, o1, o2, l0, l1, l2, x2, w_out, nf, wg, wu, wd, npl, pg, p2, pproj, nfin)


def _rope_tab_kernel(pos4_ref, invf_ref, cos_ref, sin_ref, cos4_ref, sin4_ref, cos16_ref, sin16_ref):
    ang = pos4_ref[...].astype(F32) * invf_ref[...]
    lane = lax.broadcasted_iota(jnp.int32, ang.shape, 1)
    rotary = lane % HALF < ROPE_HALF
    cos_all = jnp.cos(ang)
    sin_all = jnp.sin(ang)
    tm = cos_ref.shape[0]
    for j in range(4):
        c, s = cos_all, sin_all
        if j:
            c = pltpu.roll(c, LANES - ROPE_HALF * j, 1)
            s = pltpu.roll(s, LANES - ROPE_HALF * j, 1)
        c = jnp.where(rotary, c, 1.0)
        s = jnp.where(rotary, jnp.where(lane < HALF, -s, s), 0.0)
        cos4_ref[j] = c
        sin4_ref[j] = s
        rows = pl.ds(j, tm // 4, stride=4)
        cos_ref[rows, :] = c
        sin_ref[rows, :] = s
    for r16 in range(16):
        rows = pl.ds(r16 // 4, tm // 16, stride=4)
        cos16_ref[r16] = cos4_ref[r16 % 4, rows, :]
        sin16_ref[r16] = sin4_ref[r16 % 4, rows, :]


def _rope_tables(pos4, invf_lane, tm, batch):
    T = pos4.shape[0] * 4
    S = T // batch
    row = pl.BlockSpec((tm, LANES), lambda i: (i, 0))
    res = lambda dil: jax.ShapeDtypeStruct((batch, dil, S // dil, LANES), F32)
    return pl.pallas_call(
        _rope_tab_kernel,
        grid=(T // tm,),
        in_specs=[pl.BlockSpec((tm // 4, LANES), lambda i: (i, 0)), _const_spec((1, LANES))],
        out_specs=[row, row] + [_residue_spec(4, tm, LANES, S)] * 2 + [_residue_spec(16, tm, LANES, S)] * 2,
        out_shape=[jax.ShapeDtypeStruct((T, LANES), F32)] * 2 + [res(4)] * 2 + [res(16)] * 2,
        compiler_params=_params(("parallel",)),
        name="rope_tab",
    )(pos4, invf_lane)


MXU_COLS = 256
LOG2E = 1.4426950408889634
LN2 = 0.6931471805599453


def _in_proj1_kernel(hn_ref, w_ref, cos_ref, sin_ref, out_ref):
    hn = hn_ref[...]
    q_scale = (A_DH ** -0.5) * LOG2E
    tables = ((cos_ref[...] * q_scale, sin_ref[...] * q_scale), (cos_ref[...], sin_ref[...]))
    for nb in range(3 * D_MODEL // MXU_COLS):
        kind = nb * MXU_COLS // D_MODEL
        acc = _dot(hn, w_ref[:, nb * MXU_COLS:(nb + 1) * MXU_COLS])
        for half in range(MXU_COLS // A_DH):
            seg = acc[:, half * A_DH:(half + 1) * A_DH]
            if kind < 2:
                c, s = tables[kind]
                seg = seg * c + pltpu.roll(seg, A_DH // 2, 1) * s
            col = nb * MXU_COLS + half * A_DH
            out_ref[:, col:col + A_DH] = seg.astype(BF16)


def _in_proj1(hn, w, g, cos_t, sin_t, tm):
    T = hn.shape[0]
    row = lambda n: pl.BlockSpec((tm, n), lambda i: (i, 0))
    return pl.pallas_call(
        _in_proj1_kernel,
        grid=(T // tm,),
        in_specs=[row(D_MODEL),
                  pl.BlockSpec((D_MODEL, 3 * D_MODEL), lambda i: (0, g), pipeline_mode=pl.Buffered(1)),
                  row(LANES), row(LANES)],
        out_specs=row(3 * D_MODEL),
        out_shape=jax.ShapeDtypeStruct((T, 3 * D_MODEL), BF16),
        compiler_params=_params(("parallel",)),
        name=f"in_proj1_g{g}",
    )(hn, w, cos_t, sin_t)


ATT_SUB = 128


def _attn_kernel(q_ref, kp_ref, kc_ref, kn_ref, vp_ref, vc_ref, vn_ref, o_ref, lse_ref, kall, vall,
                 *, blkq, n_keys):
    i = pl.program_id(2)
    kall[0:RADIUS, :] = kp_ref[...]
    kall[RADIUS:RADIUS + blkq, :] = kc_ref[...]
    kall[RADIUS + blkq:, :] = kn_ref[...]
    vall[0:RADIUS, :] = vp_ref[...]
    vall[RADIUS:RADIUS + blkq, :] = vc_ref[...]
    vall[RADIUS + blkq:, :] = vn_ref[...]
    nk = ATT_SUB + 2 * RADIUS
    r = lax.broadcasted_iota(jnp.int32, (ATT_SUB, nk), 0)
    c = lax.broadcasted_iota(jnp.int32, (ATT_SUB, nk), 1)
    band = jnp.abs(c - RADIUS - r) <= RADIUS
    lane = lax.broadcasted_iota(jnp.int32, (ATT_SUB, LANES), 1)
    ones = jnp.ones((nk, A_DH), BF16)
    for a in range(blkq // ATT_SUB):
        key0 = i * blkq + a * ATT_SUB - RADIUS
        valid = band & (c + key0 >= 0) & (c + key0 < n_keys)
        rows = slice(a * ATT_SUB, (a + 1) * ATT_SUB)
        krows = slice(a * ATT_SUB, a * ATT_SUB + nk)
        lse_tile = jnp.zeros((ATT_SUB, LANES), F32)
        for h in range(A_HEADS):
            sl = slice(h * A_DH, (h + 1) * A_DH)
            s = jnp.where(valid, _dot_nt(q_ref[rows, sl], kall[krows, sl]), NEG_INF)
            m = jnp.max(s, axis=1, keepdims=True)
            p = jnp.exp2(s - m).astype(BF16)
            pv = _dot(p, jnp.concatenate([vall[krows, sl], ones], axis=1))
            den = pv[:, A_DH:]
            o_ref[rows, sl] = (pv[:, :A_DH] / den).astype(BF16)
            lse_tile = jnp.where(lane == h, (m + jnp.log2(den)) * LN2, lse_tile)
        lse_ref[rows, :] = lse_tile


def _attention(proj, blkq):
    B, dil, U, _ = proj.shape
    blkq = min(blkq, U)
    hb = blkq // RADIUS
    nhalo = U // RADIUS

    def cur(j):
        return pl.BlockSpec((None, None, blkq, D_MODEL), lambda b, r, i: (b, r, i, j))

    def prev(j):
        return pl.BlockSpec((None, None, RADIUS, D_MODEL),
                            lambda b, r, i: (b, r, jnp.maximum(i * hb - 1, 0), j))

    def nxt(j):
        return pl.BlockSpec((None, None, RADIUS, D_MODEL),
                            lambda b, r, i: (b, r, jnp.minimum((i + 1) * hb, nhalo - 1), j))

    return pl.pallas_call(
        functools.partial(_attn_kernel, blkq=blkq, n_keys=U),
        grid=(B, dil, U // blkq),
        in_specs=[cur(0), prev(1), cur(1), nxt(1), prev(2), cur(2), nxt(2)],
        out_specs=[pl.BlockSpec((None, None, blkq, D_MODEL), lambda b, r, i: (b, r, i, 0)),
                   pl.BlockSpec((None, None, blkq, LANES), lambda b, r, i: (b, r, i, 0))],
        out_shape=[jax.ShapeDtypeStruct((B, dil, U, D_MODEL), BF16),
                   jax.ShapeDtypeStruct((B, dil, U, LANES), F32)],
        scratch_shapes=[pltpu.VMEM((blkq + 2 * RADIUS, D_MODEL), BF16)] * 2,
        compiler_params=_params(("parallel", "parallel", "parallel")),
        name=f"attn_d{dil}",
    )(proj, proj, proj, proj, proj, proj, proj)


def kernel(x, p, positions, norm_mix, a_w_in, a_gate_bias, a_head_norm, a_w_out, b_w_in, b_w_out,
           norm_ffn, w_gate, w_up, w_down, norm_ple, ple_gate, ple_proj, final_norm):
    B, S, _ = x.shape
    T = B * S
    bf = lambda w: w.astype(BF16)
    vec = lambda w: w.reshape(1, -1).astype(F32)
    x2 = x.reshape(T, D_MODEL)
    nc = S // M_CHUNK

    w_in = bf(a_w_in)
    q, ktd, v, o, feat, rfeat = _in_proj0(x2, vec(norm_mix[0]), w_in, w_in[0, :, M_QK_ALL:2 * M_QK_ALL].T,
                                          vec(a_gate_bias[0]), tm=ROWS_IN_PROJ0)
    hf, hb = _mlstm(q.reshape(B, S, -1), ktd.reshape(B, nc, M_QK_ALL, LANES), v.reshape(B, S, -1),
                    feat.reshape(N_FEAT, B, S, N_GATE), rfeat.reshape(B, S // LANES, N_ROWF, LANES),
                    tb=ROWS_MLSTM)
    wg_all, wu_all, wd_all, pg_all, pp_all = bf(w_gate), bf(w_up), bf(w_down), bf(ple_gate), bf(ple_proj)
    p_all = p.reshape(p.shape[0], T, PLE_DIM)
    h, *hns = _tail0(hf.reshape(T, -1), hb.reshape(T, -1), o, vec(a_head_norm[0]), x2, bf(a_w_out),
                     vec(norm_ffn[0]), wg_all, wu_all, wd_all, vec(norm_ple[0]), pg_all, p_all, pp_all,
                     vec(norm_mix[1]), tm=ROWS_TAIL, batch=B)

    inv_freq = ROPE_THETA ** (-jnp.arange(0, ROPE_DIM, 2, dtype=F32) / ROPE_DIM)
    invf_lane = jnp.tile(inv_freq, LANES // ROPE_HALF).reshape(1, LANES)
    pos4 = jnp.tile(jnp.repeat(positions.reshape(T // 4, 4), ROPE_HALF, axis=1), (1, 2))
    tabs = _rope_tables(pos4, invf_lane, tm=ROWS_ROPE, batch=B)
    n_blocks = N_GROUPS * 3 * A_HEADS
    w1 = bf(b_w_in[0]).reshape(D_MODEL, n_blocks, A_DH)
    dim = jnp.arange(A_DH)
    moved = jnp.where(dim < HALF, jnp.roll(w1, -ROPE_HALF, axis=2), jnp.roll(w1, HALF - ROPE_HALF, axis=2))
    stays = (dim < ROPE_HALF) | (dim >= HALF + ROPE_HALF)
    is_v = (jnp.arange(n_blocks) // A_HEADS) % 3 == 2
    w1 = jnp.where(stays[None, None, :] | is_v[None, :, None], w1, moved).reshape(D_MODEL, -1)
    o_g, l_g = [], []
    for g, (_, dil) in enumerate(DILATED_GROUPS):
        proj = _in_proj1(hns[g].reshape(T, D_MODEL), w1, g, tabs[2 * g].reshape(T, LANES),
                         tabs[2 * g + 1].reshape(T, LANES), tm=ROWS_IN_PROJ1)
        og, lg = _attention(proj.reshape(B, dil, S // dil, 3 * D_MODEL), blkq=ROWS_ATTN)
        o_g.append(og)
        l_g.append(lg)
    out = _tail1(o_g[0].reshape(T, D_MODEL), o_g[1], o_g[2], l_g[0].reshape(T, LANES), l_g[1], l_g[2], h,
                 bf(b_w_out), vec(norm_ffn[1]), wg_all, wu_all, wd_all, vec(norm_ple[1]), pg_all, p_all, pp_all,
                 vec(final_norm), tm=ROWS_TAIL, batch=B)
    return out.reshape(B, S, D_MODEL)
```

```python
import functools

import jax
import jax.numpy as jnp
from jax import lax
from jax.experimental import pallas as pl
from jax.experimental.pallas import tpu as pltpu

F32 = jnp.float32
BF16 = jnp.bfloat16

D_MODEL = 1024
LANES = 128
EPS = 1e-6

M_HEADS = 8
M_QK = 64
M_V = 128
M_CHUNK = 64
M_QK_ALL = M_HEADS * M_QK

A_HEADS = 8
A_DH = 128
RADIUS = 64
DILATED_GROUPS = ((128, 1), (512, 4), (2048, 16))
N_GROUPS = 3
ROPE_DIM = 32
ROPE_HALF = 16
ROPE_THETA = 500000.0
NEG_INF = -1e30

FFN_HIDDEN = 2816
PLE_DIM = 256

VMEM_LIMIT = 60 * 1024 * 1024

ROWS_IN_PROJ0 = 1024
ROWS_MLSTM = 1024
ROWS_TAIL = 512
ROWS_ROPE = 1024
ROWS_IN_PROJ1 = 2048
ROWS_ATTN = 1024


def _dot(a, b):
    return jnp.dot(a, b, preferred_element_type=F32)


def _dot_nt(a, b):
    return lax.dot_general(a, b, (((1,), (1,)), ((), ())), preferred_element_type=F32)


def _rms(x, w):
    ms = jnp.mean(x * x, axis=-1, keepdims=True)
    return x * lax.rsqrt(ms + EPS) * w


def _const_spec(shape):
    nd = len(shape)
    return pl.BlockSpec(shape, lambda *_: (0,) * nd, pipeline_mode=pl.Buffered(1))


def _params(sem):
    return pltpu.CompilerParams(dimension_semantics=sem, vmem_limit_bytes=VMEM_LIMIT)


N_DIRS = 2
N_GATE = N_DIRS * M_HEADS
N_FEAT = 4


def _chunk_scan(x, op, fill, reverse):
    n = x.shape[0]
    pos = lax.broadcasted_iota(jnp.int32, x.shape, 0) % M_CHUNK
    shift = 1
    while shift < M_CHUNK:
        if reverse:
            shifted, ok = pltpu.roll(x, n - shift, 0), pos < M_CHUNK - shift
        else:
            shifted, ok = pltpu.roll(x, shift, 0), pos >= shift
        x = op(x, jnp.where(ok, shifted, fill))
        shift *= 2
    return x


W0_Q, W0_V, W0_O, W0_G = 0, 2 * M_QK_ALL, 2 * M_QK_ALL + D_MODEL, 2 * M_QK_ALL + 2 * D_MODEL


def _chunk_first(x):
    pos = lax.broadcasted_iota(jnp.int32, x.shape, 0) % M_CHUNK
    shift = 1
    while shift < M_CHUNK:
        x = jnp.where(pos >= shift, pltpu.roll(x, shift, 0), x)
        shift *= 2
    return x


def _in_proj0_kernel(x_ref, nw_ref, w_ref, wkt_ref, gb_ref, q_ref, ktd_ref, v_ref, o_ref, f_ref, rf_ref):
    tm = x_ref.shape[0]
    hn = _rms(x_ref[...], nw_ref[...]).astype(BF16)
    gates = _dot(hn, w_ref[:, W0_G:]) + gb_ref[...]
    ig = gates[:, :N_GATE]
    fg = gates[:, N_GATE:]
    lf = jnp.minimum(fg, 0.0) - jnp.log1p(jnp.exp(-jnp.abs(fg)))
    is_fwd = lax.broadcasted_iota(jnp.int32, lf.shape, 1) < M_HEADS
    q_ref[...] = (_dot(hn, w_ref[:, W0_Q:W0_Q + M_QK_ALL]) * (M_QK ** -0.5)).astype(BF16)
    pre = _chunk_scan(lf, jnp.add, 0.0, False)
    suf = _chunk_scan(lf, jnp.add, 0.0, True)
    b = jnp.where(is_fwd, pre, suf)
    g_tok = pre + suf - lf
    w = ig - b
    v_ref[...] = _dot(hn, w_ref[:, W0_V:W0_O]).astype(BF16)
    wpre = _chunk_scan(w, jnp.maximum, -jnp.inf, False)
    wsuf = _chunk_scan(w, jnp.maximum, -jnp.inf, True)
    wmax = jnp.maximum(wpre, wsuf)
    g = _chunk_first(g_tok)
    for k, feat in enumerate((b, jnp.where(is_fwd, wpre, wsuf), g, wmax)):
        f_ref[k] = feat
    o_ref[...] = _dot(hn, w_ref[:, W0_O:W0_G]).astype(BF16)
    for k, feat in enumerate((w, g + w, g, wmax)):
        feat_t = feat.T
        for t in range(tm // LANES):
            rf_ref[t, k * N_GATE:(k + 1) * N_GATE, :] = feat_t[:, t * LANES:(t + 1) * LANES]
    kt = _dot_nt(wkt_ref[...], hn)
    low = lax.broadcasted_iota(jnp.int32, (M_QK_ALL, LANES), 1) < HALF
    for c in range(0, tm // M_CHUNK, 2):
        both = kt[:, c * M_CHUNK:(c + 2) * M_CHUNK]
        swapped = pltpu.roll(both, HALF, 1)
        ktd_ref[c] = jnp.where(low, both, swapped).astype(BF16)
        ktd_ref[c + 1] = jnp.where(low, swapped, both).astype(BF16)


def _in_proj0(x2, nw, w_in, w_kt, gate_bias, tm):
    T = x2.shape[0]
    row = lambda n: pl.BlockSpec((tm, n), lambda i: (i, 0))
    nch = tm // M_CHUNK
    return pl.pallas_call(
        _in_proj0_kernel,
        grid=(T // tm,),
        in_specs=[row(D_MODEL), _const_spec((1, D_MODEL)), _layer_spec(w_in.shape[1:], 0),
                  _const_spec(w_kt.shape), _const_spec((1, 2 * N_GATE))],
        out_specs=[row(M_QK_ALL), pl.BlockSpec((nch, M_QK_ALL, LANES), lambda i: (i, 0, 0)),
                   row(D_MODEL), row(D_MODEL), pl.BlockSpec((N_FEAT, tm, N_GATE), lambda i: (0, i, 0)),
                   pl.BlockSpec((tm // LANES, N_FEAT * N_GATE, LANES), lambda i: (i, 0, 0))],
        out_shape=[jax.ShapeDtypeStruct((T, M_QK_ALL), BF16),
                   jax.ShapeDtypeStruct((T // M_CHUNK, M_QK_ALL, LANES), BF16),
                   jax.ShapeDtypeStruct((T, D_MODEL), BF16), jax.ShapeDtypeStruct((T, D_MODEL), BF16),
                   jax.ShapeDtypeStruct((N_FEAT, T, N_GATE), F32),
                   jax.ShapeDtypeStruct((T // LANES, N_FEAT * N_GATE, LANES), F32)],
        compiler_params=_params(("parallel",)),
        name="in_proj0",
    )(x2, nw, w_in, w_kt, gate_bias)


N_PAIRS = M_HEADS // 2
HALF = LANES // 2
PAIR_W = 2 * 2 * M_V
ROW_W, ROW_A, ROW_G, ROW_WMAX, N_ROWF = 0, N_GATE, 2 * N_GATE, 3 * N_GATE, 4 * N_GATE


def _lane_pair(x, col):
    lane = lax.broadcasted_iota(jnp.int32, (x.shape[0], LANES), 1)
    return jnp.where(lane < HALF, x[:, col:col + 1], x[:, col + 1:col + 2])


def _mlstm_chunk(q_ref, kt_ref, v_ref, cf_ref, rf_ref, h_ref, s_ref, sb_ref, mc_ref, mr_ref, sub, fwd):
    L = M_CHUNK
    r0 = pl.multiple_of(sub * L, L)
    rows = pl.ds(r0, L)
    lo = 0 if fwd else M_HEADS
    b_c = cf_ref[0, rows, :]
    m_r = mr_ref[...]
    mj = b_c + jnp.maximum(m_r, cf_ref[1, rows, :])
    u = b_c - mj
    iw = jnp.exp(b_c + m_r - mj)
    en = jnp.exp(-mj)
    mr_ref[...] = cf_ref[2, pl.ds(r0, 1), :] + jnp.maximum(m_r, cf_ref[3, pl.ds(r0, 1), :])
    low_half = lax.broadcasted_iota(jnp.int32, (1, LANES), 1) < HALF
    tile = rf_ref[sub // 2]
    swapped = pltpu.roll(tile, HALF, 1)
    first = sub % 2 == 0
    own_lo = jnp.where(first, tile, swapped)
    own_hi = jnp.where(first, swapped, tile)
    both = jnp.where(low_half, own_lo, own_hi)
    m_c = mc_ref[...]
    g = both[ROW_G + lo:ROW_G + lo + M_HEADS]
    m_new = g + jnp.maximum(m_c, both[ROW_WMAX + lo:ROW_WMAX + lo + M_HEADS])
    decay = jnp.exp(g + m_c - m_new)
    mc_ref[...] = m_new

    li = lax.broadcasted_iota(jnp.int32, (L, LANES), 0)
    si = lax.broadcasted_iota(jnp.int32, (L, LANES), 1) % HALF
    mask = (si <= li) if fwd else (si >= li)
    same_head = (lax.broadcasted_iota(jnp.int32, (LANES, LANES), 0) // HALF
                 == lax.broadcasted_iota(jnp.int32, (LANES, LANES), 1) // HALF)
    ones = jnp.ones((L, M_V), BF16)
    zeros = jnp.zeros((L, M_V), BF16)
    yield
    q2s, kbds, scores = [], [], []
    for p in range(N_PAIRS):
        q2s.append(q_ref[rows, p * LANES:(p + 1) * LANES])
        kbds.append(jnp.where(same_head, kt_ref[sub, p * LANES:(p + 1) * LANES, :], jnp.zeros((), BF16)))
        scores.append(_dot(q2s[p], kbds[p]))
    yield
    mains, inters = [], []
    for p in range(N_PAIRS):
        def pair_row(base):
            r = base + lo + 2 * p
            return jnp.where(low_half, own_lo[r:r + 1], own_hi[r + 1:r + 2])

        w_pr = pair_row(ROW_W)
        m_new_pr = jnp.where(low_half, m_new[2 * p:2 * p + 1], m_new[2 * p + 1:2 * p + 2])
        k_scale = jnp.exp(pair_row(ROW_A) - m_new_pr)
        dw = jnp.exp(jnp.where(mask, _lane_pair(u, lo + 2 * p) + w_pr, -jnp.inf))
        sm = (scores[p] * dw).astype(BF16)
        kw = (kbds[p].astype(F32) * k_scale).astype(BF16)
        qi = (q2s[p].astype(F32) * _lane_pair(iw, lo + 2 * p)).astype(BF16)
        pad = jnp.concatenate([zeros, zeros], axis=1)
        pair = []
        for j in range(2):
            h = 2 * p + j
            v_ext = jnp.concatenate([v_ref[rows, h * M_V:(h + 1) * M_V], ones], axis=1)
            rhs = jnp.concatenate([v_ext, pad] if j == 0 else [pad, v_ext], axis=0)
            pair.append(_dot(jnp.concatenate([sm, kw[j * M_QK:(j + 1) * M_QK]], axis=0), rhs))
        mains.append(pair)
        inters.append(_dot(qi, sb_ref[p]))
    yield
    outs = []
    for p in range(N_PAIRS):
        for j in range(2):
            h = 2 * p + j
            out = mains[p][j][:L] + inters[p][:, j * 2 * M_V:(j + 1) * 2 * M_V]
            outs.append(out[:, :M_V] / jnp.maximum(jnp.abs(out[:, M_V:]), en[:, lo + h:lo + h + 1]))
            blk = (p, slice(j * M_QK, (j + 1) * M_QK), slice(j * 2 * M_V, (j + 1) * 2 * M_V))
            dec = jnp.concatenate([decay[h:h + 1]] * 2, axis=1)
            s_new = dec * s_ref[blk] + mains[p][j][L:]
            s_ref[blk] = s_new
            sb_ref[blk] = s_new.astype(BF16)
    h_ref[rows, :] = jnp.concatenate(outs, axis=1).astype(h_ref.dtype)
    yield


N_MLSTM_STAGES = 4


def _mlstm_kernel(qf, ktf, vf, cff, rff, qb, ktb, vb, cfb, rfb, hf_ref, hb_ref,
                  sf, sbf, mcf, mrf, sb, sbb, mcb, mrb, *, nsub):
    @pl.when(pl.program_id(1) == 0)
    def _():
        for ref in (sf, sbf, mcf, mrf, sb, sbb, mcb, mrb):
            ref[...] = jnp.zeros_like(ref)

    def body(j, carry):
        chunks = [_mlstm_chunk(qf, ktf, vf, cff, rff, hf_ref, sf, sbf, mcf, mrf, j, True),
                  _mlstm_chunk(qb, ktb, vb, cfb, rfb, hb_ref, sb, sbb, mcb, mrb, nsub - 1 - j, False)]
        for _ in range(N_MLSTM_STAGES):
            for chunk in chunks:
                next(chunk)
        return carry

    lax.fori_loop(0, nsub, body, 0)


def _mlstm(q, ktd, v, cfeat, rfeat, tb):
    B, S, _ = q.shape
    n = S // tb
    nsub = tb // M_CHUNK
    fw = lambda b, c: (b, c, 0)
    bw = lambda b, c: (b, n - 1 - c, 0)
    fw4 = lambda b, c: (b, c, 0, 0)
    bw4 = lambda b, c: (b, n - 1 - c, 0, 0)

    def specs(im, im4):
        return [pl.BlockSpec((None, tb, M_QK_ALL), im), pl.BlockSpec((None, nsub, M_QK_ALL, LANES), im4),
                pl.BlockSpec((None, tb, D_MODEL), im),
                pl.BlockSpec((4, None, tb, N_GATE), lambda b, c: (0,) + im(b, c)),
                pl.BlockSpec((None, nsub // 2, N_ROWF, LANES), im4)]

    state = [pltpu.VMEM((N_PAIRS, LANES, PAIR_W), F32), pltpu.VMEM((N_PAIRS, LANES, PAIR_W), BF16),
             pltpu.VMEM((M_HEADS, LANES), F32), pltpu.VMEM((1, N_GATE), F32)]
    return pl.pallas_call(
        functools.partial(_mlstm_kernel, nsub=nsub),
        grid=(B, n),
        in_specs=specs(fw, fw4) + specs(bw, bw4),
        out_specs=[pl.BlockSpec((None, tb, D_MODEL), fw), pl.BlockSpec((None, tb, D_MODEL), bw)],
        out_shape=[jax.ShapeDtypeStruct((B, S, D_MODEL), BF16)] * 2,
        scratch_shapes=state + state,
        compiler_params=_params(("parallel", "arbitrary")),
        name="mlstm",
    )(q, ktd, v, cfeat, rfeat, q, ktd, v, cfeat, rfeat)


def _sigmoid(x):
    return 0.5 * jnp.tanh(0.5 * x) + 0.5


def _tail_common(mix_pair, x_ref, w_out_ref, nf_ref, wg_ref, wu_ref, wd_ref, npl_ref, pg_ref, p_ref, pp_ref):
    emb = _dot(p_ref[...].astype(BF16), pp_ref[...])
    h1 = x_ref[...]
    for i in range(D_MODEL // MXU_COLS):
        h1 = h1 + _dot(mix_pair(i), w_out_ref[i * MXU_COLS:(i + 1) * MXU_COLS, :])
    hn = _rms(h1, nf_ref[...]).astype(BF16)
    gate = _dot(hn, wg_ref[...])
    act = (gate * _sigmoid(gate) * _dot(hn, wu_ref[...])).astype(BF16)
    h2 = h1 + _dot(act, wd_ref[...])
    pgate = _sigmoid(_dot(_rms(h2, npl_ref[...]).astype(BF16), pg_ref[...]))
    return h2 + pgate * emb


N_SLABS = D_MODEL // LANES


def _tail0_kernel(hf_ref, hb_ref, o_ref, hnorm_ref, x_ref, w_out_ref, nf_ref, wg_ref, wu_ref, wd_ref,
                  npl_ref, pg_ref, p_ref, pp_ref, nnext_ref, out_ref, hn_ref, hn4_ref, hn16_ref, slab_ref,
                  slab4_ref):
    def head(h):
        sl = slice(h * M_V, (h + 1) * M_V)
        hh = hf_ref[:, sl].astype(F32) + hb_ref[:, sl].astype(F32)
        hh = hh * lax.rsqrt(jnp.mean(hh * hh, axis=-1, keepdims=True) + EPS) * hnorm_ref[:, sl]
        return (_sigmoid(o_ref[:, sl].astype(F32)) * hh).astype(BF16)

    h3 = _tail_common(lambda i: jnp.concatenate([head(2 * i), head(2 * i + 1)], axis=1), x_ref, w_out_ref,
                      nf_ref, wg_ref, wu_ref, wd_ref, npl_ref, pg_ref, p_ref, pp_ref)
    out_ref[...] = h3
    hn = _rms(h3, nnext_ref[...])
    hn_ref[...] = hn.astype(BF16)
    tm = hn.shape[0]
    q4 = tm // 4
    for s in range(N_SLABS):
        slab_ref[s] = hn[:, s * LANES:(s + 1) * LANES]
    for r4 in range(4):
        parts = [slab_ref[s, pl.ds(r4, q4, stride=4), :] for s in range(N_SLABS)]
        hn4_ref[r4] = jnp.concatenate(parts, axis=1).astype(BF16)
        for s in range(N_SLABS):
            slab4_ref[s, r4 * q4:(r4 + 1) * q4, :] = parts[s]
    for r16 in range(16):
        start = (r16 % 4) * q4 + r16 // 4
        hn16_ref[r16] = jnp.concatenate([slab4_ref[s, pl.ds(start, tm // 16, stride=4), :]
                                         for s in range(N_SLABS)], axis=1).astype(BF16)


def _tail1_kernel(o0_ref, o1_ref, o2_ref, l0_ref, l1_ref, l2_ref, x_ref, w_out_ref, nf_ref, wg_ref, wu_ref,
                  wd_ref, npl_ref, pg_ref, p_ref, pp_ref, nfin_ref, out_ref, slab1_ref, slab2_ref,
                  ls1_ref, ls2_ref, slabt_ref, lst_ref):
    tm = o0_ref.shape[0]
    q4 = tm // 4
    for r16 in range(16):
        rows = pl.ds((r16 % 4) * q4 + r16 // 4, tm // 16, stride=4)
        lst_ref[rows, :] = l2_ref[r16]
        o_r = o2_ref[r16].astype(F32)
        for s in range(N_SLABS):
            slabt_ref[s, rows, :] = o_r[:, s * LANES:(s + 1) * LANES]
    for r4 in range(4):
        rows = pl.ds(r4, q4, stride=4)
        block = slice(r4 * q4, (r4 + 1) * q4)
        ls1_ref[rows, :] = l1_ref[r4]
        ls2_ref[rows, :] = lst_ref[block, :]
        o_r = o1_ref[r4].astype(F32)
        for s in range(N_SLABS):
            slab1_ref[s, rows, :] = o_r[:, s * LANES:(s + 1) * LANES]
            slab2_ref[s, rows, :] = slabt_ref[s, block, :]
    l0 = l0_ref[...]
    l1 = ls1_ref[...]
    l2 = ls2_ref[...]
    mx = jnp.maximum(jnp.maximum(l0, l1), l2)
    e0 = jnp.exp(l0 - mx)
    e1 = jnp.exp(l1 - mx)
    e2 = jnp.exp(l2 - mx)
    inv = 1.0 / (e0 + e1 + e2)
    w1 = e1 * inv
    w2 = e2 * inv

    def head(h):
        o0 = o0_ref[:, h * A_DH:(h + 1) * A_DH].astype(F32)
        return (o0 + w1[:, h:h + 1] * (slab1_ref[h] - o0) + w2[:, h:h + 1] * (slab2_ref[h] - o0)).astype(BF16)

    h3 = _tail_common(lambda i: jnp.concatenate([head(2 * i), head(2 * i + 1)], axis=1), x_ref, w_out_ref,
                      nf_ref, wg_ref, wu_ref, wd_ref, npl_ref, pg_ref, p_ref, pp_ref)
    out_ref[...] = _rms(h3, nfin_ref[...])


def _layer_spec(shape, layer):
    return pl.BlockSpec((None,) + shape, lambda *_: (layer, 0, 0), pipeline_mode=pl.Buffered(1))


def _tail_weight_specs(layer):
    return [_layer_spec((D_MODEL, D_MODEL), 0), _const_spec((1, D_MODEL)),
            _layer_spec((D_MODEL, FFN_HIDDEN), layer), _layer_spec((D_MODEL, FFN_HIDDEN), layer),
            _layer_spec((FFN_HIDDEN, D_MODEL), layer), _const_spec((1, D_MODEL)),
            _layer_spec((D_MODEL, D_MODEL), layer)]


def _residue_spec(dil, tm, n, seq):
    nb = seq // tm
    return pl.BlockSpec((None, dil, tm // dil, n), lambda i: (i // nb, 0, i % nb, 0))


def _tail0(hf, hb, o, hnorm, x2, w_out, nf, wg, wu, wd, npl, pg, p2, pproj, nnext, tm, batch):
    T = x2.shape[0]
    S = T // batch
    row = lambda n: pl.BlockSpec((tm, n), lambda i: (i, 0))
    return pl.pallas_call(
        _tail0_kernel,
        grid=(T // tm,),
        in_specs=[row(D_MODEL), row(D_MODEL), row(D_MODEL), _const_spec((1, D_MODEL)), row(D_MODEL)]
        + _tail_weight_specs(0) + [pl.BlockSpec((None, tm, PLE_DIM), lambda i: (0, i, 0)),
                                   _layer_spec((PLE_DIM, D_MODEL), 0), _const_spec((1, D_MODEL))],
        out_specs=[row(D_MODEL), row(D_MODEL), _residue_spec(4, tm, D_MODEL, S),
                   _residue_spec(16, tm, D_MODEL, S)],
        out_shape=[jax.ShapeDtypeStruct((T, D_MODEL), F32), jax.ShapeDtypeStruct((T, D_MODEL), BF16),
                   jax.ShapeDtypeStruct((batch, 4, S // 4, D_MODEL), BF16),
                   jax.ShapeDtypeStruct((batch, 16, S // 16, D_MODEL), BF16)],
        scratch_shapes=[pltpu.VMEM((N_SLABS, tm, LANES), F32)] * 2,
        compiler_params=_params(("parallel",)),
        name="tail0",
    )(hf, hb, o, hnorm, x2, w_out, nf, wg, wu, wd, npl, pg, p2, pproj, nnext)


def _tail1(o0, o1, o2, l0, l1, l2, x2, w_out, nf, wg, wu, wd, npl, pg, p2, pproj, nfin, tm, batch):
    T = x2.shape[0]
    S = T // batch
    row = lambda n: pl.BlockSpec((tm, n), lambda i: (i, 0))
    return pl.pallas_call(
        _tail1_kernel,
        grid=(T // tm,),
        in_specs=[row(D_MODEL), _residue_spec(4, tm, D_MODEL, S), _residue_spec(16, tm, D_MODEL, S),
                  row(LANES), _residue_spec(4, tm, LANES, S), _residue_spec(16, tm, LANES, S), row(D_MODEL)]
        + _tail_weight_specs(1) + [pl.BlockSpec((None, tm, PLE_DIM), lambda i: (1, i, 0)),
                                   _layer_spec((PLE_DIM, D_MODEL), 1), _const_spec((1, D_MODEL))],
        out_specs=row(D_MODEL),
        out_shape=jax.ShapeDtypeStruct((T, D_MODEL), F32),
        scratch_shapes=[pltpu.VMEM((N_SLABS, tm, LANES), F32),
                        pltpu.VMEM((N_SLABS, tm, LANES), F32), pltpu.VMEM((tm, LANES), F32),
                        pltpu.VMEM((tm, LANES), F32), pltpu.VMEM((N_SLABS, tm, LANES), F32),
                        pltpu.VMEM((tm, LANES), F32)],
        compiler_params=_params(("parallel",)),
        name="tail1",
    )(o0, o1, o2, l0, l1, l2, x2, w_out, nf, wg, wu, wd, npl, pg, p2, pproj, nfin)


def _rope_tab_kernel(pos4_ref, invf_ref, cos_ref, sin_ref, cos4_ref, sin4_ref, cos16_ref, sin16_ref):
    ang = pos4_ref[...].astype(F32) * invf_ref[...]
    lane = lax.broadcasted_iota(jnp.int32, ang.shape, 1)
    rotary = lane % HALF < ROPE_HALF
    cos_all = jnp.cos(ang)
    sin_all = jnp.sin(ang)
    tm = cos_ref.shape[0]
    for j in range(4):
        c, s = cos_all, sin_all
        if j:
            c = pltpu.roll(c, LANES - ROPE_HALF * j, 1)
            s = pltpu.roll(s, LANES - ROPE_HALF * j, 1)
        c = jnp.where(rotary, c, 1.0)
        s = jnp.where(rotary, jnp.where(lane < HALF, -s, s), 0.0)
        cos4_ref[j] = c
        sin4_ref[j] = s
        rows = pl.ds(j, tm // 4, stride=4)
        cos_ref[rows, :] = c
        sin_ref[rows, :] = s
    for r16 in range(16):
        rows = pl.ds(r16 // 4, tm // 16, stride=4)
        cos16_ref[r16] = cos4_ref[r16 % 4, rows, :]
        sin16_ref[r16] = sin4_ref[r16 % 4, rows, :]


def _rope_tables(pos4, invf_lane, tm, batch):
    T = pos4.shape[0] * 4
    S = T // batch
    row = pl.BlockSpec((tm, LANES), lambda i: (i, 0))
    res = lambda dil: jax.ShapeDtypeStruct((batch, dil, S // dil, LANES), F32)
    return pl.pallas_call(
        _rope_tab_kernel,
        grid=(T // tm,),
        in_specs=[pl.BlockSpec((tm // 4, LANES), lambda i: (i, 0)), _const_spec((1, LANES))],
        out_specs=[row, row] + [_residue_spec(4, tm, LANES, S)] * 2 + [_residue_spec(16, tm, LANES, S)] * 2,
        out_shape=[jax.ShapeDtypeStruct((T, LANES), F32)] * 2 + [res(4)] * 2 + [res(16)] * 2,
        compiler_params=_params(("parallel",)),
        name="rope_tab",
    )(pos4, invf_lane)


MXU_COLS = 256
LOG2E = 1.4426950408889634
LN2 = 0.6931471805599453


def _in_proj1_kernel(hn_ref, w_ref, cos_ref, sin_ref, out_ref, wqk_ref):
    @pl.when(pl.program_id(0) == 0)
    def _():
        lane = lax.broadcasted_iota(jnp.int32, (D_MODEL, A_DH), 1)
        stays = (lane < ROPE_HALF) | (lane >= HALF + ROPE_HALF)
        for blk in range(2 * A_HEADS):
            cols = slice(blk * A_DH, (blk + 1) * A_DH)
            x = w_ref[:, cols].astype(F32)
            moved = jnp.where(lane < HALF, pltpu.roll(x, A_DH - ROPE_HALF, 1),
                              pltpu.roll(x, HALF - ROPE_HALF, 1))
            wqk_ref[:, cols] = jnp.where(stays, x, moved).astype(BF16)

    hn = hn_ref[...]
    q_scale = (A_DH ** -0.5) * LOG2E
    tables = ((cos_ref[...] * q_scale, sin_ref[...] * q_scale), (cos_ref[...], sin_ref[...]))
    for nb in range(3 * D_MODEL // MXU_COLS):
        kind = nb * MXU_COLS // D_MODEL
        rhs = wqk_ref if kind < 2 else w_ref
        acc = _dot(hn, rhs[:, nb * MXU_COLS:(nb + 1) * MXU_COLS])
        for half in range(MXU_COLS // A_DH):
            seg = acc[:, half * A_DH:(half + 1) * A_DH]
            if kind < 2:
                c, s = tables[kind]
                seg = seg * c + pltpu.roll(seg, A_DH // 2, 1) * s
            col = nb * MXU_COLS + half * A_DH
            out_ref[:, col:col + A_DH] = seg.astype(BF16)


def _in_proj1(hn, w, g, cos_t, sin_t, tm):
    T = hn.shape[0]
    row = lambda n: pl.BlockSpec((tm, n), lambda i: (i, 0))
    return pl.pallas_call(
        _in_proj1_kernel,
        grid=(T // tm,),
        in_specs=[row(D_MODEL),
                  pl.BlockSpec((D_MODEL, 3 * D_MODEL), lambda i: (0, g), pipeline_mode=pl.Buffered(1)),
                  row(LANES), row(LANES)],
        out_specs=row(3 * D_MODEL),
        out_shape=jax.ShapeDtypeStruct((T, 3 * D_MODEL), BF16),
        scratch_shapes=[pltpu.VMEM((D_MODEL, 2 * D_MODEL), BF16)],
        compiler_params=_params(("arbitrary",)),
        name=f"in_proj1_g{g}",
    )(hn, w, cos_t, sin_t)


ATT_SUB = 128


def _attn_kernel(q_ref, kp_ref, kc_ref, kn_ref, vp_ref, vc_ref, vn_ref, o_ref, lse_ref, kall, vall,
                 *, blkq, n_keys):
    i = pl.program_id(2)
    kall[0:RADIUS, :] = kp_ref[...]
    kall[RADIUS:RADIUS + blkq, :] = kc_ref[...]
    kall[RADIUS + blkq:, :] = kn_ref[...]
    vall[0:RADIUS, :] = vp_ref[...]
    vall[RADIUS:RADIUS + blkq, :] = vc_ref[...]
    vall[RADIUS + blkq:, :] = vn_ref[...]
    nk = ATT_SUB + 2 * RADIUS
    r = lax.broadcasted_iota(jnp.int32, (ATT_SUB, nk), 0)
    c = lax.broadcasted_iota(jnp.int32, (ATT_SUB, nk), 1)
    band = jnp.abs(c - RADIUS - r) <= RADIUS
    lane = lax.broadcasted_iota(jnp.int32, (ATT_SUB, LANES), 1)
    ones = jnp.ones((nk, A_DH), BF16)
    for a in range(blkq // ATT_SUB):
        key0 = i * blkq + a * ATT_SUB - RADIUS
        valid = band & (c + key0 >= 0) & (c + key0 < n_keys)
        rows = slice(a * ATT_SUB, (a + 1) * ATT_SUB)
        krows = slice(a * ATT_SUB, a * ATT_SUB + nk)
        lse_tile = jnp.zeros((ATT_SUB, LANES), F32)
        for h in range(A_HEADS):
            sl = slice(h * A_DH, (h + 1) * A_DH)
            s = jnp.where(valid, _dot_nt(q_ref[rows, sl], kall[krows, sl]), NEG_INF)
            m = jnp.max(s, axis=1, keepdims=True)
            p = jnp.exp2(s - m).astype(BF16)
            pv = _dot(p, jnp.concatenate([vall[krows, sl], ones], axis=1))
            den = pv[:, A_DH:]
            o_ref[rows, sl] = (pv[:, :A_DH] / den).astype(BF16)
            lse_tile = jnp.where(lane == h, (m + jnp.log2(den)) * LN2, lse_tile)
        lse_ref[rows, :] = lse_tile


def _attention(proj, blkq):
    B, dil, U, _ = proj.shape
    blkq = min(blkq, U)
    hb = blkq // RADIUS
    nhalo = U // RADIUS

    def cur(j):
        return pl.BlockSpec((None, None, blkq, D_MODEL), lambda b, r, i: (b, r, i, j))

    def prev(j):
        return pl.BlockSpec((None, None, RADIUS, D_MODEL),
                            lambda b, r, i: (b, r, jnp.maximum(i * hb - 1, 0), j))

    def nxt(j):
        return pl.BlockSpec((None, None, RADIUS, D_MODEL),
                            lambda b, r, i: (b, r, jnp.minimum((i + 1) * hb, nhalo - 1), j))

    return pl.pallas_call(
        functools.partial(_attn_kernel, blkq=blkq, n_keys=U),
        grid=(B, dil, U // blkq),
        in_specs=[cur(0), prev(1), cur(1), nxt(1), prev(2), cur(2), nxt(2)],
        out_specs=[pl.BlockSpec((None, None, blkq, D_MODEL), lambda b, r, i: (b, r, i, 0)),
                   pl.BlockSpec((None, None, blkq, LANES), lambda b, r, i: (b, r, i, 0))],
        out_shape=[jax.ShapeDtypeStruct((B, dil, U, D_MODEL), BF16),
                   jax.ShapeDtypeStruct((B, dil, U, LANES), F32)],
        scratch_shapes=[pltpu.VMEM((blkq + 2 * RADIUS, D_MODEL), BF16)] * 2,
        compiler_params=_params(("parallel", "parallel", "parallel")),
        name=f"attn_d{dil}",
    )(proj, proj, proj, proj, proj, proj, proj)


def kernel(x, p, positions, norm_mix, a_w_in, a_gate_bias, a_head_norm, a_w_out, b_w_in, b_w_out,
           norm_ffn, w_gate, w_up, w_down, norm_ple, ple_gate, ple_proj, final_norm):
    B, S, _ = x.shape
    T = B * S
    bf = lambda w: w.astype(BF16)
    vec = lambda w: w.reshape(1, -1).astype(F32)
    x2 = x.reshape(T, D_MODEL)
    nc = S // M_CHUNK

    w_in = bf(a_w_in)
    q, ktd, v, o, feat, rfeat = _in_proj0(x2, vec(norm_mix[0]), w_in, w_in[0, :, M_QK_ALL:2 * M_QK_ALL].T,
                                          vec(a_gate_bias[0]), tm=ROWS_IN_PROJ0)
    hf, hb = _mlstm(q.reshape(B, S, -1), ktd.reshape(B, nc, M_QK_ALL, LANES), v.reshape(B, S, -1),
                    feat.reshape(N_FEAT, B, S, N_GATE), rfeat.reshape(B, S // LANES, N_ROWF, LANES),
                    tb=ROWS_MLSTM)
    wg_all, wu_all, wd_all, pg_all, pp_all = bf(w_gate), bf(w_up), bf(w_down), bf(ple_gate), bf(ple_proj)
    p_all = p.reshape(p.shape[0], T, PLE_DIM)
    h, *hns = _tail0(hf.reshape(T, -1), hb.reshape(T, -1), o, vec(a_head_norm[0]), x2, bf(a_w_out),
                     vec(norm_ffn[0]), wg_all, wu_all, wd_all, vec(norm_ple[0]), pg_all, p_all, pp_all,
                     vec(norm_mix[1]), tm=ROWS_TAIL, batch=B)

    inv_freq = ROPE_THETA ** (-jnp.arange(0, ROPE_DIM, 2, dtype=F32) / ROPE_DIM)
    invf_lane = jnp.tile(inv_freq, LANES // ROPE_HALF).reshape(1, LANES)
    pos4 = jnp.tile(jnp.repeat(positions.reshape(T // 4, 4), ROPE_HALF, axis=1), (1, 2))
    tabs = _rope_tables(pos4, invf_lane, tm=ROWS_ROPE, batch=B)
    w1 = bf(b_w_in[0])
    o_g, l_g = [], []
    for g, (_, dil) in enumerate(DILATED_GROUPS):
        proj = _in_proj1(hns[g].reshape(T, D_MODEL), w1, g, tabs[2 * g].reshape(T, LANES),
                         tabs[2 * g + 1].reshape(T, LANES), tm=ROWS_IN_PROJ1)
        og, lg = _attention(proj.reshape(B, dil, S // dil, 3 * D_MODEL), blkq=ROWS_ATTN)
        o_g.append(og)
        l_g.append(lg)
    out = _tail1(o_g[0].reshape(T, D_MODEL), o_g[1], o_g[2], l_g[0].reshape(T, LANES), l_g[1], l_g[2], h,
                 bf(b_w_out), vec(norm_ffn[1]), wg_all, wu_all, wd_all, vec(norm_ple[1]), pg_all, p_all, pp_all,
                 vec(final_norm), tm=ROWS_TAIL, batch=B)
    return out.reshape(B, S, D_MODEL)
```

```python
import functools

import jax
import jax.numpy as jnp
from jax import lax
from jax.experimental import pallas as pl
from jax.experimental.pallas import tpu as pltpu

F32 = jnp.float32
BF16 = jnp.bfloat16

D_MODEL = 1024
LANES = 128
EPS = 1e-6

M_HEADS = 8
M_QK = 64
M_V = 128
M_CHUNK = 64
M_QK_ALL = M_HEADS * M_QK

A_HEADS = 8
A_DH = 128
RADIUS = 64
DILATED_GROUPS = ((128, 1), (512, 4), (2048, 16))
N_GROUPS = 3
ROPE_DIM = 32
ROPE_HALF = 16
ROPE_THETA = 500000.0
NEG_INF = -1e30

FFN_HIDDEN = 2816
PLE_DIM = 256

VMEM_LIMIT = 60 * 1024 * 1024

ROWS_IN_PROJ0 = 1024
ROWS_MLSTM = 1024
ROWS_TAIL = 512
ROWS_ROPE = 1024
ROWS_IN_PROJ1 = 2048
ROWS_ATTN = 2048


def _dot(a, b):
    return jnp.dot(a, b, preferred_element_type=F32)


def _dot_nt(a, b):
    return lax.dot_general(a, b, (((1,), (1,)), ((), ())), preferred_element_type=F32)


def _rms(x, w):
    ms = jnp.mean(x * x, axis=-1, keepdims=True)
    return x * lax.rsqrt(ms + EPS) * w


def _const_spec(shape):
    nd = len(shape)
    return pl.BlockSpec(shape, lambda *_: (0,) * nd, pipeline_mode=pl.Buffered(1))


def _params(sem):
    return pltpu.CompilerParams(dimension_semantics=sem, vmem_limit_bytes=VMEM_LIMIT)


N_DIRS = 2
N_GATE = N_DIRS * M_HEADS
N_FEAT = 4


def _chunk_scan(x, op, fill, reverse):
    n = x.shape[0]
    pos = lax.broadcasted_iota(jnp.int32, x.shape, 0) % M_CHUNK
    shift = 1
    while shift < M_CHUNK:
        if reverse:
            shifted, ok = pltpu.roll(x, n - shift, 0), pos < M_CHUNK - shift
        else:
            shifted, ok = pltpu.roll(x, shift, 0), pos >= shift
        x = op(x, jnp.where(ok, shifted, fill))
        shift *= 2
    return x


W0_Q, W0_V, W0_O, W0_G = 0, 2 * M_QK_ALL, 2 * M_QK_ALL + D_MODEL, 2 * M_QK_ALL + 2 * D_MODEL


def _chunk_first(x):
    pos = lax.broadcasted_iota(jnp.int32, x.shape, 0) % M_CHUNK
    shift = 1
    while shift < M_CHUNK:
        x = jnp.where(pos >= shift, pltpu.roll(x, shift, 0), x)
        shift *= 2
    return x


def _in_proj0_kernel(x_ref, nw_ref, w_ref, wkt_ref, gb_ref, q_ref, ktd_ref, v_ref, o_ref, f_ref, rf_ref):
    tm = x_ref.shape[0]
    hn = _rms(x_ref[...], nw_ref[...]).astype(BF16)
    gates = _dot(hn, w_ref[:, W0_G:]) + gb_ref[...]
    ig = gates[:, :N_GATE]
    fg = gates[:, N_GATE:]
    lf = jnp.minimum(fg, 0.0) - jnp.log1p(jnp.exp(-jnp.abs(fg)))
    is_fwd = lax.broadcasted_iota(jnp.int32, lf.shape, 1) < M_HEADS
    q_ref[...] = (_dot(hn, w_ref[:, W0_Q:W0_Q + M_QK_ALL]) * (M_QK ** -0.5)).astype(BF16)
    pre = _chunk_scan(lf, jnp.add, 0.0, False)
    suf = _chunk_scan(lf, jnp.add, 0.0, True)
    b = jnp.where(is_fwd, pre, suf)
    g_tok = pre + suf - lf
    w = ig - b
    v_ref[...] = _dot(hn, w_ref[:, W0_V:W0_O]).astype(BF16)
    wpre = _chunk_scan(w, jnp.maximum, -jnp.inf, False)
    wsuf = _chunk_scan(w, jnp.maximum, -jnp.inf, True)
    wmax = jnp.maximum(wpre, wsuf)
    g = _chunk_first(g_tok)
    for k, feat in enumerate((b, jnp.where(is_fwd, wpre, wsuf), g, wmax)):
        f_ref[k] = feat
    o_ref[...] = _dot(hn, w_ref[:, W0_O:W0_G]).astype(BF16)
    for k, feat in enumerate((w, g + w, g, wmax)):
        feat_t = feat.T
        for t in range(tm // LANES):
            rf_ref[t, k * N_GATE:(k + 1) * N_GATE, :] = feat_t[:, t * LANES:(t + 1) * LANES]
    kt = _dot_nt(wkt_ref[...], hn)
    low = lax.broadcasted_iota(jnp.int32, (M_QK_ALL, LANES), 1) < HALF
    for c in range(0, tm // M_CHUNK, 2):
        both = kt[:, c * M_CHUNK:(c + 2) * M_CHUNK]
        swapped = pltpu.roll(both, HALF, 1)
        ktd_ref[c] = jnp.where(low, both, swapped).astype(BF16)
        ktd_ref[c + 1] = jnp.where(low, swapped, both).astype(BF16)


def _in_proj0(x2, nw, w_in, w_kt, gate_bias, tm):
    T = x2.shape[0]
    row = lambda n: pl.BlockSpec((tm, n), lambda i: (i, 0))
    nch = tm // M_CHUNK
    return pl.pallas_call(
        _in_proj0_kernel,
        grid=(T // tm,),
        in_specs=[row(D_MODEL), _const_spec((1, D_MODEL)), _layer_spec(w_in.shape[1:], 0),
                  _const_spec(w_kt.shape), _const_spec((1, 2 * N_GATE))],
        out_specs=[row(M_QK_ALL), pl.BlockSpec((nch, M_QK_ALL, LANES), lambda i: (i, 0, 0)),
                   row(D_MODEL), row(D_MODEL), pl.BlockSpec((N_FEAT, tm, N_GATE), lambda i: (0, i, 0)),
                   pl.BlockSpec((tm // LANES, N_FEAT * N_GATE, LANES), lambda i: (i, 0, 0))],
        out_shape=[jax.ShapeDtypeStruct((T, M_QK_ALL), BF16),
                   jax.ShapeDtypeStruct((T // M_CHUNK, M_QK_ALL, LANES), BF16),
                   jax.ShapeDtypeStruct((T, D_MODEL), BF16), jax.ShapeDtypeStruct((T, D_MODEL), BF16),
                   jax.ShapeDtypeStruct((N_FEAT, T, N_GATE), F32),
                   jax.ShapeDtypeStruct((T // LANES, N_FEAT * N_GATE, LANES), F32)],
        compiler_params=_params(("parallel",)),
        name="in_proj0",
    )(x2, nw, w_in, w_kt, gate_bias)


N_PAIRS = M_HEADS // 2
HALF = LANES // 2
PAIR_W = 2 * 2 * M_V
ROW_W, ROW_A, ROW_G, ROW_WMAX, N_ROWF = 0, N_GATE, 2 * N_GATE, 3 * N_GATE, 4 * N_GATE


def _lane_pair(x, col):
    lane = lax.broadcasted_iota(jnp.int32, (x.shape[0], LANES), 1)
    return jnp.where(lane < HALF, x[:, col:col + 1], x[:, col + 1:col + 2])


def _mlstm_chunk(q_ref, kt_ref, v_ref, cf_ref, rf_ref, h_ref, s_ref, sb_ref, mc_ref, mr_ref, sub, fwd):
    L = M_CHUNK
    r0 = pl.multiple_of(sub * L, L)
    rows = pl.ds(r0, L)
    lo = 0 if fwd else M_HEADS
    b_c = cf_ref[0, rows, :]
    m_r = mr_ref[...]
    mj = b_c + jnp.maximum(m_r, cf_ref[1, rows, :])
    u = b_c - mj
    iw = jnp.exp(b_c + m_r - mj)
    en = jnp.exp(-mj)
    mr_ref[...] = cf_ref[2, pl.ds(r0, 1), :] + jnp.maximum(m_r, cf_ref[3, pl.ds(r0, 1), :])
    low_half = lax.broadcasted_iota(jnp.int32, (1, LANES), 1) < HALF
    tile = rf_ref[sub // 2]
    swapped = pltpu.roll(tile, HALF, 1)
    first = sub % 2 == 0
    own_lo = jnp.where(first, tile, swapped)
    own_hi = jnp.where(first, swapped, tile)
    both = jnp.where(low_half, own_lo, own_hi)
    m_c = mc_ref[...]
    g = both[ROW_G + lo:ROW_G + lo + M_HEADS]
    m_new = g + jnp.maximum(m_c, both[ROW_WMAX + lo:ROW_WMAX + lo + M_HEADS])
    decay = jnp.exp(g + m_c - m_new)
    mc_ref[...] = m_new

    li = lax.broadcasted_iota(jnp.int32, (L, LANES), 0)
    si = lax.broadcasted_iota(jnp.int32, (L, LANES), 1) % HALF
    mask = (si <= li) if fwd else (si >= li)
    same_head = (lax.broadcasted_iota(jnp.int32, (LANES, LANES), 0) // HALF
                 == lax.broadcasted_iota(jnp.int32, (LANES, LANES), 1) // HALF)
    ones = jnp.ones((L, M_V), BF16)
    zeros = jnp.zeros((L, M_V), BF16)
    yield
    q2s, kbds, scores = [], [], []
    for p in range(N_PAIRS):
        q2s.append(q_ref[rows, p * LANES:(p + 1) * LANES])
        kbds.append(jnp.where(same_head, kt_ref[sub, p * LANES:(p + 1) * LANES, :], jnp.zeros((), BF16)))
        scores.append(_dot(q2s[p], kbds[p]))
    yield
    mains, inters = [], []
    for p in range(N_PAIRS):
        def pair_row(base):
            r = base + lo + 2 * p
            return jnp.where(low_half, own_lo[r:r + 1], own_hi[r + 1:r + 2])

        w_pr = pair_row(ROW_W)
        m_new_pr = jnp.where(low_half, m_new[2 * p:2 * p + 1], m_new[2 * p + 1:2 * p + 2])
        k_scale = jnp.exp(pair_row(ROW_A) - m_new_pr)
        dw = jnp.exp(jnp.where(mask, _lane_pair(u, lo + 2 * p) + w_pr, -jnp.inf))
        sm = (scores[p] * dw).astype(BF16)
        kw = (kbds[p].astype(F32) * k_scale).astype(BF16)
        qi = (q2s[p].astype(F32) * _lane_pair(iw, lo + 2 * p)).astype(BF16)
        pad = jnp.concatenate([zeros, zeros], axis=1)
        pair = []
        for j in range(2):
            h = 2 * p + j
            v_ext = jnp.concatenate([v_ref[rows, h * M_V:(h + 1) * M_V], ones], axis=1)
            rhs = jnp.concatenate([v_ext, pad] if j == 0 else [pad, v_ext], axis=0)
            pair.append(_dot(jnp.concatenate([sm, kw[j * M_QK:(j + 1) * M_QK]], axis=0), rhs))
        mains.append(pair)
        inters.append(_dot(qi, sb_ref[p]))
    yield
    outs = []
    for p in range(N_PAIRS):
        for j in range(2):
            h = 2 * p + j
            out = mains[p][j][:L] + inters[p][:, j * 2 * M_V:(j + 1) * 2 * M_V]
            outs.append(out[:, :M_V] / jnp.maximum(jnp.abs(out[:, M_V:]), en[:, lo + h:lo + h + 1]))
            blk = (p, slice(j * M_QK, (j + 1) * M_QK), slice(j * 2 * M_V, (j + 1) * 2 * M_V))
            dec = jnp.concatenate([decay[h:h + 1]] * 2, axis=1)
            s_new = dec * s_ref[blk] + mains[p][j][L:]
            s_ref[blk] = s_new
            sb_ref[blk] = s_new.astype(BF16)
    h_ref[rows, :] = jnp.concatenate(outs, axis=1).astype(h_ref.dtype)
    yield


N_MLSTM_STAGES = 4


def _mlstm_kernel(qf, ktf, vf, cff, rff, qb, ktb, vb, cfb, rfb, hf_ref, hb_ref,
                  sf, sbf, mcf, mrf, sb, sbb, mcb, mrb, *, nsub):
    @pl.when(pl.program_id(1) == 0)
    def _():
        for ref in (sf, sbf, mcf, mrf, sb, sbb, mcb, mrb):
            ref[...] = jnp.zeros_like(ref)

    def body(j, carry):
        chunks = [_mlstm_chunk(qf, ktf, vf, cff, rff, hf_ref, sf, sbf, mcf, mrf, j, True),
                  _mlstm_chunk(qb, ktb, vb, cfb, rfb, hb_ref, sb, sbb, mcb, mrb, nsub - 1 - j, False)]
        for _ in range(N_MLSTM_STAGES):
            for chunk in chunks:
                next(chunk)
        return carry

    lax.fori_loop(0, nsub, body, 0)


def _mlstm(q, ktd, v, cfeat, rfeat, tb):
    B, S, _ = q.shape
    n = S // tb
    nsub = tb // M_CHUNK
    fw = lambda b, c: (b, c, 0)
    bw = lambda b, c: (b, n - 1 - c, 0)
    fw4 = lambda b, c: (b, c, 0, 0)
    bw4 = lambda b, c: (b, n - 1 - c, 0, 0)

    def specs(im, im4):
        return [pl.BlockSpec((None, tb, M_QK_ALL), im), pl.BlockSpec((None, nsub, M_QK_ALL, LANES), im4),
                pl.BlockSpec((None, tb, D_MODEL), im),
                pl.BlockSpec((4, None, tb, N_GATE), lambda b, c: (0,) + im(b, c)),
                pl.BlockSpec((None, nsub // 2, N_ROWF, LANES), im4)]

    state = [pltpu.VMEM((N_PAIRS, LANES, PAIR_W), F32), pltpu.VMEM((N_PAIRS, LANES, PAIR_W), BF16),
             pltpu.VMEM((M_HEADS, LANES), F32), pltpu.VMEM((1, N_GATE), F32)]
    return pl.pallas_call(
        functools.partial(_mlstm_kernel, nsub=nsub),
        grid=(B, n),
        in_specs=specs(fw, fw4) + specs(bw, bw4),
        out_specs=[pl.BlockSpec((None, tb, D_MODEL), fw), pl.BlockSpec((None, tb, D_MODEL), bw)],
        out_shape=[jax.ShapeDtypeStruct((B, S, D_MODEL), BF16)] * 2,
        scratch_shapes=state + state,
        compiler_params=_params(("parallel", "arbitrary")),
        name="mlstm",
    )(q, ktd, v, cfeat, rfeat, q, ktd, v, cfeat, rfeat)


def _sigmoid(x):
    return 0.5 * jnp.tanh(0.5 * x) + 0.5


def _tail_common(mix_pair, x_ref, w_out_ref, nf_ref, wg_ref, wu_ref, wd_ref, npl_ref, pg_ref, p_ref, pp_ref):
    emb = _dot(p_ref[...].astype(BF16), pp_ref[...])
    h1 = x_ref[...]
    for i in range(D_MODEL // MXU_COLS):
        h1 = h1 + _dot(mix_pair(i), w_out_ref[i * MXU_COLS:(i + 1) * MXU_COLS, :])
    hn = _rms(h1, nf_ref[...]).astype(BF16)
    gate = _dot(hn, wg_ref[...])
    act = (gate * _sigmoid(gate) * _dot(hn, wu_ref[...])).astype(BF16)
    h2 = h1 + _dot(act, wd_ref[...])
    pgate = _sigmoid(_dot(_rms(h2, npl_ref[...]).astype(BF16), pg_ref[...]))
    return h2 + pgate * emb


N_SLABS = D_MODEL // LANES


def _tail0_kernel(hf_ref, hb_ref, o_ref, hnorm_ref, x_ref, w_out_ref, nf_ref, wg_ref, wu_ref, wd_ref,
                  npl_ref, pg_ref, p_ref, pp_ref, nnext_ref, out_ref, hn_ref, hn4_ref, hn16_ref, slab_ref,
                  slab4_ref):
    def head(h):
        sl = slice(h * M_V, (h + 1) * M_V)
        hh = hf_ref[:, sl].astype(F32) + hb_ref[:, sl].astype(F32)
        hh = hh * lax.rsqrt(jnp.mean(hh * hh, axis=-1, keepdims=True) + EPS) * hnorm_ref[:, sl]
        return (_sigmoid(o_ref[:, sl].astype(F32)) * hh).astype(BF16)

    h3 = _tail_common(lambda i: jnp.concatenate([head(2 * i), head(2 * i + 1)], axis=1), x_ref, w_out_ref,
                      nf_ref, wg_ref, wu_ref, wd_ref, npl_ref, pg_ref, p_ref, pp_ref)
    out_ref[...] = h3
    hn = _rms(h3, nnext_ref[...])
    hn_ref[...] = hn.astype(BF16)
    tm = hn.shape[0]
    q4 = tm // 4
    for s in range(N_SLABS):
        slab_ref[s] = hn[:, s * LANES:(s + 1) * LANES]
    for r4 in range(4):
        parts = [slab_ref[s, pl.ds(r4, q4, stride=4), :] for s in range(N_SLABS)]
        hn4_ref[r4] = jnp.concatenate(parts, axis=1).astype(BF16)
        for s in range(N_SLABS):
            slab4_ref[s, r4 * q4:(r4 + 1) * q4, :] = parts[s]
    for r16 in range(16):
        start = (r16 % 4) * q4 + r16 // 4
        hn16_ref[r16] = jnp.concatenate([slab4_ref[s, pl.ds(start, tm // 16, stride=4), :]
                                         for s in range(N_SLABS)], axis=1).astype(BF16)


def _tail1_kernel(o0_ref, o1_ref, o2_ref, l0_ref, l1_ref, l2_ref, x_ref, w_out_ref, nf_ref, wg_ref, wu_ref,
                  wd_ref, npl_ref, pg_ref, p_ref, pp_ref, nfin_ref, out_ref, slab1_ref, slab2_ref,
                  ls1_ref, ls2_ref, slabt_ref, lst_ref):
    tm = o0_ref.shape[0]
    q4 = tm // 4
    for r16 in range(16):
        rows = pl.ds((r16 % 4) * q4 + r16 // 4, tm // 16, stride=4)
        lst_ref[rows, :] = l2_ref[r16]
        o_r = o2_ref[r16].astype(F32)
        for s in range(N_SLABS):
            slabt_ref[s, rows, :] = o_r[:, s * LANES:(s + 1) * LANES]
    for r4 in range(4):
        rows = pl.ds(r4, q4, stride=4)
        block = slice(r4 * q4, (r4 + 1) * q4)
        ls1_ref[rows, :] = l1_ref[r4]
        ls2_ref[rows, :] = lst_ref[block, :]
        o_r = o1_ref[r4].astype(F32)
        for s in range(N_SLABS):
            slab1_ref[s, rows, :] = o_r[:, s * LANES:(s + 1) * LANES]
            slab2_ref[s, rows, :] = slabt_ref[s, block, :]
    l0 = l0_ref[...]
    l1 = ls1_ref[...]
    l2 = ls2_ref[...]
    mx = jnp.maximum(jnp.maximum(l0, l1), l2)
    e0 = jnp.exp(l0 - mx)
    e1 = jnp.exp(l1 - mx)
    e2 = jnp.exp(l2 - mx)
    inv = 1.0 / (e0 + e1 + e2)
    w1 = e1 * inv
    w2 = e2 * inv

    def head(h):
        o0 = o0_ref[:, h * A_DH:(h + 1) * A_DH].astype(F32)
        return (o0 + w1[:, h:h + 1] * (slab1_ref[h] - o0) + w2[:, h:h + 1] * (slab2_ref[h] - o0)).astype(BF16)

    h3 = _tail_common(lambda i: jnp.concatenate([head(2 * i), head(2 * i + 1)], axis=1), x_ref, w_out_ref,
                      nf_ref, wg_ref, wu_ref, wd_ref, npl_ref, pg_ref, p_ref, pp_ref)
    out_ref[...] = _rms(h3, nfin_ref[...])


def _layer_spec(shape, layer):
    return pl.BlockSpec((None,) + shape, lambda *_: (layer, 0, 0), pipeline_mode=pl.Buffered(1))


def _tail_weight_specs(layer):
    return [_layer_spec((D_MODEL, D_MODEL), 0), _const_spec((1, D_MODEL)),
            _layer_spec((D_MODEL, FFN_HIDDEN), layer), _layer_spec((D_MODEL, FFN_HIDDEN), layer),
            _layer_spec((FFN_HIDDEN, D_MODEL), layer), _const_spec((1, D_MODEL)),
            _layer_spec((D_MODEL, D_MODEL), layer)]


def _residue_spec(dil, tm, n, seq):
    nb = seq // tm
    return pl.BlockSpec((None, dil, tm // dil, n), lambda i: (i // nb, 0, i % nb, 0))


def _tail0(hf, hb, o, hnorm, x2, w_out, nf, wg, wu, wd, npl, pg, p2, pproj, nnext, tm, batch):
    T = x2.shape[0]
    S = T // batch
    row = lambda n: pl.BlockSpec((tm, n), lambda i: (i, 0))
    return pl.pallas_call(
        _tail0_kernel,
        grid=(T // tm,),
        in_specs=[row(D_MODEL), row(D_MODEL), row(D_MODEL), _const_spec((1, D_MODEL)), row(D_MODEL)]
        + _tail_weight_specs(0) + [pl.BlockSpec((None, tm, PLE_DIM), lambda i: (0, i, 0)),
                                   _layer_spec((PLE_DIM, D_MODEL), 0), _const_spec((1, D_MODEL))],
        out_specs=[row(D_MODEL), row(D_MODEL), _residue_spec(4, tm, D_MODEL, S),
                   _residue_spec(16, tm, D_MODEL, S)],
        out_shape=[jax.ShapeDtypeStruct((T, D_MODEL), F32), jax.ShapeDtypeStruct((T, D_MODEL), BF16),
                   jax.ShapeDtypeStruct((batch, 4, S // 4, D_MODEL), BF16),
                   jax.ShapeDtypeStruct((batch, 16, S // 16, D_MODEL), BF16)],
        scratch_shapes=[pltpu.VMEM((N_SLABS, tm, LANES), F32)] * 2,
        compiler_params=_params(("parallel",)),
        name="tail0",
    )(hf, hb, o, hnorm, x2, w_out, nf, wg, wu, wd, npl, pg, p2, pproj, nnext)


def _tail1(o0, o1, o2, l0, l1, l2, x2, w_out, nf, wg, wu, wd, npl, pg, p2, pproj, nfin, tm, batch):
    T = x2.shape[0]
    S = T // batch
    row = lambda n: pl.BlockSpec((tm, n), lambda i: (i, 0))
    return pl.pallas_call(
        _tail1_kernel,
        grid=(T // tm,),
        in_specs=[row(D_MODEL), _residue_spec(4, tm, D_MODEL, S), _residue_spec(16, tm, D_MODEL, S),
                  row(LANES), _residue_spec(4, tm, LANES, S), _residue_spec(16, tm, LANES, S), row(D_MODEL)]
        + _tail_weight_specs(1) + [pl.BlockSpec((None, tm, PLE_DIM), lambda i: (1, i, 0)),
                                   _layer_spec((PLE_DIM, D_MODEL), 1), _const_spec((1, D_MODEL))],
        out_specs=row(D_MODEL),
        out_shape=jax.ShapeDtypeStruct((T, D_MODEL), F32),
        scratch_shapes=[pltpu.VMEM((N_SLABS, tm, LANES), F32),
                        pltpu.VMEM((N_SLABS, tm, LANES), F32), pltpu.VMEM((tm, LANES), F32),
                        pltpu.VMEM((tm, LANES), F32), pltpu.VMEM((N_SLABS, tm, LANES), F32),
                        pltpu.VMEM((tm, LANES), F32)],
        compiler_params=_params(("parallel",)),
        name="tail1",
    )(o0, o1, o2, l0, l1, l2, x2, w_out, nf, wg, wu, wd, npl, pg, p2, pproj, nfin)


def _rope_tab_kernel(pos4_ref, invf_ref, cos_ref, sin_ref, cos4_ref, sin4_ref, cos16_ref, sin16_ref):
    ang = pos4_ref[...].astype(F32) * invf_ref[...]
    lane = lax.broadcasted_iota(jnp.int32, ang.shape, 1)
    rotary = lane % HALF < ROPE_HALF
    cos_all = jnp.cos(ang)
    sin_all = jnp.sin(ang)
    tm = cos_ref.shape[0]
    for j in range(4):
        c, s = cos_all, sin_all
        if j:
            c = pltpu.roll(c, LANES - ROPE_HALF * j, 1)
            s = pltpu.roll(s, LANES - ROPE_HALF * j, 1)
        c = jnp.where(rotary, c, 1.0)
        s = jnp.where(rotary, jnp.where(lane < HALF, -s, s), 0.0)
        cos4_ref[j] = c
        sin4_ref[j] = s
        rows = pl.ds(j, tm // 4, stride=4)
        cos_ref[rows, :] = c
        sin_ref[rows, :] = s
    for r16 in range(16):
        rows = pl.ds(r16 // 4, tm // 16, stride=4)
        cos16_ref[r16] = cos4_ref[r16 % 4, rows, :]
        sin16_ref[r16] = sin4_ref[r16 % 4, rows, :]


def _rope_tables(pos4, invf_lane, tm, batch):
    T = pos4.shape[0] * 4
    S = T // batch
    row = pl.BlockSpec((tm, LANES), lambda i: (i, 0))
    res = lambda dil: jax.ShapeDtypeStruct((batch, dil, S // dil, LANES), F32)
    return pl.pallas_call(
        _rope_tab_kernel,
        grid=(T // tm,),
        in_specs=[pl.BlockSpec((tm // 4, LANES), lambda i: (i, 0)), _const_spec((1, LANES))],
        out_specs=[row, row] + [_residue_spec(4, tm, LANES, S)] * 2 + [_residue_spec(16, tm, LANES, S)] * 2,
        out_shape=[jax.ShapeDtypeStruct((T, LANES), F32)] * 2 + [res(4)] * 2 + [res(16)] * 2,
        compiler_params=_params(("parallel",)),
        name="rope_tab",
    )(pos4, invf_lane)


MXU_COLS = 256
LOG2E = 1.4426950408889634
LN2 = 0.6931471805599453


def _in_proj1_kernel(hn_ref, w_ref, cos_ref, sin_ref, out_ref, wqk_ref):
    @pl.when(pl.program_id(0) == 0)
    def _():
        lane = lax.broadcasted_iota(jnp.int32, (D_MODEL, A_DH), 1)
        stays = (lane < ROPE_HALF) | (lane >= HALF + ROPE_HALF)
        for blk in range(2 * A_HEADS):
            cols = slice(blk * A_DH, (blk + 1) * A_DH)
            x = w_ref[:, cols].astype(F32)
            moved = jnp.where(lane < HALF, pltpu.roll(x, A_DH - ROPE_HALF, 1),
                              pltpu.roll(x, HALF - ROPE_HALF, 1))
            wqk_ref[:, cols] = jnp.where(stays, x, moved).astype(BF16)

    hn = hn_ref[...]
    q_scale = (A_DH ** -0.5) * LOG2E
    tables = ((cos_ref[...] * q_scale, sin_ref[...] * q_scale), (cos_ref[...], sin_ref[...]))
    for nb in range(3 * D_MODEL // MXU_COLS):
        kind = nb * MXU_COLS // D_MODEL
        rhs = wqk_ref if kind < 2 else w_ref
        acc = _dot(hn, rhs[:, nb * MXU_COLS:(nb + 1) * MXU_COLS])
        for half in range(MXU_COLS // A_DH):
            seg = acc[:, half * A_DH:(half + 1) * A_DH]
            if kind < 2:
                c, s = tables[kind]
                seg = seg * c + pltpu.roll(seg, A_DH // 2, 1) * s
            col = nb * MXU_COLS + half * A_DH
            out_ref[:, col:col + A_DH] = seg.astype(BF16)


def _in_proj1(hn, w, g, cos_t, sin_t, tm):
    T = hn.shape[0]
    row = lambda n: pl.BlockSpec((tm, n), lambda i: (i, 0))
    return pl.pallas_call(
        _in_proj1_kernel,
        grid=(T // tm,),
        in_specs=[row(D_MODEL),
                  pl.BlockSpec((D_MODEL, 3 * D_MODEL), lambda i: (0, g), pipeline_mode=pl.Buffered(1)),
                  row(LANES), row(LANES)],
        out_specs=row(3 * D_MODEL),
        out_shape=jax.ShapeDtypeStruct((T, 3 * D_MODEL), BF16),
        scratch_shapes=[pltpu.VMEM((D_MODEL, 2 * D_MODEL), BF16)],
        compiler_params=_params(("arbitrary",)),
        name=f"in_proj1_g{g}",
    )(hn, w, cos_t, sin_t)


ATT_SUB = 128


def _attn_kernel(q_ref, kp_ref, kc_ref, kn_ref, vp_ref, vc_ref, vn_ref, o_ref, lse_ref, kall, vall,
                 *, blkq, n_keys):
    i = pl.program_id(2)
    kall[0:RADIUS, :] = kp_ref[...]
    kall[RADIUS:RADIUS + blkq, :] = kc_ref[...]
    kall[RADIUS + blkq:, :] = kn_ref[...]
    vall[0:RADIUS, :] = vp_ref[...]
    vall[RADIUS:RADIUS + blkq, :] = vc_ref[...]
    vall[RADIUS + blkq:, :] = vn_ref[...]
    nk = ATT_SUB + 2 * RADIUS
    r = lax.broadcasted_iota(jnp.int32, (ATT_SUB, nk), 0)
    c = lax.broadcasted_iota(jnp.int32, (ATT_SUB, nk), 1)
    band = jnp.abs(c - RADIUS - r) <= RADIUS
    lane = lax.broadcasted_iota(jnp.int32, (ATT_SUB, LANES), 1)
    ones = jnp.ones((nk, A_DH), BF16)
    for a in range(blkq // ATT_SUB):
        key0 = i * blkq + a * ATT_SUB - RADIUS
        valid = band & (c + key0 >= 0) & (c + key0 < n_keys)
        rows = slice(a * ATT_SUB, (a + 1) * ATT_SUB)
        krows = slice(a * ATT_SUB, a * ATT_SUB + nk)
        lse_tile = jnp.zeros((ATT_SUB, LANES), F32)
        for h in range(A_HEADS):
            sl = slice(h * A_DH, (h + 1) * A_DH)
            s = jnp.where(valid, _dot_nt(q_ref[rows, sl], kall[krows, sl]), NEG_INF)
            m = jnp.max(s, axis=1, keepdims=True)
            p = jnp.exp2(s - m).astype(BF16)
            pv = _dot(p, jnp.concatenate([vall[krows, sl], ones], axis=1))
            den = pv[:, A_DH:]
            o_ref[rows, sl] = (pv[:, :A_DH] / den).astype(BF16)
            lse_tile = jnp.where(lane == h, (m + jnp.log2(den)) * LN2, lse_tile)
        lse_ref[rows, :] = lse_tile


def _attention(proj, blkq):
    B, dil, U, _ = proj.shape
    blkq = min(blkq, U)
    hb = blkq // RADIUS
    nhalo = U // RADIUS

    def cur(j):
        return pl.BlockSpec((None, None, blkq, D_MODEL), lambda b, r, i: (b, r, i, j))

    def prev(j):
        return pl.BlockSpec((None, None, RADIUS, D_MODEL),
                            lambda b, r, i: (b, r, jnp.maximum(i * hb - 1, 0), j))

    def nxt(j):
        return pl.BlockSpec((None, None, RADIUS, D_MODEL),
                            lambda b, r, i: (b, r, jnp.minimum((i + 1) * hb, nhalo - 1), j))

    return pl.pallas_call(
        functools.partial(_attn_kernel, blkq=blkq, n_keys=U),
        grid=(B, dil, U // blkq),
        in_specs=[cur(0), prev(1), cur(1), nxt(1), prev(2), cur(2), nxt(2)],
        out_specs=[pl.BlockSpec((None, None, blkq, D_MODEL), lambda b, r, i: (b, r, i, 0)),
                   pl.BlockSpec((None, None, blkq, LANES), lambda b, r, i: (b, r, i, 0))],
        out_shape=[jax.ShapeDtypeStruct((B, dil, U, D_MODEL), BF16),
                   jax.ShapeDtypeStruct((B, dil, U, LANES), F32)],
        scratch_shapes=[pltpu.VMEM((blkq + 2 * RADIUS, D_MODEL), BF16)] * 2,
        compiler_params=_params(("parallel", "parallel", "parallel")),
        name=f"attn_d{dil}",
    )(proj, proj, proj, proj, proj, proj, proj)


def kernel(x, p, positions, norm_mix, a_w_in, a_gate_bias, a_head_norm, a_w_out, b_w_in, b_w_out,
           norm_ffn, w_gate, w_up, w_down, norm_ple, ple_gate, ple_proj, final_norm):
    B, S, _ = x.shape
    T = B * S
    bf = lambda w: w.astype(BF16)
    vec = lambda w: w.reshape(1, -1).astype(F32)
    x2 = x.reshape(T, D_MODEL)
    nc = S // M_CHUNK

    w_in = bf(a_w_in)
    q, ktd, v, o, feat, rfeat = _in_proj0(x2, vec(norm_mix[0]), w_in, w_in[0, :, M_QK_ALL:2 * M_QK_ALL].T,
                                          vec(a_gate_bias[0]), tm=ROWS_IN_PROJ0)
    hf, hb = _mlstm(q.reshape(B, S, -1), ktd.reshape(B, nc, M_QK_ALL, LANES), v.reshape(B, S, -1),
                    feat.reshape(N_FEAT, B, S, N_GATE), rfeat.reshape(B, S // LANES, N_ROWF, LANES),
                    tb=ROWS_MLSTM)
    wg_all, wu_all, wd_all, pg_all, pp_all = bf(w_gate), bf(w_up), bf(w_down), bf(ple_gate), bf(ple_proj)
    p_all = p.reshape(p.shape[0], T, PLE_DIM)
    h, *hns = _tail0(hf.reshape(T, -1), hb.reshape(T, -1), o, vec(a_head_norm[0]), x2, bf(a_w_out),
                     vec(norm_ffn[0]), wg_all, wu_all, wd_all, vec(norm_ple[0]), pg_all, p_all, pp_all,
                     vec(norm_mix[1]), tm=ROWS_TAIL, batch=B)

    inv_freq = ROPE_THETA ** (-jnp.arange(0, ROPE_DIM, 2, dtype=F32) / ROPE_DIM)
    invf_lane = jnp.tile(inv_freq, LANES // ROPE_HALF).reshape(1, LANES)
    pos4 = jnp.tile(jnp.repeat(positions.reshape(T // 4, 4), ROPE_HALF, axis=1), (1, 2))
    tabs = _rope_tables(pos4, invf_lane, tm=ROWS_ROPE, batch=B)
    w1 = bf(b_w_in[0])
    o_g, l_g = [], []
    for g, (_, dil) in enumerate(DILATED_GROUPS):
        proj = _in_proj1(hns[g].reshape(T, D_MODEL), w1, g, tabs[2 * g].reshape(T, LANES),
                         tabs[2 * g + 1].reshape(T, LANES), tm=ROWS_IN_PROJ1)
        og, lg = _attention(proj.reshape(B, dil, S // dil, 3 * D_MODEL), blkq=ROWS_ATTN)
        o_g.append(og)
        l_g.append(lg)
    out = _tail1(o_g[0].reshape(T, D_MODEL), o_g[1], o_g[2], l_g[0].reshape(T, LANES), l_g[1], l_g[2], h,
                 bf(b_w_out), vec(norm_ffn[1]), wg_all, wu_all, wd_all, vec(norm_ple[1]), pg_all, p_all, pp_all,
                 vec(final_norm), tm=ROWS_TAIL, batch=B)
    return out.reshape(B, S, D_MODEL)
```
